```python
import jax, jax.numpy as jnp
from jax import lax
import numpy as np

D_MODEL = 1024
BATCH = 8
SEQ = 2048
DEPTH = 2

MEM_LEN = 256
POOL_GROUPS = 4
POOL_GROUP_DIM = D_MODEL // 16
POOL_WIDTH = POOL_GROUPS * POOL_GROUP_DIM
POOL_WINDOWS = (2, 4, 8, 16)
FOX_HEADS = 8
FOX_HEAD_DIM = 64
FOX_WIDTH = FOX_HEADS * FOX_HEAD_DIM
Q_BLOCK = 128
SGU_GROUPS = 4
SGU_GROUP_DIM = D_MODEL // 16
SGU_WIDTH = SGU_GROUPS * SGU_GROUP_DIM
SGU_CHUNK = 128
N_BRANCH = 3
OFF_A = 0
OFF_Q = OFF_A + POOL_WIDTH
OFF_K = OFF_Q + FOX_WIDTH
OFF_V = OFF_K + FOX_WIDTH
OFF_F = OFF_V + FOX_WIDTH
OFF_C = OFF_F + FOX_HEADS
OFF_G = OFF_C + 2 * SGU_WIDTH
N_IN = OFF_G + N_BRANCH * D_MODEL
XATTN_HEADS = 4
XATTN_HEAD_DIM = D_MODEL // XATTN_HEADS
D_FF = 4 * D_MODEL
EPS = 1e-6
NEG = -1e30

kernel_name = "hybrid_pool_fox_sgu_gated_block"


def rmsnorm(x, g):
    xf = x.astype(jnp.float32)
    y = xf * lax.rsqrt(jnp.mean(xf * xf, axis=-1, keepdims=True) + EPS)
    return (y * g.astype(jnp.float32)).astype(x.dtype)


def pool_mixer(a, w, scale):
    B, S, _ = a.shape
    af = a.astype(jnp.float32)
    c = jnp.pad(jnp.cumsum(af, axis=1), ((0, 0), (1, 0), (0, 0)))
    t = jnp.arange(S)
    outs = []
    for gi, win in enumerate(POOL_WINDOWS):
        sl = slice(gi * POOL_GROUP_DIM, (gi + 1) * POOL_GROUP_DIM)
        cg = c[..., sl]
        lo = jnp.take(cg, jnp.maximum(t + 1 - win, 0), axis=1)
        cnt = jnp.minimum(t + 1, win).astype(jnp.float32)[None, :, None]
        outs.append((cg[:, 1:] - lo) / cnt - af[..., sl])
    d = jnp.stack(outs, axis=2).astype(a.dtype)
    y = jnp.einsum('bsgc,gcd->bsgd', d, w).reshape(B, S, POOL_WIDTH)
    return y * scale


def forgetting_attention(q, k, v, logf):
    S = q.shape[1]
    F = jnp.cumsum(logf, axis=1).transpose(0, 2, 1)
    scale = FOX_HEAD_DIM ** -0.5
    outs = []
    for i in range(S // Q_BLOCK):
        q0 = i * Q_BLOCK
        kend = q0 + Q_BLOCK
        s = jnp.einsum('bqhd,bkhd->bhqk', q[:, q0:kend], k[:, :kend]).astype(jnp.float32) * scale
        s = s + F[:, :, q0:kend, None] - F[:, :, None, :kend]
        mask = (q0 + jnp.arange(Q_BLOCK))[:, None] >= jnp.arange(kend)[None, :]
        s = jnp.where(mask, s, NEG)
        p = jax.nn.softmax(s, axis=-1).astype(v.dtype)
        outs.append(jnp.einsum('bhqk,bkhd->bqhd', p, v[:, :kend]))
    return jnp.concatenate(outs, axis=1)


def spatial_gating(z, norm_g, ws, b):
    B, S, _ = z.shape
    u, v = z[..., :SGU_WIDTH], z[..., SGU_WIDTH:]
    v = rmsnorm(v, norm_g)
    vc = v.reshape(B, S // SGU_CHUNK, SGU_CHUNK, SGU_GROUPS, SGU_GROUP_DIM)
    causal = jnp.tril(jnp.ones((SGU_CHUNK, SGU_CHUNK), dtype=ws.dtype))
    w = ws * causal[None]
    mixed = jnp.einsum('gts,bcsgd->bctgd', w, vc) + b.T[None, None, :, :, None]
    return u * mixed.reshape(B, S, SGU_WIDTH)


def setup_inputs(seed: int = 0) -> dict:
    key = jax.random.key(seed)
    ks = jax.random.split(key, 24)
    L, D = DEPTH, D_MODEL
    nrm = lambda k, shape, fan_in: jax.random.normal(k, shape, jnp.float32) * (fan_in ** -0.5)
    gain = lambda k, shape: 1.0 + 0.05 * jax.random.normal(k, shape, jnp.float32)
    b_forget = jnp.linspace(1.0, 6.0, FOX_HEADS, dtype=jnp.float32)[None, :] + 0.1 * jax.random.normal(ks[3], (L, FOX_HEADS), jnp.float32)
    return {
        "x": jax.random.normal(ks[0], (BATCH, SEQ, D), jnp.float32),
        "mem": jax.random.normal(ks[1], (BATCH, MEM_LEN, D), jnp.float32),
        "norm_mix_g": gain(ks[2], (L, D)),
        "w_in": nrm(ks[4], (L, D, N_IN), D),
        "b_forget": b_forget,
        "pool_w": nrm(ks[5], (L, POOL_GROUPS, POOL_GROUP_DIM, POOL_GROUP_DIM), POOL_GROUP_DIM),
        "pool_scale": gain(ks[6], (L, POOL_WIDTH)),
        "sgu_norm_g": gain(ks[7], (L, SGU_WIDTH)),
        "sgu_w": nrm(ks[8], (L, SGU_GROUPS, SGU_CHUNK, SGU_CHUNK), SGU_CHUNK),
        "sgu_b": gain(ks[9], (L, SGU_GROUPS, SGU_CHUNK)),
        "w_branch_a": nrm(ks[10], (L, POOL_WIDTH, D), POOL_WIDTH),
        "w_branch_b": nrm(ks[11], (L, FOX_WIDTH, D), FOX_WIDTH),
        "w_branch_c": nrm(ks[12], (L, SGU_WIDTH, D), SGU_WIDTH),
        "b_gate": 0.01 * jax.random.normal(ks[13], (L, N_BRANCH * D), jnp.float32),
        "w_out": nrm(ks[14], (L, D, D), D),
        "norm_xattn_g": gain(ks[15], (L, D)),
        "norm_mem_g": gain(ks[16], (L, D)),
        "w_xq": nrm(ks[17], (L, D, D), D),
        "w_xkv": nrm(ks[18], (L, D, 2 * D), D),
        "w_xo": nrm(ks[19], (L, D, D), D),
        "norm_ffn_g": gain(ks[20], (L, D)),
        "w_ff1": nrm(ks[21], (L, D, D_FF), D),
        "w_ff2": nrm(ks[22], (L, D_FF, D), D_FF),
        "final_norm_g": gain(ks[23], (D,)),
    }


def reference(x, mem, norm_mix_g, w_in, b_forget, pool_w, pool_scale, sgu_norm_g, sgu_w, sgu_b,
              w_branch_a, w_branch_b, w_branch_c, b_gate, w_out, norm_xattn_g, norm_mem_g,
              w_xq, w_xkv, w_xo, norm_ffn_g, w_ff1, w_ff2, final_norm_g):
    B, S, D = x.shape
    M = mem.shape[1]
    for l in range(DEPTH):
        h = rmsnorm(x, norm_mix_g[l])
        proj = h @ w_in[l]
        a = proj[..., OFF_A:OFF_Q]
        q = proj[..., OFF_Q:OFF_K].reshape(B, S, FOX_HEADS, FOX_HEAD_DIM)
        k = proj[..., OFF_K:OFF_V].reshape(B, S, FOX_HEADS, FOX_HEAD_DIM)
        v = proj[..., OFF_V:OFF_F].reshape(B, S, FOX_HEADS, FOX_HEAD_DIM)
        logf = jax.nn.log_sigmoid(proj[..., OFF_F:OFF_C].astype(jnp.float32) + b_forget[l].astype(jnp.float32))
        zc = jax.nn.gelu(proj[..., OFF_C:OFF_G])
        gates = jax.nn.sigmoid(proj[..., OFF_G:] + b_gate[l])

        y_a = pool_mixer(a, pool_w[l], pool_scale[l]) @ w_branch_a[l]
        y_b = forgetting_attention(q, k, v, logf).reshape(B, S, FOX_WIDTH) @ w_branch_b[l]
        y_c = spatial_gating(zc, sgu_norm_g[l], sgu_w[l], sgu_b[l]) @ w_branch_c[l]
        merged = gates[..., :D] * y_a + gates[..., D:2 * D] * y_b + gates[..., 2 * D:] * y_c
        x = x + merged @ w_out[l]

        hx = rmsnorm(x, norm_xattn_g[l])
        hm = rmsnorm(mem, norm_mem_g[l])
        xq = (hx @ w_xq[l]).reshape(B, S, XATTN_HEADS, XATTN_HEAD_DIM)
        kv = hm @ w_xkv[l]
        xk = kv[..., :D].reshape(B, M, XATTN_HEADS, XATTN_HEAD_DIM)
        xv = kv[..., D:].reshape(B, M, XATTN_HEADS, XATTN_HEAD_DIM)
        s = jnp.einsum('bqhd,bkhd->bhqk', xq, xk).astype(jnp.float32) * (XATTN_HEAD_DIM ** -0.5)
        p = jax.nn.softmax(s, axis=-1).astype(xv.dtype)
        o = jnp.einsum('bhqk,bkhd->bqhd', p, xv).reshape(B, S, D)
        x = x + o @ w_xo[l]

        hf = rmsnorm(x, norm_ffn_g[l])
        x = x + jnp.square(jax.nn.relu(hf @ w_ff1[l])) @ w_ff2[l]
    return rmsnorm(x, final_norm_g)
```

```python
import functools
import math

import numpy as np
import jax
import jax.numpy as jnp
from jax import lax
from jax.experimental import pallas as pl
from jax.experimental.pallas import tpu as pltpu

D_MODEL = 1024
DEPTH = 2
POOL_GROUP_DIM = 64
POOL_WIDTH = 256
POOL_WINDOWS = (2, 4, 8, 16)
POOL_HALO = 16
FOX_HEADS = 8
FOX_HEAD_DIM = 64
FOX_WIDTH = 512
SGU_WIDTH = 256
SGU_GROUPS = 4
SGU_CHUNK = 128
XATTN_HEADS = 4
XATTN_HEAD_DIM = 256
D_FF = 4096
EPS = 1e-6
NEG = -1e30
LOG2E = math.log2(math.e)

LANES = 128
AUG_LANES = 8
HEAD_PAIRS = FOX_HEADS // 2
PAIR_W = 2 * LANES

C_A = 0
C_Q = C_A + POOL_WIDTH
C_K = C_Q + FOX_WIDTH
C_V = C_K + FOX_WIDTH
C_C = C_V + FOX_WIDTH
C_F = C_C + 2 * SGU_WIDTH
N_MAIN = C_F + LANES

TM = 512
TQ = 256
VMEM_LIMIT = 56 * 1024 * 1024

_F32 = jnp.float32
_BF16 = jnp.bfloat16


def _dot(a, b):
    return jnp.dot(a, b, preferred_element_type=_F32)


def _dot_t(a, b):
    return lax.dot_general(a, b, (((1,), (1,)), ((), ())), preferred_element_type=_F32)


def _rms(x, g):
    ms = jnp.mean(x * x, axis=-1, keepdims=True)
    return x * lax.rsqrt(ms + EPS) * g


def _const_spec(shape):
    nd = len(shape)
    return pl.BlockSpec(shape, lambda *_: (0,) * nd, pipeline_mode=pl.Buffered(1))


def _params(sem):
    return pltpu.CompilerParams(dimension_semantics=sem, vmem_limit_bytes=VMEM_LIMIT)


def _mem_kv_kernel(mem_ref, g_ref, w_ref, xk_ref, xv_ref):
    hm = _rms(mem_ref[...], g_ref[...]).astype(_BF16)
    xk_ref[...] = _dot(hm, w_ref[:, :D_MODEL]).astype(_BF16)
    xv_ref[...] = _dot(hm, w_ref[:, D_MODEL:]).astype(_BF16)


def _mem_kv(mem2d, g, w_xkv):
    rows = mem2d.shape[0]
    return pl.pallas_call(
        _mem_kv_kernel,
        grid=(rows // TM,),
        in_specs=[pl.BlockSpec((TM, D_MODEL), lambda i: (i, 0)),
                  _const_spec((1, D_MODEL)),
                  _const_spec((D_MODEL, 2 * D_MODEL))],
        out_specs=[pl.BlockSpec((TM, D_MODEL), lambda i: (i, 0))] * 2,
        out_shape=[jax.ShapeDtypeStruct((rows, D_MODEL), _BF16)] * 2,
        compiler_params=_params(("arbitrary",)),
        name="mem_kv",
    )(mem2d, g, w_xkv)


def _mix_in_kernel(tiles_per_seq,
                   x_ref, g_ref, w_ref, bf_ref, wpool_ref, pscale_ref, sgug_ref, sguw_ref,
                   bmat_ref, e_ref, cq_ref, ck_ref,
                   pa_ref, q2_ref, k2_ref, v_ref, pc_ref,
                   halo_ref, fcarry_ref):
    jj = pl.program_id(0) % tiles_per_seq

    @pl.when(jj == 0)
    def _():
        halo_ref[...] = jnp.zeros_like(halo_ref)
        fcarry_ref[...] = jnp.zeros_like(fcarry_ref)

    h = _rms(x_ref[...], g_ref[...]).astype(_BF16)

    a = _dot(h, w_ref[:, C_A:C_Q])
    ext = jnp.concatenate([halo_ref[...], a], axis=0)
    halo_ref[...] = a[TM - POOL_HALO:, :]
    w2 = ext + pltpu.roll(ext, 1, 0)
    w4 = w2 + pltpu.roll(w2, 2, 0)
    w8 = w4 + pltpu.roll(w4, 4, 0)
    w16 = w8 + pltpu.roll(w8, 8, 0)
    grp = lax.broadcasted_iota(jnp.int32, (TM, POOL_WIDTH), 1) // POOL_GROUP_DIM
    wsel = jnp.where(grp == 0, w2[POOL_HALO:],
                     jnp.where(grp == 1, w4[POOL_HALO:],
                               jnp.where(grp == 2, w8[POOL_HALO:], w16[POOL_HALO:])))
    pos = jj * TM + lax.broadcasted_iota(jnp.int32, (TM, POOL_WIDTH), 0)
    win = jnp.left_shift(2, grp)
    cnt = jnp.minimum(pos + 1, win).astype(_F32)
    d = wsel / cnt - a
    pa = _dot(d.astype(_BF16), wpool_ref[...]) * pscale_ref[...]
    pa_ref[...] = pa.astype(_BF16)

    q = _dot(h, w_ref[:, C_Q:C_K]) * (FOX_HEAD_DIM ** -0.5 * LOG2E)
    k = _dot(h, w_ref[:, C_K:C_V])
    v_ref[...] = _dot(h, w_ref[:, C_V:C_C]).astype(_BF16)

    z = _dot(h, w_ref[:, C_F:N_MAIN]) + bf_ref[...]
    logf = jnp.minimum(z, 0.0) - jnp.log1p(jnp.exp(-jnp.abs(z)))
    row = lax.broadcasted_iota(jnp.int32, (TM, LANES), 0)
    c = logf
    shift = 1
    while shift < TM:
        c = c + jnp.where(row >= shift, pltpu.roll(c, shift, 0), 0.0)
        shift *= 2
    fcum = (c + fcarry_ref[...]) * LOG2E
    fcarry_ref[...] = fcarry_ref[...] + jnp.sum(logf, axis=0, keepdims=True)
    hi = fcum.astype(_BF16).astype(_F32)
    r1 = fcum - hi
    mid = r1.astype(_BF16).astype(_F32)
    lo = r1 - mid
    lane = lax.broadcasted_iota(jnp.int32, (TM, LANES), 1)
    packed = jnp.where(lane < FOX_HEADS, hi,
                       jnp.where(lane < 2 * FOX_HEADS, pltpu.roll(mid, FOX_HEADS, 1),
                                 jnp.where(lane < 3 * FOX_HEADS, pltpu.roll(lo, 2 * FOX_HEADS, 1), 0.0)))
    aug = _dot(packed.astype(_BF16), e_ref[...])
    aq = (aug[:, :LANES] + cq_ref[...]).astype(_BF16)
    ak = (aug[:, LANES:] + ck_ref[...]).astype(_BF16)
    qb = q.astype(_BF16)
    kb = k.astype(_BF16)
    for p in range(HEAD_PAIRS):
        q2_ref[:, p * PAIR_W:p * PAIR_W + LANES] = qb[:, p * LANES:(p + 1) * LANES]
        q2_ref[:, p * PAIR_W + LANES:(p + 1) * PAIR_W] = aq
        k2_ref[:, p * PAIR_W:p * PAIR_W + LANES] = kb[:, p * LANES:(p + 1) * LANES]
        k2_ref[:, p * PAIR_W + LANES:(p + 1) * PAIR_W] = ak

    cz = _dot(h, w_ref[:, C_C:C_F])
    zc = cz * (0.5 * (1.0 + jnp.tanh(math.sqrt(2.0 / math.pi) * (cz + 0.044715 * (cz * cz * cz)))))
    u = zc[:, :SGU_WIDTH]
    vv = _rms(zc[:, SGU_WIDTH:], sgug_ref[...])
    wrow = lax.broadcasted_iota(jnp.int32, (SGU_GROUPS * SGU_CHUNK, SGU_CHUNK), 0) % SGU_CHUNK
    wcol = lax.broadcasted_iota(jnp.int32, (SGU_GROUPS * SGU_CHUNK, SGU_CHUNK), 1)
    wst = jnp.where(wcol <= wrow, sguw_ref[...], 0.0).astype(_BF16)
    cgrp = lax.broadcasted_iota(jnp.int32, (SGU_CHUNK, SGU_WIDTH), 1) // (SGU_WIDTH // SGU_GROUPS)
    for ci in range(TM // SGU_CHUNK):
        rs = slice(ci * SGU_CHUNK, (ci + 1) * SGU_CHUNK)
        r = _dot(wst, vv[rs].astype(_BF16))
        mixed = jnp.where(cgrp == 0, r[0:SGU_CHUNK],
                          jnp.where(cgrp == 1, r[SGU_CHUNK:2 * SGU_CHUNK],
                                    jnp.where(cgrp == 2, r[2 * SGU_CHUNK:3 * SGU_CHUNK],
                                              r[3 * SGU_CHUNK:])))
        pc_ref[rs, :] = (u[rs] * (mixed + bmat_ref[...])).astype(_BF16)


def _mix_in(x2d, seq, g, w_main, bf, wpool, pscale, sgug, sguw, bmat, e, cq, ck):
    t = x2d.shape[0]
    row_spec = lambda w: pl.BlockSpec((TM, w), lambda i: (i, 0))
    return pl.pallas_call(
        functools.partial(_mix_in_kernel, seq // TM),
        grid=(t // TM,),
        in_specs=[row_spec(D_MODEL),
                  _const_spec((1, D_MODEL)),
                  _const_spec((D_MODEL, N_MAIN)),
                  _const_spec((1, LANES)),
                  _const_spec((POOL_WIDTH, POOL_WIDTH)),
                  _const_spec((1, POOL_WIDTH)),
                  _const_spec((1, SGU_WIDTH)),
                  _const_spec((SGU_GROUPS * SGU_CHUNK, SGU_CHUNK)),
                  _const_spec((SGU_CHUNK, SGU_WIDTH)),
                  _const_spec((LANES, 2 * LANES)),
                  _const_spec((1, LANES)),
                  _const_spec((1, LANES))],
        out_specs=[row_spec(POOL_WIDTH), row_spec(HEAD_PAIRS * PAIR_W), row_spec(HEAD_PAIRS * PAIR_W),
                   row_spec(FOX_WIDTH), row_spec(SGU_WIDTH)],
        out_shape=[jax.ShapeDtypeStruct((t, POOL_WIDTH), _BF16),
                   jax.ShapeDtypeStruct((t, HEAD_PAIRS * PAIR_W), _BF16),
                   jax.ShapeDtypeStruct((t, HEAD_PAIRS * PAIR_W), _BF16),
                   jax.ShapeDtypeStruct((t, FOX_WIDTH), _BF16),
                   jax.ShapeDtypeStruct((t, SGU_WIDTH), _BF16)],
        scratch_shapes=[pltpu.VMEM((POOL_HALO, POOL_WIDTH), _F32),
                        pltpu.VMEM((1, LANES), _F32)],
        compiler_params=_params(("arbitrary",)),
        name="mix_in",
    )(x2d, g, w_main, bf, wpool, pscale, sgug, sguw, bmat, e, cq, ck)


def _fox_kernel(seq, q2_ref, k2_ref, v_ref, o_ref, s_ref, p_ref):
    pair = pl.program_id(1)
    lane = lax.broadcasted_iota(jnp.int32, (TQ, PAIR_W), 1)
    keep = []
    for hh in range(2):
        aug_lo = LANES + (2 * pair + hh) * AUG_LANES
        keep.append(((lane >= hh * FOX_HEAD_DIM) & (lane < (hh + 1) * FOX_HEAD_DIM))
                    | ((lane >= aug_lo) & (lane < aug_lo + AUG_LANES)))
    drow = lax.broadcasted_iota(jnp.int32, (2 * TQ, TQ), 0) % TQ
    dcol = lax.broadcasted_iota(jnp.int32, (2 * TQ, TQ), 1)
    causal = dcol <= drow
    out_lane = lax.broadcasted_iota(jnp.int32, (TQ, LANES), 1)

    for i in range(seq // TQ):
        kend = (i + 1) * TQ
        qb = q2_ref[i * TQ:kend, :]
        zero = jnp.zeros_like(qb)
        qm = jnp.concatenate([jnp.where(keep[0], qb, zero), jnp.where(keep[1], qb, zero)], axis=0)
        s_ref[:, :kend] = _dot_t(qm, k2_ref[:kend, :])

        mrun = jnp.full((2 * TQ, LANES), NEG, _F32)
        for c in range(i + 1):
            sc = s_ref[:, c * TQ:(c + 1) * TQ]
            if c == i:
                sc = jnp.where(causal, sc, NEG)
                s_ref[:, c * TQ:(c + 1) * TQ] = sc
            mrun = jnp.maximum(mrun, jnp.maximum(sc[:, :LANES], sc[:, LANES:]))
        mb = jnp.broadcast_to(jnp.max(mrun, axis=1, keepdims=True), (2 * TQ, LANES))

        lrun = jnp.zeros((2 * TQ, LANES), _F32)
        for c in range(i + 1):
            p0 = jnp.exp2(s_ref[:, c * TQ:c * TQ + LANES] - mb)
            p1 = jnp.exp2(s_ref[:, c * TQ + LANES:(c + 1) * TQ] - mb)
            lrun = lrun + (p0 + p1)
            p_ref[:, c * TQ:c * TQ + LANES] = p0.astype(_BF16)
            p_ref[:, c * TQ + LANES:(c + 1) * TQ] = p1.astype(_BF16)
        linv = 1.0 / jnp.sum(lrun, axis=1, keepdims=True)

        o = _dot(p_ref[:, :kend], v_ref[:kend, :]) * linv
        o_ref[i * TQ:kend, :] = jnp.where(out_lane < FOX_HEAD_DIM, o[:TQ], o[TQ:]).astype(_BF16)


def _fox(q2, k2, v):
    b, seq, _ = q2.shape
    return pl.pallas_call(
        functools.partial(_fox_kernel, seq),
        grid=(b, HEAD_PAIRS),
        in_specs=[pl.BlockSpec((None, seq, PAIR_W), lambda bi, p: (bi, 0, p)),
                  pl.BlockSpec((None, seq, PAIR_W), lambda bi, p: (bi, 0, p)),
                  pl.BlockSpec((None, seq, LANES), lambda bi, p: (bi, 0, p))],
        out_specs=pl.BlockSpec((None, seq, LANES), lambda bi, p: (bi, 0, p)),
        out_shape=jax.ShapeDtypeStruct((b, seq, FOX_WIDTH), _BF16),
        scratch_shapes=[pltpu.VMEM((2 * TQ, seq), _F32),
                        pltpu.VMEM((2 * TQ, seq), _BF16)],
        compiler_params=_params(("arbitrary", "arbitrary")),
        name="fox_attention",
    )(q2, k2, v)


def _merge_xattn_kernel(x_ref, pa_ref, att_ref, pc_ref, gmix_ref, wg_ref, bg_ref,
                        wba_ref, wbb_ref, wbc_ref, wout_ref, gx_ref, wxq_ref,
                        xk_ref, xv_ref, wxo_ref, o_ref):
    x = x_ref[...]
    h = _rms(x, gmix_ref[...]).astype(_BF16)

    def gated(branch_ref, wb_ref, j):
        cols = slice(j * D_MODEL, (j + 1) * D_MODEL)
        gate = 1.0 / (1.0 + jnp.exp(-(_dot(h, wg_ref[:, cols]) + bg_ref[:, cols])))
        return gate * _dot(branch_ref[...], wb_ref[...])

    merged = gated(pa_ref, wba_ref, 0) + gated(att_ref, wbb_ref, 1) + gated(pc_ref, wbc_ref, 2)
    x1 = x + _dot(merged.astype(_BF16), wout_ref[...])

    hx = _rms(x1, gx_ref[...]).astype(_BF16)
    xq = (_dot(hx, wxq_ref[...]) * (XATTN_HEAD_DIM ** -0.5 * LOG2E)).astype(_BF16)
    heads = []
    for hd in range(XATTN_HEADS):
        cols = slice(hd * XATTN_HEAD_DIM, (hd + 1) * XATTN_HEAD_DIM)
        s = _dot_t(xq[:, cols], xk_ref[:, cols])
        p = jnp.exp2(s - jnp.max(s, axis=1, keepdims=True))
        linv = 1.0 / jnp.sum(p, axis=1, keepdims=True)
        heads.append((_dot(p.astype(_BF16), xv_ref[:, cols]) * linv).astype(_BF16))
    o = jnp.concatenate(heads, axis=1)
    o_ref[...] = x1 + _dot(o, wxo_ref[...])


def _merge_xattn(x2d, seq, pa, att, pc, gmix, w_gate, b_gate, wba, wbb, wbc, w_out, gx, w_xq, xk, xv, w_xo):
    t = x2d.shape[0]
    mlen = xk.shape[1]
    tiles_per_seq = seq // TM
    row_spec = lambda w: pl.BlockSpec((TM, w), lambda i: (i, 0))
    mem_spec = pl.BlockSpec((None, mlen, D_MODEL), lambda i: (i // tiles_per_seq, 0, 0))
    return pl.pallas_call(
        _merge_xattn_kernel,
        grid=(t // TM,),
        in_specs=[row_spec(D_MODEL), row_spec(POOL_WIDTH), row_spec(FOX_WIDTH), row_spec(SGU_WIDTH),
                  _const_spec((1, D_MODEL)),
                  _const_spec((D_MODEL, 3 * D_MODEL)),
                  _const_spec((1, 3 * D_MODEL)),
                  _const_spec((POOL_WIDTH, D_MODEL)),
                  _const_spec((FOX_WIDTH, D_MODEL)),
                  _const_spec((SGU_WIDTH, D_MODEL)),
                  _const_spec((D_MODEL, D_MODEL)),
                  _const_spec((1, D_MODEL)),
                  _const_spec((D_MODEL, D_MODEL)),
                  mem_spec, mem_spec,
                  _const_spec((D_MODEL, D_MODEL))],
        out_specs=row_spec(D_MODEL),
        out_shape=jax.ShapeDtypeStruct((t, D_MODEL), _F32),
        compiler_params=_params(("arbitrary",)),
        name="merge_xattn",
    )(x2d, pa, att, pc, gmix, w_gate, b_gate, wba, wbb, wbc, w_out, gx, w_xq, xk, xv, w_xo)


FF_CHUNK = 1024


def _ffn_kernel(final_norm, x_ref, g_ref, w1_ref, w2_ref, gfin_ref, o_ref, act_ref):
    x = x_ref[...]
    hf = _rms(x, g_ref[...]).astype(_BF16)
    for c in range(D_FF // FF_CHUNK):
        cols = slice(c * FF_CHUNK, (c + 1) * FF_CHUNK)
        r = jnp.maximum(_dot(hf, w1_ref[:, cols]), 0.0)
        act_ref[:, cols] = (r * r).astype(_BF16)
    y = x + _dot(act_ref[...], w2_ref[...])
    if final_norm:
        y = _rms(y, gfin_ref[...])
    o_ref[...] = y


def _ffn(x2d, g, w1, w2, gfin, final_norm):
    t = x2d.shape[0]
    return pl.pallas_call(
        functools.partial(_ffn_kernel, final_norm),
        grid=(t // TM,),
        in_specs=[pl.BlockSpec((TM, D_MODEL), lambda i: (i, 0)),
                  _const_spec((1, D_MODEL)),
                  _const_spec((D_MODEL, D_FF)),
                  _const_spec((D_FF, D_MODEL)),
                  _const_spec((1, D_MODEL))],
        out_specs=pl.BlockSpec((TM, D_MODEL), lambda i: (i, 0)),
        out_shape=jax.ShapeDtypeStruct((t, D_MODEL), _F32),
        scratch_shapes=[pltpu.VMEM((TM, D_FF), _BF16)],
        compiler_params=_params(("arbitrary",)),
        name="ffn",
    )(x2d, g, w1, w2, gfin)


def _aug_constants():
    e = np.zeros((LANES, 2 * LANES), np.float32)
    cq = np.zeros((1, LANES), np.float32)
    ck = np.zeros((1, LANES), np.float32)
    for h in range(FOX_HEADS):
        for piece in range(3):
            e[piece * FOX_HEADS + h, h * AUG_LANES + piece] = 1.0
            e[piece * FOX_HEADS + h, LANES + h * AUG_LANES + 3 + piece] = -1.0
            cq[0, h * AUG_LANES + 3 + piece] = 1.0
            ck[0, h * AUG_LANES + piece] = 1.0
    return jnp.asarray(e, _BF16), jnp.asarray(cq), jnp.asarray(ck)


def kernel(x, mem, norm_mix_g, w_in, b_forget, pool_w, pool_scale, sgu_norm_g, sgu_w, sgu_b,
           w_branch_a, w_branch_b, w_branch_c, b_gate, w_out, norm_xattn_g, norm_mem_g,
           w_xq, w_xkv, w_xo, norm_ffn_g, w_ff1, w_ff2, final_norm_g):
    b, seq, d = x.shape
    mlen = mem.shape[1]
    t = b * seq
    off_q = POOL_WIDTH
    off_f = off_q + 3 * FOX_WIDTH
    off_c = off_f + FOX_HEADS
    off_g = off_c + 2 * SGU_WIDTH
    e, cq, ck = _aug_constants()
    row = lambda a: a.reshape(1, -1)

    xc = x.reshape(t, d)
    mem2d = mem.reshape(b * mlen, d)
    for l in range(DEPTH):
        wl = w_in[l]
        w_main = jnp.concatenate(
            [wl[:, :off_f], wl[:, off_c:off_g],
             jnp.pad(wl[:, off_f:off_c], ((0, 0), (0, LANES - FOX_HEADS)))], axis=1).astype(_BF16)
        w_gate = wl[:, off_g:].astype(_BF16)
        bf = jnp.pad(b_forget[l], (0, LANES - FOX_HEADS)).reshape(1, LANES)
        wpool = jax.scipy.linalg.block_diag(*[pool_w[l, g] for g in range(len(POOL_WINDOWS))]).astype(_BF16)
        sguw = sgu_w[l].reshape(SGU_GROUPS * SGU_CHUNK, SGU_CHUNK)
        bmat = jnp.repeat(sgu_b[l].T, SGU_WIDTH // SGU_GROUPS, axis=1)

        xk, xv = _mem_kv(mem2d, row(norm_mem_g[l]), w_xkv[l].astype(_BF16))
        pa, q2, k2, v, pc = _mix_in(xc, seq, row(norm_mix_g[l]), w_main, bf, wpool, row(pool_scale[l]),
                                    row(sgu_norm_g[l]), sguw, bmat, e, cq, ck)
        att = _fox(q2.reshape(b, seq, -1), k2.reshape(b, seq, -1), v.reshape(b, seq, -1))
        xc = _merge_xattn(xc, seq, pa, att.reshape(t, -1), pc, row(norm_mix_g[l]), w_gate, row(b_gate[l]),
                          w_branch_a[l].astype(_BF16), w_branch_b[l].astype(_BF16),
                          w_branch_c[l].astype(_BF16), w_out[l].astype(_BF16), row(norm_xattn_g[l]),
                          w_xq[l].astype(_BF16), xk.reshape(b, mlen, d), xv.reshape(b, mlen, d),
                          w_xo[l].astype(_BF16))
        xc = _ffn(xc, row(norm_ffn_g[l]), w_ff1[l].astype(_BF16), w_ff2[l].astype(_BF16),
                  row(final_norm_g), l == DEPTH - 1)
    return xc.reshape(b, seq, d)
```

```python
import functools
import math

import numpy as np
import jax
import jax.numpy as jnp
from jax import lax
from jax.experimental import pallas as pl
from jax.experimental.pallas import tpu as pltpu

D_MODEL = 1024
DEPTH = 2
POOL_GROUP_DIM = 64
POOL_WIDTH = 256
POOL_WINDOWS = (2, 4, 8, 16)
POOL_HALO = 16
FOX_HEADS = 8
FOX_HEAD_DIM = 64
FOX_WIDTH = 512
SGU_WIDTH = 256
SGU_GROUPS = 4
SGU_CHUNK = 128
XATTN_HEADS = 4
XATTN_HEAD_DIM = 256
D_FF = 4096
EPS = 1e-6
NEG = -1e30
LOG2E = math.log2(math.e)

LANES = 128
AUG_LANES = 8
HEAD_PAIRS = FOX_HEADS // 2
PAIR_W = 2 * LANES

C_A = 0
C_Q = C_A + POOL_WIDTH
C_K = C_Q + FOX_WIDTH
C_V = C_K + FOX_WIDTH
C_C = C_V + FOX_WIDTH
C_F = C_C + 2 * SGU_WIDTH
N_MAIN = C_F + LANES

TM = 512
TQ = 256
VMEM_LIMIT = 56 * 1024 * 1024

_F32 = jnp.float32
_BF16 = jnp.bfloat16


def _dot(a, b):
    return jnp.dot(a, b, preferred_element_type=_F32)


def _dot_t(a, b):
    return lax.dot_general(a, b, (((1,), (1,)), ((), ())), preferred_element_type=_F32)


def _rms(x, g):
    ms = jnp.mean(x * x, axis=-1, keepdims=True)
    return x * lax.rsqrt(ms + EPS) * g


def _const_spec(shape):
    nd = len(shape)
    return pl.BlockSpec(shape, lambda *_: (0,) * nd, pipeline_mode=pl.Buffered(1))


def _layer_spec(tail, l):
    nd = len(tail)
    return pl.BlockSpec((None,) + tuple(tail), lambda *_: (l,) + (0,) * nd, pipeline_mode=pl.Buffered(1))


def _params(sem):
    return pltpu.CompilerParams(dimension_semantics=sem, vmem_limit_bytes=VMEM_LIMIT)


def _mem_kv_kernel(mem_ref, g_ref, w_ref, xk_ref, xv_ref):
    hm = _rms(mem_ref[...], g_ref[...]).astype(_BF16)
    xk_ref[...] = _dot(hm, w_ref[:, :D_MODEL].astype(_BF16)).astype(_BF16)
    xv_ref[...] = _dot(hm, w_ref[:, D_MODEL:].astype(_BF16)).astype(_BF16)


def _mem_kv(l, mem2d, g, w_xkv):
    rows = mem2d.shape[0]
    return pl.pallas_call(
        _mem_kv_kernel,
        grid=(rows // TM,),
        in_specs=[pl.BlockSpec((TM, D_MODEL), lambda i: (i, 0)),
                  _layer_spec((1, D_MODEL), l),
                  _layer_spec((D_MODEL, 2 * D_MODEL), l)],
        out_specs=[pl.BlockSpec((TM, D_MODEL), lambda i: (i, 0))] * 2,
        out_shape=[jax.ShapeDtypeStruct((rows, D_MODEL), _BF16)] * 2,
        compiler_params=_params(("arbitrary",)),
        name="mem_kv",
    )(mem2d, g, w_xkv)


def _mix_in_kernel(tiles_per_seq,
                   x_ref, g_ref, w_ref, bf_ref, wpool_ref, pscale_ref, sgug_ref, sguw_ref,
                   bmat_ref, e_ref, cq_ref, ck_ref,
                   pa_ref, q2_ref, k2_ref, v_ref, pc_ref,
                   halo_ref, fcarry_ref):
    jj = pl.program_id(0) % tiles_per_seq

    @pl.when(jj == 0)
    def _():
        halo_ref[...] = jnp.zeros_like(halo_ref)
        fcarry_ref[...] = jnp.zeros_like(fcarry_ref)

    h = _rms(x_ref[...], g_ref[...]).astype(_BF16)

    a = _dot(h, w_ref[:, C_A:C_Q])
    ext = jnp.concatenate([halo_ref[...], a], axis=0)
    halo_ref[...] = a[TM - POOL_HALO:, :]
    w2 = ext + pltpu.roll(ext, 1, 0)
    w4 = w2 + pltpu.roll(w2, 2, 0)
    w8 = w4 + pltpu.roll(w4, 4, 0)
    w16 = w8 + pltpu.roll(w8, 8, 0)
    grp = lax.broadcasted_iota(jnp.int32, (TM, POOL_WIDTH), 1) // POOL_GROUP_DIM
    wsel = jnp.where(grp == 0, w2[POOL_HALO:],
                     jnp.where(grp == 1, w4[POOL_HALO:],
                               jnp.where(grp == 2, w8[POOL_HALO:], w16[POOL_HALO:])))
    pos = jj * TM + lax.broadcasted_iota(jnp.int32, (TM, POOL_WIDTH), 0)
    win = jnp.left_shift(2, grp)
    cnt = jnp.minimum(pos + 1, win).astype(_F32)
    d = wsel / cnt - a
    pa = _dot(d.astype(_BF16), wpool_ref[...]) * pscale_ref[...]
    pa_ref[...] = pa.astype(_BF16)

    q = _dot(h, w_ref[:, C_Q:C_K]) * (FOX_HEAD_DIM ** -0.5 * LOG2E)
    k = _dot(h, w_ref[:, C_K:C_V])
    v_ref[...] = _dot(h, w_ref[:, C_V:C_C]).astype(_BF16)

    z = _dot(h, w_ref[:, C_F:N_MAIN]) + bf_ref[...]
    logf = jnp.minimum(z, 0.0) - jnp.log1p(jnp.exp(-jnp.abs(z)))
    row = lax.broadcasted_iota(jnp.int32, (TM, LANES), 0)
    c = logf
    shift = 1
    while shift < TM:
        c = c + jnp.where(row >= shift, pltpu.roll(c, shift, 0), 0.0)
        shift *= 2
    fcum = (c + fcarry_ref[...]) * LOG2E
    fcarry_ref[...] = fcarry_ref[...] + jnp.sum(logf, axis=0, keepdims=True)
    hi = fcum.astype(_BF16).astype(_F32)
    r1 = fcum - hi
    mid = r1.astype(_BF16).astype(_F32)
    lo = r1 - mid
    lane = lax.broadcasted_iota(jnp.int32, (TM, LANES), 1)
    packed = jnp.where(lane < FOX_HEADS, hi,
                       jnp.where(lane < 2 * FOX_HEADS, pltpu.roll(mid, FOX_HEADS, 1),
                                 jnp.where(lane < 3 * FOX_HEADS, pltpu.roll(lo, 2 * FOX_HEADS, 1), 0.0)))
    aug = _dot(packed.astype(_BF16), e_ref[...])
    aq = (aug[:, :LANES] + cq_ref[...]).astype(_BF16)
    ak = (aug[:, LANES:] + ck_ref[...]).astype(_BF16)
    qb = q.astype(_BF16)
    kb = k.astype(_BF16)
    for p in range(HEAD_PAIRS):
        q2_ref[:, p * PAIR_W:p * PAIR_W + LANES] = qb[:, p * LANES:(p + 1) * LANES]
        q2_ref[:, p * PAIR_W + LANES:(p + 1) * PAIR_W] = aq
        k2_ref[:, p * PAIR_W:p * PAIR_W + LANES] = kb[:, p * LANES:(p + 1) * LANES]
        k2_ref[:, p * PAIR_W + LANES:(p + 1) * PAIR_W] = ak

    cz = _dot(h, w_ref[:, C_C:C_F])
    zc = cz * (0.5 * (1.0 + jnp.tanh(math.sqrt(2.0 / math.pi) * (cz + 0.044715 * (cz * cz * cz)))))
    u = zc[:, :SGU_WIDTH]
    vv = _rms(zc[:, SGU_WIDTH:], sgug_ref[...])
    wrow = lax.broadcasted_iota(jnp.int32, (SGU_GROUPS * SGU_CHUNK, SGU_CHUNK), 0) % SGU_CHUNK
    wcol = lax.broadcasted_iota(jnp.int32, (SGU_GROUPS * SGU_CHUNK, SGU_CHUNK), 1)
    wst = jnp.where(wcol <= wrow, sguw_ref[...], 0.0).astype(_BF16)
    cgrp = lax.broadcasted_iota(jnp.int32, (SGU_CHUNK, SGU_WIDTH), 1) // (SGU_WIDTH // SGU_GROUPS)
    for ci in range(TM // SGU_CHUNK):
        rs = slice(ci * SGU_CHUNK, (ci + 1) * SGU_CHUNK)
        r = _dot(wst, vv[rs].astype(_BF16))
        mixed = jnp.where(cgrp == 0, r[0:SGU_CHUNK],
                          jnp.where(cgrp == 1, r[SGU_CHUNK:2 * SGU_CHUNK],
                                    jnp.where(cgrp == 2, r[2 * SGU_CHUNK:3 * SGU_CHUNK],
                                              r[3 * SGU_CHUNK:])))
        pc_ref[rs, :] = (u[rs] * (mixed + bmat_ref[...])).astype(_BF16)


def _mix_in(l, x2d, seq, g, w_main, bf, wpool, pscale, sgug, sguw, bmat, e, cq, ck):
    t = x2d.shape[0]
    row_spec = lambda w: pl.BlockSpec((TM, w), lambda i: (i, 0))
    return pl.pallas_call(
        functools.partial(_mix_in_kernel, seq // TM),
        grid=(t // TM,),
        in_specs=[row_spec(D_MODEL),
                  _layer_spec((1, D_MODEL), l),
                  _layer_spec((D_MODEL, N_MAIN), l),
                  _layer_spec((1, LANES), l),
                  _layer_spec((POOL_WIDTH, POOL_WIDTH), l),
                  _layer_spec((1, POOL_WIDTH), l),
                  _layer_spec((1, SGU_WIDTH), l),
                  _layer_spec((SGU_GROUPS * SGU_CHUNK, SGU_CHUNK), l),
                  _layer_spec((SGU_CHUNK, SGU_WIDTH), l),
                  _const_spec((LANES, 2 * LANES)),
                  _const_spec((1, LANES)),
                  _const_spec((1, LANES))],
        out_specs=[row_spec(POOL_WIDTH), row_spec(HEAD_PAIRS * PAIR_W), row_spec(HEAD_PAIRS * PAIR_W),
                   row_spec(FOX_WIDTH), row_spec(SGU_WIDTH)],
        out_shape=[jax.ShapeDtypeStruct((t, POOL_WIDTH), _BF16),
                   jax.ShapeDtypeStruct((t, HEAD_PAIRS * PAIR_W), _BF16),
                   jax.ShapeDtypeStruct((t, HEAD_PAIRS * PAIR_W), _BF16),
                   jax.ShapeDtypeStruct((t, FOX_WIDTH), _BF16),
                   jax.ShapeDtypeStruct((t, SGU_WIDTH), _BF16)],
        scratch_shapes=[pltpu.VMEM((POOL_HALO, POOL_WIDTH), _F32),
                        pltpu.VMEM((1, LANES), _F32)],
        compiler_params=_params(("arbitrary",)),
        name="mix_in",
    )(x2d, g, w_main, bf, wpool, pscale, sgug, sguw, bmat, e, cq, ck)


def _fox_kernel(seq, q2_ref, k2_ref, v_ref, o_ref, s_ref, p_ref):
    pair = pl.program_id(1)
    lane = lax.broadcasted_iota(jnp.int32, (TQ, PAIR_W), 1)
    keep = []
    for hh in range(2):
        aug_lo = LANES + (2 * pair + hh) * AUG_LANES
        keep.append(((lane >= hh * FOX_HEAD_DIM) & (lane < (hh + 1) * FOX_HEAD_DIM))
                    | ((lane >= aug_lo) & (lane < aug_lo + AUG_LANES)))
    drow = lax.broadcasted_iota(jnp.int32, (2 * TQ, TQ), 0) % TQ
    dcol = lax.broadcasted_iota(jnp.int32, (2 * TQ, TQ), 1)
    causal = dcol <= drow
    out_lane = lax.broadcasted_iota(jnp.int32, (TQ, LANES), 1)

    for i in range(seq // TQ):
        kend = (i + 1) * TQ
        qb = q2_ref[i * TQ:kend, :]
        zero = jnp.zeros_like(qb)
        qm = jnp.concatenate([jnp.where(keep[0], qb, zero), jnp.where(keep[1], qb, zero)], axis=0)
        s_ref[:, :kend] = _dot_t(qm, k2_ref[:kend, :])

        mrun = jnp.full((2 * TQ, LANES), NEG, _F32)
        for c in range(i + 1):
            sc = s_ref[:, c * TQ:(c + 1) * TQ]
            if c == i:
                sc = jnp.where(causal, sc, NEG)
                s_ref[:, c * TQ:(c + 1) * TQ] = sc
            mrun = jnp.maximum(mrun, jnp.maximum(sc[:, :LANES], sc[:, LANES:]))
        mb = jnp.broadcast_to(jnp.max(mrun, axis=1, keepdims=True), (2 * TQ, LANES))

        lrun = jnp.zeros((2 * TQ, LANES), _F32)
        for c in range(i + 1):
            p0 = jnp.exp2(s_ref[:, c * TQ:c * TQ + LANES] - mb)
            p1 = jnp.exp2(s_ref[:, c * TQ + LANES:(c + 1) * TQ] - mb)
            lrun = lrun + (p0 + p1)
            p_ref[:, c * TQ:c * TQ + LANES] = p0.astype(_BF16)
            p_ref[:, c * TQ + LANES:(c + 1) * TQ] = p1.astype(_BF16)
        linv = 1.0 / jnp.sum(lrun, axis=1, keepdims=True)

        o = _dot(p_ref[:, :kend], v_ref[:kend, :]) * linv
        o_ref[i * TQ:kend, :] = jnp.where(out_lane < FOX_HEAD_DIM, o[:TQ], o[TQ:]).astype(_BF16)


def _fox(q2, k2, v):
    b, seq, _ = q2.shape
    return pl.pallas_call(
        functools.partial(_fox_kernel, seq),
        grid=(b, HEAD_PAIRS),
        in_specs=[pl.BlockSpec((None, seq, PAIR_W), lambda bi, p: (bi, 0, p)),
                  pl.BlockSpec((None, seq, PAIR_W), lambda bi, p: (bi, 0, p)),
                  pl.BlockSpec((None, seq, LANES), lambda bi, p: (bi, 0, p))],
        out_specs=pl.BlockSpec((None, seq, LANES), lambda bi, p: (bi, 0, p)),
        out_shape=jax.ShapeDtypeStruct((b, seq, FOX_WIDTH), _BF16),
        scratch_shapes=[pltpu.VMEM((2 * TQ, seq), _F32),
                        pltpu.VMEM((2 * TQ, seq), _BF16)],
        compiler_params=_params(("arbitrary", "arbitrary")),
        name="fox_attention",
    )(q2, k2, v)


def _merge_xattn_kernel(x_ref, pa_ref, att_ref, pc_ref, gmix_ref, wg_ref, bg_ref,
                        wba_ref, wbb_ref, wbc_ref, wout_ref, gx_ref, wxq_ref,
                        xk_ref, xv_ref, wxo_ref, o_ref):
    x = x_ref[...]
    h = _rms(x, gmix_ref[...]).astype(_BF16)

    def gated(branch_ref, wb_ref, j):
        cols = slice(j * D_MODEL, (j + 1) * D_MODEL)
        gate = 1.0 / (1.0 + jnp.exp(-(_dot(h, wg_ref[:, cols]) + bg_ref[:, cols])))
        return gate * _dot(branch_ref[...], wb_ref[...].astype(_BF16))

    merged = gated(pa_ref, wba_ref, 0) + gated(att_ref, wbb_ref, 1) + gated(pc_ref, wbc_ref, 2)
    x1 = x + _dot(merged.astype(_BF16), wout_ref[...].astype(_BF16))

    hx = _rms(x1, gx_ref[...]).astype(_BF16)
    xq = (_dot(hx, wxq_ref[...].astype(_BF16)) * (XATTN_HEAD_DIM ** -0.5 * LOG2E)).astype(_BF16)
    heads = []
    for hd in range(XATTN_HEADS):
        cols = slice(hd * XATTN_HEAD_DIM, (hd + 1) * XATTN_HEAD_DIM)
        s = _dot_t(xq[:, cols], xk_ref[:, cols])
        p = jnp.exp2(s - jnp.max(s, axis=1, keepdims=True))
        linv = 1.0 / jnp.sum(p, axis=1, keepdims=True)
        heads.append((_dot(p.astype(_BF16), xv_ref[:, cols]) * linv).astype(_BF16))
    o = jnp.concatenate(heads, axis=1)
    o_ref[...] = x1 + _dot(o, wxo_ref[...].astype(_BF16))


def _merge_xattn(l, x2d, seq, pa, att, pc, gmix, w_gate, b_gate, wba, wbb, wbc, w_out, gx, w_xq, xk, xv, w_xo):
    t = x2d.shape[0]
    mlen = xk.shape[1]
    tiles_per_seq = seq // TM
    row_spec = lambda w: pl.BlockSpec((TM, w), lambda i: (i, 0))
    mem_spec = pl.BlockSpec((None, mlen, D_MODEL), lambda i: (i // tiles_per_seq, 0, 0))
    return pl.pallas_call(
        _merge_xattn_kernel,
        grid=(t // TM,),
        in_specs=[row_spec(D_MODEL), row_spec(POOL_WIDTH), row_spec(FOX_WIDTH), row_spec(SGU_WIDTH),
                  _layer_spec((1, D_MODEL), l),
                  _layer_spec((D_MODEL, 3 * D_MODEL), l),
                  _layer_spec((1, 3 * D_MODEL), l),
                  _layer_spec((POOL_WIDTH, D_MODEL), l),
                  _layer_spec((FOX_WIDTH, D_MODEL), l),
                  _layer_spec((SGU_WIDTH, D_MODEL), l),
                  _layer_spec((D_MODEL, D_MODEL), l),
                  _layer_spec((1, D_MODEL), l),
                  _layer_spec((D_MODEL, D_MODEL), l),
                  mem_spec, mem_spec,
                  _layer_spec((D_MODEL, D_MODEL), l)],
        out_specs=row_spec(D_MODEL),
        out_shape=jax.ShapeDtypeStruct((t, D_MODEL), _F32),
        compiler_params=_params(("arbitrary",)),
        name="merge_xattn",
    )(x2d, pa, att, pc, gmix, w_gate, b_gate, wba, wbb, wbc, w_out, gx, w_xq, xk, xv, w_xo)


FF_CHUNK = 1024


def _ffn_kernel(final_norm, x_ref, g_ref, w1_ref, w2_ref, gfin_ref, o_ref, act_ref):
    x = x_ref[...]
    hf = _rms(x, g_ref[...]).astype(_BF16)
    for c in range(D_FF // FF_CHUNK):
        cols = slice(c * FF_CHUNK, (c + 1) * FF_CHUNK)
        r = jnp.maximum(_dot(hf, w1_ref[:, cols].astype(_BF16)), 0.0)
        act_ref[:, cols] = (r * r).astype(_BF16)
    y = x + _dot(act_ref[...], w2_ref[...].astype(_BF16))
    if final_norm:
        y = _rms(y, gfin_ref[...])
    o_ref[...] = y


def _ffn(l, x2d, g, w1, w2, gfin, final_norm):
    t = x2d.shape[0]
    return pl.pallas_call(
        functools.partial(_ffn_kernel, final_norm),
        grid=(t // TM,),
        in_specs=[pl.BlockSpec((TM, D_MODEL), lambda i: (i, 0)),
                  _layer_spec((1, D_MODEL), l),
                  _layer_spec((D_MODEL, D_FF), l),
                  _layer_spec((D_FF, D_MODEL), l),
                  _const_spec((1, D_MODEL))],
        out_specs=pl.BlockSpec((TM, D_MODEL), lambda i: (i, 0)),
        out_shape=jax.ShapeDtypeStruct((t, D_MODEL), _F32),
        scratch_shapes=[pltpu.VMEM((TM, D_FF), _BF16)],
        compiler_params=_params(("arbitrary",)),
        name="ffn",
    )(x2d, g, w1, w2, gfin)


def _aug_constants():
    e = np.zeros((LANES, 2 * LANES), np.float32)
    cq = np.zeros((1, LANES), np.float32)
    ck = np.zeros((1, LANES), np.float32)
    for h in range(FOX_HEADS):
        for piece in range(3):
            e[piece * FOX_HEADS + h, h * AUG_LANES + piece] = 1.0
            e[piece * FOX_HEADS + h, LANES + h * AUG_LANES + 3 + piece] = -1.0
            cq[0, h * AUG_LANES + 3 + piece] = 1.0
            ck[0, h * AUG_LANES + piece] = 1.0
    return jnp.asarray(e, _BF16), jnp.asarray(cq), jnp.asarray(ck)


def kernel(x, mem, norm_mix_g, w_in, b_forget, pool_w, pool_scale, sgu_norm_g, sgu_w, sgu_b,
           w_branch_a, w_branch_b, w_branch_c, b_gate, w_out, norm_xattn_g, norm_mem_g,
           w_xq, w_xkv, w_xo, norm_ffn_g, w_ff1, w_ff2, final_norm_g):
    b, seq, d = x.shape
    mlen = mem.shape[1]
    t = b * seq
    off_q = POOL_WIDTH
    off_f = off_q + 3 * FOX_WIDTH
    off_c = off_f + FOX_HEADS
    off_g = off_c + 2 * SGU_WIDTH
    e, cq, ck = _aug_constants()
    rows = lambda a: a.reshape(a.shape[0], 1, -1)
    n_groups = len(POOL_WINDOWS)

    w_main = jnp.concatenate(
        [w_in[:, :, :off_f], w_in[:, :, off_c:off_g],
         jnp.pad(w_in[:, :, off_f:off_c], ((0, 0), (0, 0), (0, LANES - FOX_HEADS)))], axis=2).astype(_BF16)
    w_gate = w_in[:, :, off_g:].astype(_BF16)
    bf = rows(jnp.pad(b_forget, ((0, 0), (0, LANES - FOX_HEADS))))
    eye = jnp.eye(n_groups, dtype=pool_w.dtype)
    wpool = jnp.einsum('lgcd,gh->lgchd', pool_w, eye).reshape(DEPTH, POOL_WIDTH, POOL_WIDTH).astype(_BF16)
    sguw = sgu_w.reshape(DEPTH, SGU_GROUPS * SGU_CHUNK, SGU_CHUNK)
    bmat = jnp.repeat(jnp.swapaxes(sgu_b, 1, 2), SGU_WIDTH // SGU_GROUPS, axis=2)
    g_mix, g_mem, g_x, g_ffn = rows(norm_mix_g), rows(norm_mem_g), rows(norm_xattn_g), rows(norm_ffn_g)
    pscale, sgug, bg = rows(pool_scale), rows(sgu_norm_g), rows(b_gate)

    xc = x.reshape(t, d)
    mem2d = mem.reshape(b * mlen, d)
    for l in range(DEPTH):
        xk, xv = _mem_kv(l, mem2d, g_mem, w_xkv)
        pa, q2, k2, v, pc = _mix_in(l, xc, seq, g_mix, w_main, bf, wpool, pscale, sgug, sguw, bmat, e, cq, ck)
        att = _fox(q2.reshape(b, seq, -1), k2.reshape(b, seq, -1), v.reshape(b, seq, -1))
        xc = _merge_xattn(l, xc, seq, pa, att.reshape(t, -1), pc, g_mix, w_gate, bg,
                          w_branch_a, w_branch_b, w_branch_c, w_out, g_x, w_xq,
                          xk.reshape(b, mlen, d), xv.reshape(b, mlen, d), w_xo)
        xc = _ffn(l, xc, g_ffn, w_ff1, w_ff2, final_norm_g.reshape(1, d), l == DEPTH - 1)
    return xc.reshape(b, seq, d)
```

```python
import functools
import math

import numpy as np
import jax
import jax.numpy as jnp
from jax import lax
from jax.experimental import pallas as pl
from jax.experimental.pallas import tpu as pltpu

D_MODEL = 1024
DEPTH = 2
POOL_GROUP_DIM = 64
POOL_WIDTH = 256
POOL_WINDOWS = (2, 4, 8, 16)
POOL_HALO = 16
FOX_HEADS = 8
FOX_HEAD_DIM = 64
FOX_WIDTH = 512
SGU_WIDTH = 256
SGU_GROUPS = 4
SGU_CHUNK = 128
XATTN_HEADS = 4
XATTN_HEAD_DIM = 256
D_FF = 4096
EPS = 1e-6
NEG = -1e30
LOG2E = math.log2(math.e)

LANES = 128
AUG_LANES = 8
QNORM_LANE = 6
KNORM_LANE = 7
NORM_CAP = 1e30
HEAD_PAIRS = FOX_HEADS // 2
PAIR_W = 2 * LANES

C_A = 0
C_Q = C_A + POOL_WIDTH
C_K = C_Q + FOX_WIDTH
C_V = C_K + FOX_WIDTH
C_C = C_V + FOX_WIDTH
C_F = C_C + 2 * SGU_WIDTH
N_MAIN = C_F + LANES

TM = 512
TQ = 256
VMEM_LIMIT = 56 * 1024 * 1024

_F32 = jnp.float32
_BF16 = jnp.bfloat16


def _dot(a, b):
    return jnp.dot(a, b, preferred_element_type=_F32)


def _dot_t(a, b):
    return lax.dot_general(a, b, (((1,), (1,)), ((), ())), preferred_element_type=_F32)


def _rms(x, g):
    ms = jnp.mean(x * x, axis=-1, keepdims=True)
    return x * lax.rsqrt(ms + EPS) * g


def _const_spec(shape):
    nd = len(shape)
    return pl.BlockSpec(shape, lambda *_: (0,) * nd, pipeline_mode=pl.Buffered(1))


def _layer_spec(tail, l):
    nd = len(tail)
    return pl.BlockSpec((None,) + tuple(tail), lambda *_: (l,) + (0,) * nd, pipeline_mode=pl.Buffered(1))


def _params(sem):
    return pltpu.CompilerParams(dimension_semantics=sem, vmem_limit_bytes=VMEM_LIMIT)


def _mem_kv_kernel(mem_ref, g_ref, w_ref, xk_ref, xv_ref):
    hm = _rms(mem_ref[...], g_ref[...]).astype(_BF16)
    xk_ref[...] = _dot(hm, w_ref[:, :D_MODEL].astype(_BF16)).astype(_BF16)
    xv_ref[...] = _dot(hm, w_ref[:, D_MODEL:].astype(_BF16)).astype(_BF16)


def _mem_kv(l, mem2d, g, w_xkv):
    rows = mem2d.shape[0]
    return pl.pallas_call(
        _mem_kv_kernel,
        grid=(rows // TM,),
        in_specs=[pl.BlockSpec((TM, D_MODEL), lambda i: (i, 0)),
                  _layer_spec((1, D_MODEL), l),
                  _layer_spec((D_MODEL, 2 * D_MODEL), l)],
        out_specs=[pl.BlockSpec((TM, D_MODEL), lambda i: (i, 0))] * 2,
        out_shape=[jax.ShapeDtypeStruct((rows, D_MODEL), _BF16)] * 2,
        compiler_params=_params(("arbitrary",)),
        name="mem_kv",
    )(mem2d, g, w_xkv)


def _mix_in_kernel(tiles_per_seq,
                   x_ref, g_ref, w_ref, bf_ref, wpool_ref, pscale_ref, sgug_ref, sguw_ref,
                   bmat_ref, e_ref, en_ref, cq_ref, ck_ref,
                   pa_ref, q2_ref, k2_ref, v_ref, pc_ref,
                   halo_ref, fcarry_ref):
    jj = pl.program_id(0) % tiles_per_seq

    @pl.when(jj == 0)
    def _():
        halo_ref[...] = jnp.zeros_like(halo_ref)
        fcarry_ref[...] = jnp.zeros_like(fcarry_ref)

    h = _rms(x_ref[...], g_ref[...]).astype(_BF16)

    a = _dot(h, w_ref[:, C_A:C_Q])
    ext = jnp.concatenate([halo_ref[...], a], axis=0)
    halo_ref[...] = a[TM - POOL_HALO:, :]
    w2 = ext + pltpu.roll(ext, 1, 0)
    w4 = w2 + pltpu.roll(w2, 2, 0)
    w8 = w4 + pltpu.roll(w4, 4, 0)
    w16 = w8 + pltpu.roll(w8, 8, 0)
    grp = lax.broadcasted_iota(jnp.int32, (TM, POOL_WIDTH), 1) // POOL_GROUP_DIM
    wsel = jnp.where(grp == 0, w2[POOL_HALO:],
                     jnp.where(grp == 1, w4[POOL_HALO:],
                               jnp.where(grp == 2, w8[POOL_HALO:], w16[POOL_HALO:])))
    pos = jj * TM + lax.broadcasted_iota(jnp.int32, (TM, POOL_WIDTH), 0)
    win = jnp.left_shift(2, grp)
    cnt = jnp.minimum(pos + 1, win).astype(_F32)
    d = wsel / cnt - a
    pa = _dot(d.astype(_BF16), wpool_ref[...]) * pscale_ref[...]
    pa_ref[...] = pa.astype(_BF16)

    q = _dot(h, w_ref[:, C_Q:C_K]) * (FOX_HEAD_DIM ** -0.5 * LOG2E)
    k = _dot(h, w_ref[:, C_K:C_V])
    v_ref[...] = _dot(h, w_ref[:, C_V:C_C]).astype(_BF16)

    z = _dot(h, w_ref[:, C_F:N_MAIN]) + bf_ref[...]
    logf = jnp.minimum(z, 0.0) - jnp.log1p(jnp.exp(-jnp.abs(z)))
    row = lax.broadcasted_iota(jnp.int32, (TM, LANES), 0)
    c = logf
    shift = 1
    while shift < TM:
        c = c + jnp.where(row >= shift, pltpu.roll(c, shift, 0), 0.0)
        shift *= 2
    fcum = (c + fcarry_ref[...]) * LOG2E
    fcarry_ref[...] = fcarry_ref[...] + jnp.sum(logf, axis=0, keepdims=True)
    hi = fcum.astype(_BF16).astype(_F32)
    r1 = fcum - hi
    mid = r1.astype(_BF16).astype(_F32)
    lo = r1 - mid
    lane = lax.broadcasted_iota(jnp.int32, (TM, LANES), 1)
    packed = jnp.where(lane < FOX_HEADS, hi,
                       jnp.where(lane < 2 * FOX_HEADS, pltpu.roll(mid, FOX_HEADS, 1),
                                 jnp.where(lane < 3 * FOX_HEADS, pltpu.roll(lo, 2 * FOX_HEADS, 1), 0.0)))
    aug = _dot(packed.astype(_BF16), e_ref[...])
    nq = jnp.minimum(_dot((q * q).astype(_BF16), en_ref[:, :LANES]), NORM_CAP)
    nk = jnp.minimum(_dot((k * k).astype(_BF16), en_ref[:, LANES:]), NORM_CAP)
    aq = (aug[:, :LANES] + cq_ref[...] + nq).astype(_BF16)
    ak = (aug[:, LANES:] + ck_ref[...] + nk).astype(_BF16)
    qb = q.astype(_BF16)
    kb = k.astype(_BF16)
    for p in range(HEAD_PAIRS):
        q2_ref[:, p * PAIR_W:p * PAIR_W + LANES] = qb[:, p * LANES:(p + 1) * LANES]
        q2_ref[:, p * PAIR_W + LANES:(p + 1) * PAIR_W] = aq
        k2_ref[:, p * PAIR_W:p * PAIR_W + LANES] = kb[:, p * LANES:(p + 1) * LANES]
        k2_ref[:, p * PAIR_W + LANES:(p + 1) * PAIR_W] = ak

    cz = _dot(h, w_ref[:, C_C:C_F])
    zc = cz * (0.5 * (1.0 + jnp.tanh(math.sqrt(2.0 / math.pi) * (cz + 0.044715 * (cz * cz * cz)))))
    u = zc[:, :SGU_WIDTH]
    vv = _rms(zc[:, SGU_WIDTH:], sgug_ref[...])
    wrow = lax.broadcasted_iota(jnp.int32, (SGU_GROUPS * SGU_CHUNK, SGU_CHUNK), 0) % SGU_CHUNK
    wcol = lax.broadcasted_iota(jnp.int32, (SGU_GROUPS * SGU_CHUNK, SGU_CHUNK), 1)
    wst = jnp.where(wcol <= wrow, sguw_ref[...], 0.0).astype(_BF16)
    cgrp = lax.broadcasted_iota(jnp.int32, (SGU_CHUNK, SGU_WIDTH), 1) // (SGU_WIDTH // SGU_GROUPS)
    for ci in range(TM // SGU_CHUNK):
        rs = slice(ci * SGU_CHUNK, (ci + 1) * SGU_CHUNK)
        r = _dot(wst, vv[rs].astype(_BF16))
        mixed = jnp.where(cgrp == 0, r[0:SGU_CHUNK],
                          jnp.where(cgrp == 1, r[SGU_CHUNK:2 * SGU_CHUNK],
                                    jnp.where(cgrp == 2, r[2 * SGU_CHUNK:3 * SGU_CHUNK],
                                              r[3 * SGU_CHUNK:])))
        pc_ref[rs, :] = (u[rs] * (mixed + bmat_ref[...])).astype(_BF16)


def _mix_in(l, x2d, seq, g, w_main, bf, wpool, pscale, sgug, sguw, bmat, e, en, cq, ck):
    t = x2d.shape[0]
    row_spec = lambda w: pl.BlockSpec((TM, w), lambda i: (i, 0))
    return pl.pallas_call(
        functools.partial(_mix_in_kernel, seq // TM),
        grid=(t // TM,),
        in_specs=[row_spec(D_MODEL),
                  _layer_spec((1, D_MODEL), l),
                  _layer_spec((D_MODEL, N_MAIN), l),
                  _layer_spec((1, LANES), l),
                  _layer_spec((POOL_WIDTH, POOL_WIDTH), l),
                  _layer_spec((1, POOL_WIDTH), l),
                  _layer_spec((1, SGU_WIDTH), l),
                  _layer_spec((SGU_GROUPS * SGU_CHUNK, SGU_CHUNK), l),
                  _layer_spec((SGU_CHUNK, SGU_WIDTH), l),
                  _const_spec((LANES, 2 * LANES)),
                  _const_spec((FOX_WIDTH, 2 * LANES)),
                  _const_spec((1, LANES)),
                  _const_spec((1, LANES))],
        out_specs=[row_spec(POOL_WIDTH), row_spec(HEAD_PAIRS * PAIR_W), row_spec(HEAD_PAIRS * PAIR_W),
                   row_spec(FOX_WIDTH), row_spec(SGU_WIDTH)],
        out_shape=[jax.ShapeDtypeStruct((t, POOL_WIDTH), _BF16),
                   jax.ShapeDtypeStruct((t, HEAD_PAIRS * PAIR_W), _BF16),
                   jax.ShapeDtypeStruct((t, HEAD_PAIRS * PAIR_W), _BF16),
                   jax.ShapeDtypeStruct((t, FOX_WIDTH), _BF16),
                   jax.ShapeDtypeStruct((t, SGU_WIDTH), _BF16)],
        scratch_shapes=[pltpu.VMEM((POOL_HALO, POOL_WIDTH), _F32),
                        pltpu.VMEM((1, LANES), _F32)],
        compiler_params=_params(("arbitrary",)),
        name="mix_in",
    )(x2d, g, w_main, bf, wpool, pscale, sgug, sguw, bmat, e, en, cq, ck)


FOX_LOGIT_BOUND = 64.0
NORM_SLACK = 1.1


def _aug_colmax(ref):
    return jnp.max(ref[:, LANES:].astype(_F32), axis=0, keepdims=True)


def _lane_scalar(row, lane_idx):
    lane = lax.broadcasted_iota(jnp.int32, row.shape, 1)
    return jnp.max(jnp.where(lane == lane_idx, row, 0.0))


def _fox_kernel(seq, q2_ref, k2_ref, v_ref, o_ref, s_ref, p_ref, vext_ref):
    pair = pl.program_id(1)
    lane = lax.broadcasted_iota(jnp.int32, (TQ, PAIR_W), 1)
    keep = []
    for hh in range(2):
        aug_lo = LANES + (2 * pair + hh) * AUG_LANES
        keep.append(((lane >= hh * FOX_HEAD_DIM) & (lane < (hh + 1) * FOX_HEAD_DIM))
                    | ((lane >= aug_lo) & (lane < aug_lo + AUG_LANES)))
    drow = lax.broadcasted_iota(jnp.int32, (2 * TQ, TQ), 0) % TQ
    dcol = lax.broadcasted_iota(jnp.int32, (2 * TQ, TQ), 1)
    causal = dcol <= drow
    out_lane = lax.broadcasted_iota(jnp.int32, (TQ, LANES), 1)

    def stacked_q(i):
        qb = q2_ref[i * TQ:(i + 1) * TQ, :]
        zero = jnp.zeros_like(qb)
        return jnp.concatenate([jnp.where(keep[0], qb, zero), jnp.where(keep[1], qb, zero)], axis=0)

    def store_heads(i, o):
        o_ref[i * TQ:(i + 1) * TQ, :] = jnp.where(out_lane < FOX_HEAD_DIM, o[:TQ], o[TQ:]).astype(_BF16)

    bound2 = FOX_LOGIT_BOUND * FOX_LOGIT_BOUND / NORM_SLACK
    qmax, kmax = _aug_colmax(q2_ref), _aug_colmax(k2_ref)
    head_ok = [_lane_scalar(qmax, (2 * pair + hh) * AUG_LANES + QNORM_LANE)
               * _lane_scalar(kmax, (2 * pair + hh) * AUG_LANES + KNORM_LANE) <= bound2 for hh in range(2)]
    bounded = jnp.logical_and(head_ok[0], head_ok[1])

    @pl.when(bounded)
    def _():
        vext_ref[:, :LANES] = v_ref[...]
        vext_ref[:, LANES:] = jnp.ones((seq, LANES), _BF16)
        for i in range(seq // TQ):
            kend = (i + 1) * TQ
            qm = stacked_q(i)
            for c in range(i + 1):
                sc = _dot_t(qm, k2_ref[c * TQ:(c + 1) * TQ, :])
                if c == i:
                    sc = jnp.where(causal, sc, NEG)
                p_ref[:, c * TQ:(c + 1) * TQ] = jnp.exp2(sc).astype(_BF16)
            o = _dot(p_ref[:, :kend], vext_ref[:kend, :])
            store_heads(i, o[:, :LANES] * (1.0 / o[:, LANES:]))

    @pl.when(jnp.logical_not(bounded))
    def _():
        for i in range(seq // TQ):
            kend = (i + 1) * TQ
            s_ref[:, :kend] = _dot_t(stacked_q(i), k2_ref[:kend, :])

            mrun = jnp.full((2 * TQ, LANES), NEG, _F32)
            for c in range(i + 1):
                sc = s_ref[:, c * TQ:(c + 1) * TQ]
                if c == i:
                    sc = jnp.where(causal, sc, NEG)
                    s_ref[:, c * TQ:(c + 1) * TQ] = sc
                mrun = jnp.maximum(mrun, jnp.maximum(sc[:, :LANES], sc[:, LANES:]))
            mb = jnp.broadcast_to(jnp.max(mrun, axis=1, keepdims=True), (2 * TQ, LANES))

            lrun = jnp.zeros((2 * TQ, LANES), _F32)
            for c in range(i + 1):
                p0 = jnp.exp2(s_ref[:, c * TQ:c * TQ + LANES] - mb)
                p1 = jnp.exp2(s_ref[:, c * TQ + LANES:(c + 1) * TQ] - mb)
                lrun = lrun + (p0 + p1)
                p_ref[:, c * TQ:c * TQ + LANES] = p0.astype(_BF16)
                p_ref[:, c * TQ + LANES:(c + 1) * TQ] = p1.astype(_BF16)
            linv = 1.0 / jnp.sum(lrun, axis=1, keepdims=True)
            store_heads(i, _dot(p_ref[:, :kend], v_ref[:kend, :]) * linv)


def _fox(q2, k2, v):
    b, seq, _ = q2.shape
    return pl.pallas_call(
        functools.partial(_fox_kernel, seq),
        grid=(b, HEAD_PAIRS),
        in_specs=[pl.BlockSpec((None, seq, PAIR_W), lambda bi, p: (bi, 0, p)),
                  pl.BlockSpec((None, seq, PAIR_W), lambda bi, p: (bi, 0, p)),
                  pl.BlockSpec((None, seq, LANES), lambda bi, p: (bi, 0, p))],
        out_specs=pl.BlockSpec((None, seq, LANES), lambda bi, p: (bi, 0, p)),
        out_shape=jax.ShapeDtypeStruct((b, seq, FOX_WIDTH), _BF16),
        scratch_shapes=[pltpu.VMEM((2 * TQ, seq), _F32),
                        pltpu.VMEM((2 * TQ, seq), _BF16),
                        pltpu.VMEM((seq, 2 * LANES), _BF16)],
        compiler_params=_params(("arbitrary", "arbitrary")),
        name="fox_attention",
    )(q2, k2, v)


def _merge_xattn_kernel(x_ref, pa_ref, att_ref, pc_ref, gmix_ref, wg_ref, bg_ref,
                        wba_ref, wbb_ref, wbc_ref, wout_ref, gx_ref, wxq_ref,
                        xk_ref, xv_ref, wxo_ref, o_ref):
    x = x_ref[...]
    h = _rms(x, gmix_ref[...]).astype(_BF16)

    def gated(branch_ref, wb_ref, j):
        cols = slice(j * D_MODEL, (j + 1) * D_MODEL)
        gate = 1.0 / (1.0 + jnp.exp(-(_dot(h, wg_ref[:, cols]) + bg_ref[:, cols])))
        return gate * _dot(branch_ref[...], wb_ref[...].astype(_BF16))

    merged = gated(pa_ref, wba_ref, 0) + gated(att_ref, wbb_ref, 1) + gated(pc_ref, wbc_ref, 2)
    x1 = x + _dot(merged.astype(_BF16), wout_ref[...].astype(_BF16))

    hx = _rms(x1, gx_ref[...]).astype(_BF16)
    xq = (_dot(hx, wxq_ref[...].astype(_BF16)) * (XATTN_HEAD_DIM ** -0.5 * LOG2E)).astype(_BF16)
    heads = []
    for hd in range(XATTN_HEADS):
        cols = slice(hd * XATTN_HEAD_DIM, (hd + 1) * XATTN_HEAD_DIM)
        s = _dot_t(xq[:, cols], xk_ref[:, cols])
        p = jnp.exp2(s - jnp.max(s, axis=1, keepdims=True))
        linv = 1.0 / jnp.sum(p, axis=1, keepdims=True)
        heads.append((_dot(p.astype(_BF16), xv_ref[:, cols]) * linv).astype(_BF16))
    o = jnp.concatenate(heads, axis=1)
    o_ref[...] = x1 + _dot(o, wxo_ref[...].astype(_BF16))


def _merge_xattn(l, x2d, seq, pa, att, pc, gmix, w_gate, b_gate, wba, wbb, wbc, w_out, gx, w_xq, xk, xv, w_xo):
    t = x2d.shape[0]
    mlen = xk.shape[1]
    tiles_per_seq = seq // TM
    row_spec = lambda w: pl.BlockSpec((TM, w), lambda i: (i, 0))
    mem_spec = pl.BlockSpec((None, mlen, D_MODEL), lambda i: (i // tiles_per_seq, 0, 0))
    return pl.pallas_call(
        _merge_xattn_kernel,
        grid=(t // TM,),
        in_specs=[row_spec(D_MODEL), row_spec(POOL_WIDTH), row_spec(FOX_WIDTH), row_spec(SGU_WIDTH),
                  _layer_spec((1, D_MODEL), l),
                  _layer_spec((D_MODEL, 3 * D_MODEL), l),
                  _layer_spec((1, 3 * D_MODEL), l),
                  _layer_spec((POOL_WIDTH, D_MODEL), l),
                  _layer_spec((FOX_WIDTH, D_MODEL), l),
                  _layer_spec((SGU_WIDTH, D_MODEL), l),
                  _layer_spec((D_MODEL, D_MODEL), l),
                  _layer_spec((1, D_MODEL), l),
                  _layer_spec((D_MODEL, D_MODEL), l),
                  mem_spec, mem_spec,
                  _layer_spec((D_MODEL, D_MODEL), l)],
        out_specs=row_spec(D_MODEL),
        out_shape=jax.ShapeDtypeStruct((t, D_MODEL), _F32),
        compiler_params=_params(("arbitrary",)),
        name="merge_xattn",
    )(x2d, pa, att, pc, gmix, w_gate, b_gate, wba, wbb, wbc, w_out, gx, w_xq, xk, xv, w_xo)


FF_CHUNK = 1024


def _ffn_kernel(final_norm, x_ref, g_ref, w1_ref, w2_ref, gfin_ref, o_ref, act_ref):
    x = x_ref[...]
    hf = _rms(x, g_ref[...]).astype(_BF16)
    for c in range(D_FF // FF_CHUNK):
        cols = slice(c * FF_CHUNK, (c + 1) * FF_CHUNK)
        r = jnp.maximum(_dot(hf, w1_ref[:, cols].astype(_BF16)), 0.0)
        act_ref[:, cols] = (r * r).astype(_BF16)
    y = x + _dot(act_ref[...], w2_ref[...].astype(_BF16))
    if final_norm:
        y = _rms(y, gfin_ref[...])
    o_ref[...] = y


def _ffn(l, x2d, g, w1, w2, gfin, final_norm):
    t = x2d.shape[0]
    return pl.pallas_call(
        functools.partial(_ffn_kernel, final_norm),
        grid=(t // TM,),
        in_specs=[pl.BlockSpec((TM, D_MODEL), lambda i: (i, 0)),
                  _layer_spec((1, D_MODEL), l),
                  _layer_spec((D_MODEL, D_FF), l),
                  _layer_spec((D_FF, D_MODEL), l),
                  _const_spec((1, D_MODEL))],
        out_specs=pl.BlockSpec((TM, D_MODEL), lambda i: (i, 0)),
        out_shape=jax.ShapeDtypeStruct((t, D_MODEL), _F32),
        scratch_shapes=[pltpu.VMEM((TM, D_FF), _BF16)],
        compiler_params=_params(("arbitrary",)),
        name="ffn",
    )(x2d, g, w1, w2, gfin)


def _aug_constants():
    e = np.zeros((LANES, 2 * LANES), np.float32)
    en = np.zeros((FOX_WIDTH, 2 * LANES), np.float32)
    cq = np.zeros((1, LANES), np.float32)
    ck = np.zeros((1, LANES), np.float32)
    for h in range(FOX_HEADS):
        for piece in range(3):
            e[piece * FOX_HEADS + h, h * AUG_LANES + piece] = 1.0
            e[piece * FOX_HEADS + h, LANES + h * AUG_LANES + 3 + piece] = -1.0
            cq[0, h * AUG_LANES + 3 + piece] = 1.0
            ck[0, h * AUG_LANES + piece] = 1.0
        cols = slice(h * FOX_HEAD_DIM, (h + 1) * FOX_HEAD_DIM)
        en[cols, h * AUG_LANES + QNORM_LANE] = 1.0
        en[cols, LANES + h * AUG_LANES + KNORM_LANE] = 1.0
    return jnp.asarray(e, _BF16), jnp.asarray(en, _BF16), jnp.asarray(cq), jnp.asarray(ck)


def kernel(x, mem, norm_mix_g, w_in, b_forget, pool_w, pool_scale, sgu_norm_g, sgu_w, sgu_b,
           w_branch_a, w_branch_b, w_branch_c, b_gate, w_out, norm_xattn_g, norm_mem_g,
           w_xq, w_xkv, w_xo, norm_ffn_g, w_ff1, w_ff2, final_norm_g):
    b, seq, d = x.shape
    mlen = mem.shape[1]
    t = b * seq
    off_q = POOL_WIDTH
    off_f = off_q + 3 * FOX_WIDTH
    off_c = off_f + FOX_HEADS
    off_g = off_c + 2 * SGU_WIDTH
    e, en, cq, ck = _aug_constants()
    rows = lambda a: a.reshape(a.shape[0], 1, -1)
    n_groups = len(POOL_WINDOWS)

    w_main = jnp.concatenate(
        [w_in[:, :, :off_f], w_in[:, :, off_c:off_g],
         jnp.pad(w_in[:, :, off_f:off_c], ((0, 0), (0, 0), (0, LANES - FOX_HEADS)))], axis=2).astype(_BF16)
    w_gate = w_in[:, :, off_g:].astype(_BF16)
    bf = rows(jnp.pad(b_forget, ((0, 0), (0, LANES - FOX_HEADS))))
    eye = jnp.eye(n_groups, dtype=pool_w.dtype)
    wpool = jnp.einsum('lgcd,gh->lgchd', pool_w, eye).reshape(DEPTH, POOL_WIDTH, POOL_WIDTH).astype(_BF16)
    sguw = sgu_w.reshape(DEPTH, SGU_GROUPS * SGU_CHUNK, SGU_CHUNK)
    bmat = jnp.repeat(jnp.swapaxes(sgu_b, 1, 2), SGU_WIDTH // SGU_GROUPS, axis=2)
    g_mix, g_mem, g_x, g_ffn = rows(norm_mix_g), rows(norm_mem_g), rows(norm_xattn_g), rows(norm_ffn_g)
    pscale, sgug, bg = rows(pool_scale), rows(sgu_norm_g), rows(b_gate)

    xc = x.reshape(t, d)
    mem2d = mem.reshape(b * mlen, d)
    for l in range(DEPTH):
        xk, xv = _mem_kv(l, mem2d, g_mem, w_xkv)
        pa, q2, k2, v, pc = _mix_in(l, xc, seq, g_mix, w_main, bf, wpool, pscale, sgug, sguw, bmat, e, en, cq, ck)
        att = _fox(q2.reshape(b, seq, -1), k2.reshape(b, seq, -1), v.reshape(b, seq, -1))
        xc = _merge_xattn(l, xc, seq, pa, att.reshape(t, -1), pc, g_mix, w_gate, bg,
                          w_branch_a, w_branch_b, w_branch_c, w_out, g_x, w_xq,
                          xk.reshape(b, mlen, d), xv.reshape(b, mlen, d), w_xo)
        xc = _ffn(l, xc, g_ffn, w_ff1, w_ff2, final_norm_g.reshape(1, d), l == DEPTH - 1)
    return xc.reshape(b, seq, d)
```

```python
import functools
import math

import numpy as np
import jax
import jax.numpy as jnp
from jax import lax
from jax.experimental import pallas as pl
from jax.experimental.pallas import tpu as pltpu

D_MODEL = 1024
DEPTH = 2
POOL_GROUP_DIM = 64
POOL_WIDTH = 256
POOL_WINDOWS = (2, 4, 8, 16)
POOL_HALO = 16
FOX_HEADS = 8
FOX_HEAD_DIM = 64
FOX_WIDTH = 512
SGU_WIDTH = 256
SGU_GROUPS = 4
SGU_CHUNK = 128
XATTN_HEADS = 4
XATTN_HEAD_DIM = 256
D_FF = 4096
EPS = 1e-6
NEG = -1e30
LOG2E = math.log2(math.e)

LANES = 128
AUG_LANES = 8
QNORM_LANE = 6
KNORM_LANE = 7
NORM_CAP = 1e30
HEAD_PAIRS = FOX_HEADS // 2
PAIR_W = 2 * LANES

C_A = 0
C_Q = C_A + POOL_WIDTH
C_K = C_Q + FOX_WIDTH
C_V = C_K + FOX_WIDTH
C_C = C_V + FOX_WIDTH
C_F = C_C + 2 * SGU_WIDTH
N_MAIN = C_F + LANES

TM = 512
TQ = 256
VMEM_LIMIT = 56 * 1024 * 1024

_F32 = jnp.float32
_BF16 = jnp.bfloat16


def _dot(a, b):
    return jnp.dot(a, b, preferred_element_type=_F32)


def _dot_t(a, b):
    return lax.dot_general(a, b, (((1,), (1,)), ((), ())), preferred_element_type=_F32)


def _rms(x, g):
    ms = jnp.mean(x * x, axis=-1, keepdims=True)
    return x * lax.rsqrt(ms + EPS) * g


def _const_spec(shape):
    nd = len(shape)
    return pl.BlockSpec(shape, lambda *_: (0,) * nd, pipeline_mode=pl.Buffered(1))


def _layer_spec(tail, l):
    nd = len(tail)
    return pl.BlockSpec((None,) + tuple(tail), lambda *_: (l,) + (0,) * nd, pipeline_mode=pl.Buffered(1))


def _params(sem):
    return pltpu.CompilerParams(dimension_semantics=sem, vmem_limit_bytes=VMEM_LIMIT)


def _mem_kv_kernel(mem_ref, g_ref, w_ref, xk_ref, xv_ref):
    hm = _rms(mem_ref[...], g_ref[...]).astype(_BF16)
    xk_ref[...] = _dot(hm, w_ref[:, :D_MODEL].astype(_BF16)).astype(_BF16)
    xv_ref[...] = _dot(hm, w_ref[:, D_MODEL:].astype(_BF16)).astype(_BF16)


def _mem_kv(l, mem2d, g, w_xkv):
    rows = mem2d.shape[0]
    return pl.pallas_call(
        _mem_kv_kernel,
        grid=(rows // TM,),
        in_specs=[pl.BlockSpec((TM, D_MODEL), lambda i: (i, 0)),
                  _layer_spec((1, D_MODEL), l),
                  _layer_spec((D_MODEL, 2 * D_MODEL), l)],
        out_specs=[pl.BlockSpec((TM, D_MODEL), lambda i: (i, 0))] * 2,
        out_shape=[jax.ShapeDtypeStruct((rows, D_MODEL), _BF16)] * 2,
        compiler_params=_params(("arbitrary",)),
        name="mem_kv",
    )(mem2d, g, w_xkv)


def _mix_in_kernel(tiles_per_seq,
                   x_ref, g_ref, w_ref, bf_ref, wpool_ref, pscale_ref, sgug_ref, sguw_ref,
                   bmat_ref, e_ref, en_ref, cq_ref, ck_ref,
                   pa_ref, q2_ref, k2_ref, v_ref, pc_ref,
                   halo_ref, fcarry_ref):
    jj = pl.program_id(0) % tiles_per_seq

    @pl.when(jj == 0)
    def _():
        halo_ref[...] = jnp.zeros_like(halo_ref)
        fcarry_ref[...] = jnp.zeros_like(fcarry_ref)

    h = _rms(x_ref[...], g_ref[...]).astype(_BF16)

    z = _dot(h, w_ref[:, C_F:N_MAIN]) + bf_ref[...]
    logf = jnp.minimum(z, 0.0) - jnp.log1p(jnp.exp(-jnp.abs(z)))
    row = lax.broadcasted_iota(jnp.int32, (TM, LANES), 0)
    c = logf
    shift = 1
    while shift < TM:
        c = c + jnp.where(row >= shift, pltpu.roll(c, shift, 0), 0.0)
        shift *= 2
    fcum = (c + fcarry_ref[...]) * LOG2E
    fcarry_ref[...] = fcarry_ref[...] + jnp.sum(logf, axis=0, keepdims=True)
    hi = fcum.astype(_BF16).astype(_F32)
    r1 = fcum - hi
    mid = r1.astype(_BF16).astype(_F32)
    lo = r1 - mid
    lane = lax.broadcasted_iota(jnp.int32, (TM, LANES), 1)
    packed = jnp.where(lane < FOX_HEADS, hi,
                       jnp.where(lane < 2 * FOX_HEADS, pltpu.roll(mid, FOX_HEADS, 1),
                                 jnp.where(lane < 3 * FOX_HEADS, pltpu.roll(lo, 2 * FOX_HEADS, 1), 0.0)))

    a = _dot(h, w_ref[:, C_A:C_Q])
    ext = jnp.concatenate([halo_ref[...], a], axis=0)
    halo_ref[...] = a[TM - POOL_HALO:, :]
    w2 = ext + pltpu.roll(ext, 1, 0)
    w4 = w2 + pltpu.roll(w2, 2, 0)
    w8 = w4 + pltpu.roll(w4, 4, 0)
    w16 = w8 + pltpu.roll(w8, 8, 0)
    grp = lax.broadcasted_iota(jnp.int32, (TM, POOL_WIDTH), 1) // POOL_GROUP_DIM
    wsel = jnp.where(grp == 0, w2[POOL_HALO:],
                     jnp.where(grp == 1, w4[POOL_HALO:],
                               jnp.where(grp == 2, w8[POOL_HALO:], w16[POOL_HALO:])))
    pos = jj * TM + lax.broadcasted_iota(jnp.int32, (TM, POOL_WIDTH), 0)
    win = jnp.left_shift(2, grp)
    cnt = jnp.minimum(pos + 1, win).astype(_F32)
    d = (wsel / cnt - a).astype(_BF16)

    q = _dot(h, w_ref[:, C_Q:C_K]) * (FOX_HEAD_DIM ** -0.5 * LOG2E)
    k = _dot(h, w_ref[:, C_K:C_V])
    v_ref[...] = _dot(h, w_ref[:, C_V:C_C]).astype(_BF16)
    qb = q.astype(_BF16)
    kb = k.astype(_BF16)
    for p in range(HEAD_PAIRS):
        q2_ref[:, p * PAIR_W:p * PAIR_W + LANES] = qb[:, p * LANES:(p + 1) * LANES]
        k2_ref[:, p * PAIR_W:p * PAIR_W + LANES] = kb[:, p * LANES:(p + 1) * LANES]
    qsq = (q * q).astype(_BF16)
    ksq = (k * k).astype(_BF16)

    cz = _dot(h, w_ref[:, C_C:C_F])
    pa_ref[...] = (_dot(d, wpool_ref[...]) * pscale_ref[...]).astype(_BF16)
    zc = cz * (0.5 * (1.0 + jnp.tanh(math.sqrt(2.0 / math.pi) * (cz + 0.044715 * (cz * cz * cz)))))
    u = zc[:, :SGU_WIDTH]
    vv = _rms(zc[:, SGU_WIDTH:], sgug_ref[...])
    wrow = lax.broadcasted_iota(jnp.int32, (SGU_GROUPS * SGU_CHUNK, SGU_CHUNK), 0) % SGU_CHUNK
    wcol = lax.broadcasted_iota(jnp.int32, (SGU_GROUPS * SGU_CHUNK, SGU_CHUNK), 1)
    wst = jnp.where(wcol <= wrow, sguw_ref[...], 0.0).astype(_BF16)
    cgrp = lax.broadcasted_iota(jnp.int32, (SGU_CHUNK, SGU_WIDTH), 1) // (SGU_WIDTH // SGU_GROUPS)
    for ci in range(TM // SGU_CHUNK):
        rs = slice(ci * SGU_CHUNK, (ci + 1) * SGU_CHUNK)
        r = _dot(wst, vv[rs].astype(_BF16))
        mixed = jnp.where(cgrp == 0, r[0:SGU_CHUNK],
                          jnp.where(cgrp == 1, r[SGU_CHUNK:2 * SGU_CHUNK],
                                    jnp.where(cgrp == 2, r[2 * SGU_CHUNK:3 * SGU_CHUNK],
                                              r[3 * SGU_CHUNK:])))
        pc_ref[rs, :] = (u[rs] * (mixed + bmat_ref[...])).astype(_BF16)

    aug = _dot(packed.astype(_BF16), e_ref[...])
    nq = jnp.minimum(_dot(qsq, en_ref[:, :LANES]), NORM_CAP)
    nk = jnp.minimum(_dot(ksq, en_ref[:, LANES:]), NORM_CAP)
    aq = (aug[:, :LANES] + cq_ref[...] + nq).astype(_BF16)
    ak = (aug[:, LANES:] + ck_ref[...] + nk).astype(_BF16)
    for p in range(HEAD_PAIRS):
        q2_ref[:, p * PAIR_W + LANES:(p + 1) * PAIR_W] = aq
        k2_ref[:, p * PAIR_W + LANES:(p + 1) * PAIR_W] = ak


def _mix_in(l, x2d, seq, g, w_main, bf, wpool, pscale, sgug, sguw, bmat, e, en, cq, ck):
    t = x2d.shape[0]
    row_spec = lambda w: pl.BlockSpec((TM, w), lambda i: (i, 0))
    return pl.pallas_call(
        functools.partial(_mix_in_kernel, seq // TM),
        grid=(t // TM,),
        in_specs=[row_spec(D_MODEL),
                  _layer_spec((1, D_MODEL), l),
                  _layer_spec((D_MODEL, N_MAIN), l),
                  _layer_spec((1, LANES), l),
                  _layer_spec((POOL_WIDTH, POOL_WIDTH), l),
                  _layer_spec((1, POOL_WIDTH), l),
                  _layer_spec((1, SGU_WIDTH), l),
                  _layer_spec((SGU_GROUPS * SGU_CHUNK, SGU_CHUNK), l),
                  _layer_spec((SGU_CHUNK, SGU_WIDTH), l),
                  _const_spec((LANES, 2 * LANES)),
                  _const_spec((FOX_WIDTH, 2 * LANES)),
                  _const_spec((1, LANES)),
                  _const_spec((1, LANES))],
        out_specs=[row_spec(POOL_WIDTH), row_spec(HEAD_PAIRS * PAIR_W), row_spec(HEAD_PAIRS * PAIR_W),
                   row_spec(FOX_WIDTH), row_spec(SGU_WIDTH)],
        out_shape=[jax.ShapeDtypeStruct((t, POOL_WIDTH), _BF16),
                   jax.ShapeDtypeStruct((t, HEAD_PAIRS * PAIR_W), _BF16),
                   jax.ShapeDtypeStruct((t, HEAD_PAIRS * PAIR_W), _BF16),
                   jax.ShapeDtypeStruct((t, FOX_WIDTH), _BF16),
                   jax.ShapeDtypeStruct((t, SGU_WIDTH), _BF16)],
        scratch_shapes=[pltpu.VMEM((POOL_HALO, POOL_WIDTH), _F32),
                        pltpu.VMEM((1, LANES), _F32)],
        compiler_params=_params(("arbitrary",)),
        name="mix_in",
    )(x2d, g, w_main, bf, wpool, pscale, sgug, sguw, bmat, e, en, cq, ck)


FOX_LOGIT_BOUND = 64.0
NORM_SLACK = 1.1


def _aug_colmax(ref):
    return jnp.max(ref[:, LANES:], axis=0, keepdims=True).astype(_F32)


def _lane_scalar(row, lane_idx):
    lane = lax.broadcasted_iota(jnp.int32, row.shape, 1)
    return jnp.max(jnp.where(lane == lane_idx, row, 0.0))


def _fox_kernel(seq, q2_ref, k2_ref, v_ref, o_ref, s_ref, p_ref, vext_ref):
    pair = pl.program_id(1)
    lane = lax.broadcasted_iota(jnp.int32, (TQ, PAIR_W), 1)
    keep = []
    for hh in range(2):
        aug_lo = LANES + (2 * pair + hh) * AUG_LANES
        keep.append(((lane >= hh * FOX_HEAD_DIM) & (lane < (hh + 1) * FOX_HEAD_DIM))
                    | ((lane >= aug_lo) & (lane < aug_lo + AUG_LANES)))
    drow = lax.broadcasted_iota(jnp.int32, (2 * TQ, TQ), 0) % TQ
    dcol = lax.broadcasted_iota(jnp.int32, (2 * TQ, TQ), 1)
    causal = dcol <= drow
    out_lane = lax.broadcasted_iota(jnp.int32, (TQ, LANES), 1)

    def stacked_q(i):
        qb = q2_ref[i * TQ:(i + 1) * TQ, :]
        zero = jnp.zeros_like(qb)
        return jnp.concatenate([jnp.where(keep[0], qb, zero), jnp.where(keep[1], qb, zero)], axis=0)

    def store_heads(i, o):
        o_ref[i * TQ:(i + 1) * TQ, :] = jnp.where(out_lane < FOX_HEAD_DIM, o[:TQ], o[TQ:]).astype(_BF16)

    bound2 = FOX_LOGIT_BOUND * FOX_LOGIT_BOUND / NORM_SLACK
    qmax, kmax = _aug_colmax(q2_ref), _aug_colmax(k2_ref)
    head_ok = [_lane_scalar(qmax, (2 * pair + hh) * AUG_LANES + QNORM_LANE)
               * _lane_scalar(kmax, (2 * pair + hh) * AUG_LANES + KNORM_LANE) <= bound2 for hh in range(2)]
    bounded = jnp.logical_and(head_ok[0], head_ok[1])

    @pl.when(bounded)
    def _():
        vext_ref[:, :LANES] = v_ref[...]
        vext_ref[:, LANES:] = jnp.ones((seq, LANES), _BF16)
        for i in range(seq // TQ):
            kend = (i + 1) * TQ
            qm = stacked_q(i)
            for c in range(i + 1):
                sc = _dot_t(qm, k2_ref[c * TQ:(c + 1) * TQ, :])
                if c == i:
                    sc = jnp.where(causal, sc, NEG)
                p_ref[:, c * TQ:(c + 1) * TQ] = jnp.exp2(sc).astype(_BF16)
            o = _dot(p_ref[:, :kend], vext_ref[:kend, :])
            store_heads(i, o[:, :LANES] * (1.0 / o[:, LANES:]))

    @pl.when(jnp.logical_not(bounded))
    def _():
        for i in range(seq // TQ):
            kend = (i + 1) * TQ
            s_ref[:, :kend] = _dot_t(stacked_q(i), k2_ref[:kend, :])

            mrun = jnp.full((2 * TQ, LANES), NEG, _F32)
            for c in range(i + 1):
                sc = s_ref[:, c * TQ:(c + 1) * TQ]
                if c == i:
                    sc = jnp.where(causal, sc, NEG)
                    s_ref[:, c * TQ:(c + 1) * TQ] = sc
                mrun = jnp.maximum(mrun, jnp.maximum(sc[:, :LANES], sc[:, LANES:]))
            mb = jnp.broadcast_to(jnp.max(mrun, axis=1, keepdims=True), (2 * TQ, LANES))

            lrun = jnp.zeros((2 * TQ, LANES), _F32)
            for c in range(i + 1):
                p0 = jnp.exp2(s_ref[:, c * TQ:c * TQ + LANES] - mb)
                p1 = jnp.exp2(s_ref[:, c * TQ + LANES:(c + 1) * TQ] - mb)
                lrun = lrun + (p0 + p1)
                p_ref[:, c * TQ:c * TQ + LANES] = p0.astype(_BF16)
                p_ref[:, c * TQ + LANES:(c + 1) * TQ] = p1.astype(_BF16)
            linv = 1.0 / jnp.sum(lrun, axis=1, keepdims=True)
            store_heads(i, _dot(p_ref[:, :kend], v_ref[:kend, :]) * linv)


def _fox(q2, k2, v):
    b, seq, _ = q2.shape
    return pl.pallas_call(
        functools.partial(_fox_kernel, seq),
        grid=(b, HEAD_PAIRS),
        in_specs=[pl.BlockSpec((None, seq, PAIR_W), lambda bi, p: (bi, 0, p)),
                  pl.BlockSpec((None, seq, PAIR_W), lambda bi, p: (bi, 0, p)),
                  pl.BlockSpec((None, seq, LANES), lambda bi, p: (bi, 0, p))],
        out_specs=pl.BlockSpec((None, seq, LANES), lambda bi, p: (bi, 0, p)),
        out_shape=jax.ShapeDtypeStruct((b, seq, FOX_WIDTH), _BF16),
        scratch_shapes=[pltpu.VMEM((2 * TQ, seq), _F32),
                        pltpu.VMEM((2 * TQ, seq), _BF16),
                        pltpu.VMEM((seq, 2 * LANES), _BF16)],
        compiler_params=_params(("arbitrary", "arbitrary")),
        name="fox_attention",
    )(q2, k2, v)


def _merge_xattn_kernel(x_ref, pa_ref, att_ref, pc_ref, gmix_ref, wg_ref, bg_ref,
                        wba_ref, wbb_ref, wbc_ref, wout_ref, gx_ref, wxq_ref,
                        xk_ref, xv_ref, wxo_ref, o_ref):
    x = x_ref[...]
    h = _rms(x, gmix_ref[...]).astype(_BF16)

    def gated(branch_ref, wb_ref, j):
        cols = slice(j * D_MODEL, (j + 1) * D_MODEL)
        gate = 1.0 / (1.0 + jnp.exp(-(_dot(h, wg_ref[:, cols]) + bg_ref[:, cols])))
        return gate * _dot(branch_ref[...], wb_ref[...].astype(_BF16))

    merged = gated(pa_ref, wba_ref, 0) + gated(att_ref, wbb_ref, 1) + gated(pc_ref, wbc_ref, 2)
    x1 = x + _dot(merged.astype(_BF16), wout_ref[...].astype(_BF16))

    hx = _rms(x1, gx_ref[...]).astype(_BF16)
    xq = (_dot(hx, wxq_ref[...].astype(_BF16)) * (XATTN_HEAD_DIM ** -0.5 * LOG2E)).astype(_BF16)
    heads = []
    for hd in range(XATTN_HEADS):
        cols = slice(hd * XATTN_HEAD_DIM, (hd + 1) * XATTN_HEAD_DIM)
        s = _dot_t(xq[:, cols], xk_ref[:, cols])
        p = jnp.exp2(s - jnp.max(s, axis=1, keepdims=True))
        linv = 1.0 / jnp.sum(p, axis=1, keepdims=True)
        heads.append((_dot(p.astype(_BF16), xv_ref[:, cols]) * linv).astype(_BF16))
    o = jnp.concatenate(heads, axis=1)
    o_ref[...] = x1 + _dot(o, wxo_ref[...].astype(_BF16))


def _merge_xattn(l, x2d, seq, pa, att, pc, gmix, w_gate, b_gate, wba, wbb, wbc, w_out, gx, w_xq, xk, xv, w_xo):
    t = x2d.shape[0]
    mlen = xk.shape[1]
    tiles_per_seq = seq // TM
    row_spec = lambda w: pl.BlockSpec((TM, w), lambda i: (i, 0))
    mem_spec = pl.BlockSpec((None, mlen, D_MODEL), lambda i: (i // tiles_per_seq, 0, 0))
    return pl.pallas_call(
        _merge_xattn_kernel,
        grid=(t // TM,),
        in_specs=[row_spec(D_MODEL), row_spec(POOL_WIDTH), row_spec(FOX_WIDTH), row_spec(SGU_WIDTH),
                  _layer_spec((1, D_MODEL), l),
                  _layer_spec((D_MODEL, 3 * D_MODEL), l),
                  _layer_spec((1, 3 * D_MODEL), l),
                  _layer_spec((POOL_WIDTH, D_MODEL), l),
                  _layer_spec((FOX_WIDTH, D_MODEL), l),
                  _layer_spec((SGU_WIDTH, D_MODEL), l),
                  _layer_spec((D_MODEL, D_MODEL), l),
                  _layer_spec((1, D_MODEL), l),
                  _layer_spec((D_MODEL, D_MODEL), l),
                  mem_spec, mem_spec,
                  _layer_spec((D_MODEL, D_MODEL), l)],
        out_specs=row_spec(D_MODEL),
        out_shape=jax.ShapeDtypeStruct((t, D_MODEL), _F32),
        compiler_params=_params(("arbitrary",)),
        name="merge_xattn",
    )(x2d, pa, att, pc, gmix, w_gate, b_gate, wba, wbb, wbc, w_out, gx, w_xq, xk, xv, w_xo)


FF_CHUNK = 1024


def _ffn_kernel(final_norm, x_ref, g_ref, w1_ref, w2_ref, gfin_ref, o_ref, act_ref):
    x = x_ref[...]
    hf = _rms(x, g_ref[...]).astype(_BF16)
    for c in range(D_FF // FF_CHUNK):
        cols = slice(c * FF_CHUNK, (c + 1) * FF_CHUNK)
        r = jnp.maximum(_dot(hf, w1_ref[:, cols].astype(_BF16)), 0.0)
        act_ref[:, cols] = (r * r).astype(_BF16)
    y = x + _dot(act_ref[...], w2_ref[...].astype(_BF16))
    if final_norm:
        y = _rms(y, gfin_ref[...])
    o_ref[...] = y


def _ffn(l, x2d, g, w1, w2, gfin, final_norm):
    t = x2d.shape[0]
    return pl.pallas_call(
        functools.partial(_ffn_kernel, final_norm),
        grid=(t // TM,),
        in_specs=[pl.BlockSpec((TM, D_MODEL), lambda i: (i, 0)),
                  _layer_spec((1, D_MODEL), l),
                  _layer_spec((D_MODEL, D_FF), l),
                  _layer_spec((D_FF, D_MODEL), l),
                  _const_spec((1, D_MODEL))],
        out_specs=pl.BlockSpec((TM, D_MODEL), lambda i: (i, 0)),
        out_shape=jax.ShapeDtypeStruct((t, D_MODEL), _F32),
        scratch_shapes=[pltpu.VMEM((TM, D_FF), _BF16)],
        compiler_params=_params(("arbitrary",)),
        name="ffn",
    )(x2d, g, w1, w2, gfin)


def _aug_constants():
    e = np.zeros((LANES, 2 * LANES), np.float32)
    en = np.zeros((FOX_WIDTH, 2 * LANES), np.float32)
    cq = np.zeros((1, LANES), np.float32)
    ck = np.zeros((1, LANES), np.float32)
    for h in range(FOX_HEADS):
        for piece in range(3):
            e[piece * FOX_HEADS + h, h * AUG_LANES + piece] = 1.0
            e[piece * FOX_HEADS + h, LANES + h * AUG_LANES + 3 + piece] = -1.0
            cq[0, h * AUG_LANES + 3 + piece] = 1.0
            ck[0, h * AUG_LANES + piece] = 1.0
        cols = slice(h * FOX_HEAD_DIM, (h + 1) * FOX_HEAD_DIM)
        en[cols, h * AUG_LANES + QNORM_LANE] = 1.0
        en[cols, LANES + h * AUG_LANES + KNORM_LANE] = 1.0
    return jnp.asarray(e, _BF16), jnp.asarray(en, _BF16), jnp.asarray(cq), jnp.asarray(ck)


def kernel(x, mem, norm_mix_g, w_in, b_forget, pool_w, pool_scale, sgu_norm_g, sgu_w, sgu_b,
           w_branch_a, w_branch_b, w_branch_c, b_gate, w_out, norm_xattn_g, norm_mem_g,
           w_xq, w_xkv, w_xo, norm_ffn_g, w_ff1, w_ff2, final_norm_g):
    b, seq, d = x.shape
    mlen = mem.shape[1]
    t = b * seq
    off_q = POOL_WIDTH
    off_f = off_q + 3 * FOX_WIDTH
    off_c = off_f + FOX_HEADS
    off_g = off_c + 2 * SGU_WIDTH
    e, en, cq, ck = _aug_constants()
    rows = lambda a: a.reshape(a.shape[0], 1, -1)
    n_groups = len(POOL_WINDOWS)

    w_main = jnp.concatenate(
        [w_in[:, :, :off_f], w_in[:, :, off_c:off_g],
         jnp.pad(w_in[:, :, off_f:off_c], ((0, 0), (0, 0), (0, LANES - FOX_HEADS)))], axis=2).astype(_BF16)
    w_gate = w_in[:, :, off_g:].astype(_BF16)
    bf = rows(jnp.pad(b_forget, ((0, 0), (0, LANES - FOX_HEADS))))
    eye = jnp.eye(n_groups, dtype=pool_w.dtype)
    wpool = jnp.einsum('lgcd,gh->lgchd', pool_w, eye).reshape(DEPTH, POOL_WIDTH, POOL_WIDTH).astype(_BF16)
    sguw = sgu_w.reshape(DEPTH, SGU_GROUPS * SGU_CHUNK, SGU_CHUNK)
    bmat = jnp.repeat(jnp.swapaxes(sgu_b, 1, 2), SGU_WIDTH // SGU_GROUPS, axis=2)
    g_mix, g_mem, g_x, g_ffn = rows(norm_mix_g), rows(norm_mem_g), rows(norm_xattn_g), rows(norm_ffn_g)
    pscale, sgug, bg = rows(pool_scale), rows(sgu_norm_g), rows(b_gate)

    xc = x.reshape(t, d)
    mem2d = mem.reshape(b * mlen, d)
    for l in range(DEPTH):
        xk, xv = _mem_kv(l, mem2d, g_mem, w_xkv)
        pa, q2, k2, v, pc = _mix_in(l, xc, seq, g_mix, w_main, bf, wpool, pscale, sgug, sguw, bmat, e, en, cq, ck)
        att = _fox(q2.reshape(b, seq, -1), k2.reshape(b, seq, -1), v.reshape(b, seq, -1))
        xc = _merge_xattn(l, xc, seq, pa, att.reshape(t, -1), pc, g_mix, w_gate, bg,
                          w_branch_a, w_branch_b, w_branch_c, w_out, g_x, w_xq,
                          xk.reshape(b, mlen, d), xv.reshape(b, mlen, d), w_xo)
        xc = _ffn(l, xc, g_ffn, w_ff1, w_ff2, final_norm_g.reshape(1, d), l == DEPTH - 1)
    return xc.reshape(b, seq, d)
```

```python
import functools
import math

import numpy as np
import jax
import jax.numpy as jnp
from jax import lax
from jax.experimental import pallas as pl
from jax.experimental.pallas import tpu as pltpu

D_MODEL = 1024
DEPTH = 2
POOL_GROUP_DIM = 64
POOL_WIDTH = 256
POOL_WINDOWS = (2, 4, 8, 16)
POOL_HALO = 16
FOX_HEADS = 8
FOX_HEAD_DIM = 64
FOX_WIDTH = 512
SGU_WIDTH = 256
SGU_GROUPS = 4
SGU_CHUNK = 128
XATTN_HEADS = 4
XATTN_HEAD_DIM = 256
D_FF = 4096
EPS = 1e-6
NEG = -1e30
LOG2E = math.log2(math.e)

LANES = 128
AUG_LANES = 8
QNORM_LANE = 6
KNORM_LANE = 7
NORM_CAP = 1e30
HEAD_PAIRS = FOX_HEADS // 2
PAIR_W = 2 * LANES

C_A = 0
C_Q = C_A + POOL_WIDTH
C_K = C_Q + FOX_WIDTH
C_V = C_K + FOX_WIDTH
C_C = C_V + FOX_WIDTH
C_F = C_C + 2 * SGU_WIDTH
N_MAIN = C_F + LANES

TM = 512
TQ = 256
VMEM_LIMIT = 56 * 1024 * 1024

_F32 = jnp.float32
_BF16 = jnp.bfloat16


def _dot(a, b):
    return jnp.dot(a, b, preferred_element_type=_F32)


def _dot_t(a, b):
    return lax.dot_general(a, b, (((1,), (1,)), ((), ())), preferred_element_type=_F32)


def _rms(x, g):
    ms = jnp.mean(x * x, axis=-1, keepdims=True)
    return x * lax.rsqrt(ms + EPS) * g


def _const_spec(shape):
    nd = len(shape)
    return pl.BlockSpec(shape, lambda *_: (0,) * nd, pipeline_mode=pl.Buffered(1))


def _layer_spec(tail, l):
    nd = len(tail)
    return pl.BlockSpec((None,) + tuple(tail), lambda *_: (l,) + (0,) * nd, pipeline_mode=pl.Buffered(1))


def _params(sem):
    return pltpu.CompilerParams(dimension_semantics=sem, vmem_limit_bytes=VMEM_LIMIT)


OFF_F = POOL_WIDTH + 3 * FOX_WIDTH
OFF_C = OFF_F + FOX_HEADS
OFF_G = OFF_C + 2 * SGU_WIDTH
N_IN = OFF_G + 3 * D_MODEL
REPACK_ROWS = 256


def _repack_kernel(w_ref, main_ref, gate_ref):
    main_ref[:, :C_C] = w_ref[:, :OFF_F].astype(_BF16)
    main_ref[:, C_C:C_F] = w_ref[:, OFF_C:OFF_G].astype(_BF16)
    fcols = w_ref[:, OFF_F:OFF_F + LANES]
    lane = lax.broadcasted_iota(jnp.int32, fcols.shape, 1)
    main_ref[:, C_F:] = jnp.where(lane < FOX_HEADS, fcols, 0.0).astype(_BF16)
    gate_ref[...] = w_ref[:, OFF_G:].astype(_BF16)


def _repack_w_in(w_in):
    depth, d, n_in = w_in.shape
    assert n_in == N_IN
    return pl.pallas_call(
        _repack_kernel,
        grid=(depth, d // REPACK_ROWS),
        in_specs=[pl.BlockSpec((None, REPACK_ROWS, N_IN), lambda l, i: (l, i, 0))],
        out_specs=[pl.BlockSpec((None, REPACK_ROWS, N_MAIN), lambda l, i: (l, i, 0)),
                   pl.BlockSpec((None, REPACK_ROWS, 3 * D_MODEL), lambda l, i: (l, i, 0))],
        out_shape=[jax.ShapeDtypeStruct((depth, d, N_MAIN), _BF16),
                   jax.ShapeDtypeStruct((depth, d, 3 * D_MODEL), _BF16)],
        compiler_params=_params(("arbitrary", "arbitrary")),
        name="repack_w_in",
    )(w_in)


def _mem_kv_kernel(mem_ref, g_ref, w_ref, xk_ref, xv_ref):
    hm = _rms(mem_ref[...], g_ref[...]).astype(_BF16)
    xk_ref[...] = _dot(hm, w_ref[:, :D_MODEL].astype(_BF16)).astype(_BF16)
    xv_ref[...] = _dot(hm, w_ref[:, D_MODEL:].astype(_BF16)).astype(_BF16)


def _mem_kv(l, mem2d, g, w_xkv):
    rows = mem2d.shape[0]
    return pl.pallas_call(
        _mem_kv_kernel,
        grid=(rows // TM,),
        in_specs=[pl.BlockSpec((TM, D_MODEL), lambda i: (i, 0)),
                  _layer_spec((1, D_MODEL), l),
                  _layer_spec((D_MODEL, 2 * D_MODEL), l)],
        out_specs=[pl.BlockSpec((TM, D_MODEL), lambda i: (i, 0))] * 2,
        out_shape=[jax.ShapeDtypeStruct((rows, D_MODEL), _BF16)] * 2,
        compiler_params=_params(("arbitrary",)),
        name="mem_kv",
    )(mem2d, g, w_xkv)


def _mix_in_kernel(tiles_per_seq,
                   x_ref, g_ref, w_ref, bf_ref, wpool_ref, pscale_ref, sgug_ref, sguw_ref,
                   bmat_ref, e_ref, en_ref, cq_ref, ck_ref,
                   pa_ref, q2_ref, k2_ref, v_ref, pc_ref,
                   halo_ref, fcarry_ref):
    jj = pl.program_id(0) % tiles_per_seq

    @pl.when(jj == 0)
    def _():
        halo_ref[...] = jnp.zeros_like(halo_ref)
        fcarry_ref[...] = jnp.zeros_like(fcarry_ref)

    h = _rms(x_ref[...], g_ref[...]).astype(_BF16)

    z = _dot(h, w_ref[:, C_F:N_MAIN]) + bf_ref[...]
    logf = jnp.minimum(z, 0.0) - jnp.log1p(jnp.exp(-jnp.abs(z)))
    row = lax.broadcasted_iota(jnp.int32, (TM, LANES), 0)
    c = logf
    shift = 1
    while shift < TM:
        c = c + jnp.where(row >= shift, pltpu.roll(c, shift, 0), 0.0)
        shift *= 2
    fcum = (c + fcarry_ref[...]) * LOG2E
    fcarry_ref[...] = fcarry_ref[...] + jnp.sum(logf, axis=0, keepdims=True)
    hi = fcum.astype(_BF16).astype(_F32)
    r1 = fcum - hi
    mid = r1.astype(_BF16).astype(_F32)
    lo = r1 - mid
    lane = lax.broadcasted_iota(jnp.int32, (TM, LANES), 1)
    packed = jnp.where(lane < FOX_HEADS, hi,
                       jnp.where(lane < 2 * FOX_HEADS, pltpu.roll(mid, FOX_HEADS, 1),
                                 jnp.where(lane < 3 * FOX_HEADS, pltpu.roll(lo, 2 * FOX_HEADS, 1), 0.0)))

    a = _dot(h, w_ref[:, C_A:C_Q])
    ext = jnp.concatenate([halo_ref[...], a], axis=0)
    halo_ref[...] = a[TM - POOL_HALO:, :]
    w2 = ext + pltpu.roll(ext, 1, 0)
    w4 = w2 + pltpu.roll(w2, 2, 0)
    w8 = w4 + pltpu.roll(w4, 4, 0)
    w16 = w8 + pltpu.roll(w8, 8, 0)
    grp = lax.broadcasted_iota(jnp.int32, (TM, POOL_WIDTH), 1) // POOL_GROUP_DIM
    wsel = jnp.where(grp == 0, w2[POOL_HALO:],
                     jnp.where(grp == 1, w4[POOL_HALO:],
                               jnp.where(grp == 2, w8[POOL_HALO:], w16[POOL_HALO:])))
    pos = jj * TM + lax.broadcasted_iota(jnp.int32, (TM, POOL_WIDTH), 0)
    win = jnp.left_shift(2, grp)
    cnt = jnp.minimum(pos + 1, win).astype(_F32)
    d = (wsel / cnt - a).astype(_BF16)

    q = _dot(h, w_ref[:, C_Q:C_K]) * (FOX_HEAD_DIM ** -0.5 * LOG2E)
    k = _dot(h, w_ref[:, C_K:C_V])
    v_ref[...] = _dot(h, w_ref[:, C_V:C_C]).astype(_BF16)
    qb = q.astype(_BF16)
    kb = k.astype(_BF16)
    for p in range(HEAD_PAIRS):
        q2_ref[:, p * PAIR_W:p * PAIR_W + LANES] = qb[:, p * LANES:(p + 1) * LANES]
        k2_ref[:, p * PAIR_W:p * PAIR_W + LANES] = kb[:, p * LANES:(p + 1) * LANES]
    qsq = (q * q).astype(_BF16)
    ksq = (k * k).astype(_BF16)

    cz = _dot(h, w_ref[:, C_C:C_F])
    pa_ref[...] = (_dot(d, wpool_ref[...]) * pscale_ref[...]).astype(_BF16)
    zc = cz * (0.5 * (1.0 + jnp.tanh(math.sqrt(2.0 / math.pi) * (cz + 0.044715 * (cz * cz * cz)))))
    u = zc[:, :SGU_WIDTH]
    vv = _rms(zc[:, SGU_WIDTH:], sgug_ref[...])
    wrow = lax.broadcasted_iota(jnp.int32, (SGU_GROUPS * SGU_CHUNK, SGU_CHUNK), 0) % SGU_CHUNK
    wcol = lax.broadcasted_iota(jnp.int32, (SGU_GROUPS * SGU_CHUNK, SGU_CHUNK), 1)
    wst = jnp.where(wcol <= wrow, sguw_ref[...], 0.0).astype(_BF16)
    cgrp = lax.broadcasted_iota(jnp.int32, (SGU_CHUNK, SGU_WIDTH), 1) // (SGU_WIDTH // SGU_GROUPS)
    for ci in range(TM // SGU_CHUNK):
        rs = slice(ci * SGU_CHUNK, (ci + 1) * SGU_CHUNK)
        r = _dot(wst, vv[rs].astype(_BF16))
        mixed = jnp.where(cgrp == 0, r[0:SGU_CHUNK],
                          jnp.where(cgrp == 1, r[SGU_CHUNK:2 * SGU_CHUNK],
                                    jnp.where(cgrp == 2, r[2 * SGU_CHUNK:3 * SGU_CHUNK],
                                              r[3 * SGU_CHUNK:])))
        pc_ref[rs, :] = (u[rs] * (mixed + bmat_ref[...])).astype(_BF16)

    aug = _dot(packed.astype(_BF16), e_ref[...])
    nq = jnp.minimum(_dot(qsq, en_ref[:, :LANES]), NORM_CAP)
    nk = jnp.minimum(_dot(ksq, en_ref[:, LANES:]), NORM_CAP)
    aq = (aug[:, :LANES] + cq_ref[...] + nq).astype(_BF16)
    ak = (aug[:, LANES:] + ck_ref[...] + nk).astype(_BF16)
    for p in range(HEAD_PAIRS):
        q2_ref[:, p * PAIR_W + LANES:(p + 1) * PAIR_W] = aq
        k2_ref[:, p * PAIR_W + LANES:(p + 1) * PAIR_W] = ak


def _mix_in(l, x2d, seq, g, w_main, bf, wpool, pscale, sgug, sguw, bmat, e, en, cq, ck):
    t = x2d.shape[0]
    row_spec = lambda w: pl.BlockSpec((TM, w), lambda i: (i, 0))
    return pl.pallas_call(
        functools.partial(_mix_in_kernel, seq // TM),
        grid=(t // TM,),
        in_specs=[row_spec(D_MODEL),
                  _layer_spec((1, D_MODEL), l),
                  _layer_spec((D_MODEL, N_MAIN), l),
                  _layer_spec((1, LANES), l),
                  _layer_spec((POOL_WIDTH, POOL_WIDTH), l),
                  _layer_spec((1, POOL_WIDTH), l),
                  _layer_spec((1, SGU_WIDTH), l),
                  _layer_spec((SGU_GROUPS * SGU_CHUNK, SGU_CHUNK), l),
                  _layer_spec((SGU_CHUNK, SGU_WIDTH), l),
                  _const_spec((LANES, 2 * LANES)),
                  _const_spec((FOX_WIDTH, 2 * LANES)),
                  _const_spec((1, LANES)),
                  _const_spec((1, LANES))],
        out_specs=[row_spec(POOL_WIDTH), row_spec(HEAD_PAIRS * PAIR_W), row_spec(HEAD_PAIRS * PAIR_W),
                   row_spec(FOX_WIDTH), row_spec(SGU_WIDTH)],
        out_shape=[jax.ShapeDtypeStruct((t, POOL_WIDTH), _BF16),
                   jax.ShapeDtypeStruct((t, HEAD_PAIRS * PAIR_W), _BF16),
                   jax.ShapeDtypeStruct((t, HEAD_PAIRS * PAIR_W), _BF16),
                   jax.ShapeDtypeStruct((t, FOX_WIDTH), _BF16),
                   jax.ShapeDtypeStruct((t, SGU_WIDTH), _BF16)],
        scratch_shapes=[pltpu.VMEM((POOL_HALO, POOL_WIDTH), _F32),
                        pltpu.VMEM((1, LANES), _F32)],
        compiler_params=_params(("arbitrary",)),
        name="mix_in",
    )(x2d, g, w_main, bf, wpool, pscale, sgug, sguw, bmat, e, en, cq, ck)


FOX_LOGIT_BOUND = 64.0
NORM_SLACK = 1.1
COLMAX_SPLIT = 8


def _aug_colmax(ref):
    rows = ref.shape[0] // COLMAX_SPLIT
    parts = [ref[r * rows:(r + 1) * rows, LANES:] for r in range(COLMAX_SPLIT)]
    while len(parts) > 1:
        parts = [jnp.maximum(a, b) for a, b in zip(parts[::2], parts[1::2])]
    return jnp.max(parts[0], axis=0, keepdims=True).astype(_F32)


def _pair_logit_bound2(q2_ref, k2_ref, pair):
    qmax, kmax = _aug_colmax(q2_ref), _aug_colmax(k2_ref)
    prod = qmax * pltpu.roll(kmax, LANES - (KNORM_LANE - QNORM_LANE), 1)
    lane = lax.broadcasted_iota(jnp.int32, prod.shape, 1)
    first = 2 * pair * AUG_LANES + QNORM_LANE
    return jnp.max(jnp.where((lane == first) | (lane == first + AUG_LANES), prod, 0.0))


def _fox_kernel(seq, q2_ref, k2_ref, v_ref, o_ref, s_ref, p_ref, vext_ref):
    pair = pl.program_id(1)
    lane = lax.broadcasted_iota(jnp.int32, (TQ, PAIR_W), 1)
    keep = []
    for hh in range(2):
        aug_lo = LANES + (2 * pair + hh) * AUG_LANES
        keep.append(((lane >= hh * FOX_HEAD_DIM) & (lane < (hh + 1) * FOX_HEAD_DIM))
                    | ((lane >= aug_lo) & (lane < aug_lo + AUG_LANES)))
    drow = lax.broadcasted_iota(jnp.int32, (2 * TQ, TQ), 0) % TQ
    dcol = lax.broadcasted_iota(jnp.int32, (2 * TQ, TQ), 1)
    causal = dcol <= drow
    out_lane = lax.broadcasted_iota(jnp.int32, (TQ, LANES), 1)

    def stacked_q(i):
        qb = q2_ref[i * TQ:(i + 1) * TQ, :]
        zero = jnp.zeros_like(qb)
        return jnp.concatenate([jnp.where(keep[0], qb, zero), jnp.where(keep[1], qb, zero)], axis=0)

    def store_heads(i, o):
        o_ref[i * TQ:(i + 1) * TQ, :] = jnp.where(out_lane < FOX_HEAD_DIM, o[:TQ], o[TQ:]).astype(_BF16)

    bounded = _pair_logit_bound2(q2_ref, k2_ref, pair) <= FOX_LOGIT_BOUND * FOX_LOGIT_BOUND / NORM_SLACK

    @pl.when(bounded)
    def _():
        vext_ref[:, :LANES] = v_ref[...]
        vext_ref[:, LANES:] = jnp.ones((seq, LANES), _BF16)

        def probabilities(i):
            qm = stacked_q(i)
            for c in range(i + 1):
                sc = _dot_t(qm, k2_ref[c * TQ:(c + 1) * TQ, :])
                if c == i:
                    sc = jnp.where(causal, sc, NEG)
                p_ref[i % 2, :, c * TQ:(c + 1) * TQ] = jnp.exp2(sc).astype(_BF16)

        n_blocks = seq // TQ
        probabilities(0)
        for i in range(n_blocks):
            if i + 1 < n_blocks:
                probabilities(i + 1)
            kend = (i + 1) * TQ
            o = _dot(p_ref[i % 2, :, :kend], vext_ref[:kend, :])
            store_heads(i, o[:, :LANES] * (1.0 / o[:, LANES:]))

    @pl.when(jnp.logical_not(bounded))
    def _():
        for i in range(seq // TQ):
            kend = (i + 1) * TQ
            s_ref[:, :kend] = _dot_t(stacked_q(i), k2_ref[:kend, :])

            mrun = jnp.full((2 * TQ, LANES), NEG, _F32)
            for c in range(i + 1):
                sc = s_ref[:, c * TQ:(c + 1) * TQ]
                if c == i:
                    sc = jnp.where(causal, sc, NEG)
                    s_ref[:, c * TQ:(c + 1) * TQ] = sc
                mrun = jnp.maximum(mrun, jnp.maximum(sc[:, :LANES], sc[:, LANES:]))
            mb = jnp.broadcast_to(jnp.max(mrun, axis=1, keepdims=True), (2 * TQ, LANES))

            lrun = jnp.zeros((2 * TQ, LANES), _F32)
            for c in range(i + 1):
                p0 = jnp.exp2(s_ref[:, c * TQ:c * TQ + LANES] - mb)
                p1 = jnp.exp2(s_ref[:, c * TQ + LANES:(c + 1) * TQ] - mb)
                lrun = lrun + (p0 + p1)
                p_ref[0, :, c * TQ:c * TQ + LANES] = p0.astype(_BF16)
                p_ref[0, :, c * TQ + LANES:(c + 1) * TQ] = p1.astype(_BF16)
            linv = 1.0 / jnp.sum(lrun, axis=1, keepdims=True)
            store_heads(i, _dot(p_ref[0, :, :kend], v_ref[:kend, :]) * linv)


def _fox(q2, k2, v):
    b, seq, _ = q2.shape
    return pl.pallas_call(
        functools.partial(_fox_kernel, seq),
        grid=(b, HEAD_PAIRS),
        in_specs=[pl.BlockSpec((None, seq, PAIR_W), lambda bi, p: (bi, 0, p)),
                  pl.BlockSpec((None, seq, PAIR_W), lambda bi, p: (bi, 0, p)),
                  pl.BlockSpec((None, seq, LANES), lambda bi, p: (bi, 0, p))],
        out_specs=pl.BlockSpec((None, seq, LANES), lambda bi, p: (bi, 0, p)),
        out_shape=jax.ShapeDtypeStruct((b, seq, FOX_WIDTH), _BF16),
        scratch_shapes=[pltpu.VMEM((2 * TQ, seq), _F32),
                        pltpu.VMEM((2, 2 * TQ, seq), _BF16),
                        pltpu.VMEM((seq, 2 * LANES), _BF16)],
        compiler_params=_params(("arbitrary", "arbitrary")),
        name="fox_attention",
    )(q2, k2, v)


def _merge_xattn_kernel(x_ref, pa_ref, att_ref, pc_ref, gmix_ref, wg_ref, bg_ref,
                        wba_ref, wbb_ref, wbc_ref, wout_ref, gx_ref, wxq_ref,
                        xk_ref, xv_ref, wxo_ref, o_ref):
    branches = [_dot(b_ref[...], wb_ref[...].astype(_BF16))
                for b_ref, wb_ref in ((pa_ref, wba_ref), (att_ref, wbb_ref), (pc_ref, wbc_ref))]
    x = x_ref[...]
    h = _rms(x, gmix_ref[...]).astype(_BF16)

    def gated(j):
        cols = slice(j * D_MODEL, (j + 1) * D_MODEL)
        gate = 1.0 / (1.0 + jnp.exp(-(_dot(h, wg_ref[:, cols]) + bg_ref[:, cols])))
        return gate * branches[j]

    merged = gated(0) + gated(1) + gated(2)
    x1 = x + _dot(merged.astype(_BF16), wout_ref[...].astype(_BF16))

    hx = _rms(x1, gx_ref[...]).astype(_BF16)
    xq = (_dot(hx, wxq_ref[...].astype(_BF16)) * (XATTN_HEAD_DIM ** -0.5 * LOG2E)).astype(_BF16)
    head_cols = [slice(hd * XATTN_HEAD_DIM, (hd + 1) * XATTN_HEAD_DIM) for hd in range(XATTN_HEADS)]
    scores = [_dot_t(xq[:, cols], xk_ref[:, cols]) for cols in head_cols]
    heads = []
    for s, cols in zip(scores, head_cols):
        p = jnp.exp2(s - jnp.max(s, axis=1, keepdims=True))
        linv = 1.0 / jnp.sum(p, axis=1, keepdims=True)
        heads.append((_dot(p.astype(_BF16), xv_ref[:, cols]) * linv).astype(_BF16))
    o = jnp.concatenate(heads, axis=1)
    o_ref[...] = x1 + _dot(o, wxo_ref[...].astype(_BF16))


def _merge_xattn(l, x2d, seq, pa, att, pc, gmix, w_gate, b_gate, wba, wbb, wbc, w_out, gx, w_xq, xk, xv, w_xo):
    t = x2d.shape[0]
    mlen = xk.shape[1]
    tiles_per_seq = seq // TM
    row_spec = lambda w: pl.BlockSpec((TM, w), lambda i: (i, 0))
    mem_spec = pl.BlockSpec((None, mlen, D_MODEL), lambda i: (i // tiles_per_seq, 0, 0))
    return pl.pallas_call(
        _merge_xattn_kernel,
        grid=(t // TM,),
        in_specs=[row_spec(D_MODEL), row_spec(POOL_WIDTH), row_spec(FOX_WIDTH), row_spec(SGU_WIDTH),
                  _layer_spec((1, D_MODEL), l),
                  _layer_spec((D_MODEL, 3 * D_MODEL), l),
                  _layer_spec((1, 3 * D_MODEL), l),
                  _layer_spec((POOL_WIDTH, D_MODEL), l),
                  _layer_spec((FOX_WIDTH, D_MODEL), l),
                  _layer_spec((SGU_WIDTH, D_MODEL), l),
                  _layer_spec((D_MODEL, D_MODEL), l),
                  _layer_spec((1, D_MODEL), l),
                  _layer_spec((D_MODEL, D_MODEL), l),
                  mem_spec, mem_spec,
                  _layer_spec((D_MODEL, D_MODEL), l)],
        out_specs=row_spec(D_MODEL),
        out_shape=jax.ShapeDtypeStruct((t, D_MODEL), _F32),
        compiler_params=_params(("arbitrary",)),
        name="merge_xattn",
    )(x2d, pa, att, pc, gmix, w_gate, b_gate, wba, wbb, wbc, w_out, gx, w_xq, xk, xv, w_xo)


FF_CHUNK = 1024


def _ffn_kernel(final_norm, x_ref, g_ref, w1_ref, w2_ref, gfin_ref, o_ref, act_ref):
    x = x_ref[...]
    hf = _rms(x, g_ref[...]).astype(_BF16)
    for c in range(D_FF // FF_CHUNK):
        cols = slice(c * FF_CHUNK, (c + 1) * FF_CHUNK)
        r = jnp.maximum(_dot(hf, w1_ref[:, cols].astype(_BF16)), 0.0)
        act_ref[:, cols] = (r * r).astype(_BF16)
    y = x + _dot(act_ref[...], w2_ref[...].astype(_BF16))
    if final_norm:
        y = _rms(y, gfin_ref[...])
    o_ref[...] = y


def _ffn(l, x2d, g, w1, w2, gfin, final_norm):
    t = x2d.shape[0]
    return pl.pallas_call(
        functools.partial(_ffn_kernel, final_norm),
        grid=(t // TM,),
        in_specs=[pl.BlockSpec((TM, D_MODEL), lambda i: (i, 0)),
                  _layer_spec((1, D_MODEL), l),
                  _layer_spec((D_MODEL, D_FF), l),
                  _layer_spec((D_FF, D_MODEL), l),
                  _const_spec((1, D_MODEL))],
        out_specs=pl.BlockSpec((TM, D_MODEL), lambda i: (i, 0)),
        out_shape=jax.ShapeDtypeStruct((t, D_MODEL), _F32),
        scratch_shapes=[pltpu.VMEM((TM, D_FF), _BF16)],
        compiler_params=_params(("arbitrary",)),
        name="ffn",
    )(x2d, g, w1, w2, gfin)


def _aug_constants():
    e = np.zeros((LANES, 2 * LANES), np.float32)
    en = np.zeros((FOX_WIDTH, 2 * LANES), np.float32)
    cq = np.zeros((1, LANES), np.float32)
    ck = np.zeros((1, LANES), np.float32)
    for h in range(FOX_HEADS):
        for piece in range(3):
            e[piece * FOX_HEADS + h, h * AUG_LANES + piece] = 1.0
            e[piece * FOX_HEADS + h, LANES + h * AUG_LANES + 3 + piece] = -1.0
            cq[0, h * AUG_LANES + 3 + piece] = 1.0
            ck[0, h * AUG_LANES + piece] = 1.0
        cols = slice(h * FOX_HEAD_DIM, (h + 1) * FOX_HEAD_DIM)
        en[cols, h * AUG_LANES + QNORM_LANE] = 1.0
        en[cols, LANES + h * AUG_LANES + KNORM_LANE] = 1.0
    return jnp.asarray(e, _BF16), jnp.asarray(en, _BF16), jnp.asarray(cq), jnp.asarray(ck)


def kernel(x, mem, norm_mix_g, w_in, b_forget, pool_w, pool_scale, sgu_norm_g, sgu_w, sgu_b,
           w_branch_a, w_branch_b, w_branch_c, b_gate, w_out, norm_xattn_g, norm_mem_g,
           w_xq, w_xkv, w_xo, norm_ffn_g, w_ff1, w_ff2, final_norm_g):
    b, seq, d = x.shape
    mlen = mem.shape[1]
    t = b * seq
    e, en, cq, ck = _aug_constants()
    rows = lambda a: a.reshape(a.shape[0], 1, -1)
    n_groups = len(POOL_WINDOWS)

    w_main, w_gate = _repack_w_in(w_in)
    bf = rows(jnp.pad(b_forget, ((0, 0), (0, LANES - FOX_HEADS))))
    eye = jnp.eye(n_groups, dtype=pool_w.dtype)
    wpool = jnp.einsum('lgcd,gh->lgchd', pool_w, eye).reshape(DEPTH, POOL_WIDTH, POOL_WIDTH).astype(_BF16)
    sguw = sgu_w.reshape(DEPTH, SGU_GROUPS * SGU_CHUNK, SGU_CHUNK)
    bmat = jnp.repeat(jnp.swapaxes(sgu_b, 1, 2), SGU_WIDTH // SGU_GROUPS, axis=2)
    g_mix, g_mem, g_x, g_ffn = rows(norm_mix_g), rows(norm_mem_g), rows(norm_xattn_g), rows(norm_ffn_g)
    pscale, sgug, bg = rows(pool_scale), rows(sgu_norm_g), rows(b_gate)

    xc = x.reshape(t, d)
    mem2d = mem.reshape(b * mlen, d)
    for l in range(DEPTH):
        xk, xv = _mem_kv(l, mem2d, g_mem, w_xkv)
        pa, q2, k2, v, pc = _mix_in(l, xc, seq, g_mix, w_main, bf, wpool, pscale, sgug, sguw, bmat, e, en, cq, ck)
        att = _fox(q2.reshape(b, seq, -1), k2.reshape(b, seq, -1), v.reshape(b, seq, -1))
        xc = _merge_xattn(l, xc, seq, pa, att.reshape(t, -1), pc, g_mix, w_gate, bg,
                          w_branch_a, w_branch_b, w_branch_c, w_out, g_x, w_xq,
                          xk.reshape(b, mlen, d), xv.reshape(b, mlen, d), w_xo)
        xc = _ffn(l, xc, g_ffn, w_ff1, w_ff2, final_norm_g.reshape(1, d), l == DEPTH - 1)
    return xc.reshape(b, seq, d)
```

```python
import functools
import math

import numpy as np
import jax
import jax.numpy as jnp
from jax import lax
from jax.experimental import pallas as pl
from jax.experimental.pallas import tpu as pltpu

D_MODEL = 1024
DEPTH = 2
POOL_GROUP_DIM = 64
POOL_WIDTH = 256
POOL_WINDOWS = (2, 4, 8, 16)
POOL_HALO = 16
FOX_HEADS = 8
FOX_HEAD_DIM = 64
FOX_WIDTH = 512
SGU_WIDTH = 256
SGU_GROUPS = 4
SGU_CHUNK = 128
XATTN_HEADS = 4
XATTN_HEAD_DIM = 256
D_FF = 4096
EPS = 1e-6
NEG = -1e30
LOG2E = math.log2(math.e)

LANES = 128
AUG_LANES = 8
QNORM_LANE = 6
KNORM_LANE = 7
NORM_CAP = 1e30
HEAD_PAIRS = FOX_HEADS // 2
PAIR_W = 2 * LANES

OFF_A = 0
OFF_Q = OFF_A + POOL_WIDTH
OFF_K = OFF_Q + FOX_WIDTH
OFF_V = OFF_K + FOX_WIDTH
OFF_F = OFF_V + FOX_WIDTH
OFF_C = OFF_F + FOX_HEADS
OFF_G = OFF_C + 2 * SGU_WIDTH
N_IN = OFF_G + 3 * D_MODEL

TM = 512
TQ = 256
VMEM_LIMIT = 56 * 1024 * 1024

_F32 = jnp.float32
_BF16 = jnp.bfloat16


def _dot(a, b):
    return jnp.dot(a, b, preferred_element_type=_F32)


def _dot_t(a, b):
    return lax.dot_general(a, b, (((1,), (1,)), ((), ())), preferred_element_type=_F32)


def _rms(x, g):
    ms = jnp.mean(x * x, axis=-1, keepdims=True)
    return x * lax.rsqrt(ms + EPS) * g


def _const_spec(shape):
    nd = len(shape)
    return pl.BlockSpec(shape, lambda *_: (0,) * nd, pipeline_mode=pl.Buffered(1))


def _layer_spec(tail, l):
    nd = len(tail)
    return pl.BlockSpec((None,) + tuple(tail), lambda *_: (l,) + (0,) * nd, pipeline_mode=pl.Buffered(1))


def _params(sem):
    return pltpu.CompilerParams(dimension_semantics=sem, vmem_limit_bytes=VMEM_LIMIT)


def _mem_kv_kernel(mem_ref, g_ref, w_ref, xk_ref, xv_ref):
    hm = _rms(mem_ref[...], g_ref[...]).astype(_BF16)
    xk_ref[...] = _dot(hm, w_ref[:, :D_MODEL].astype(_BF16)).astype(_BF16)
    xv_ref[...] = _dot(hm, w_ref[:, D_MODEL:].astype(_BF16)).astype(_BF16)


def _mem_kv(l, mem2d, g, w_xkv):
    rows = mem2d.shape[0]
    return pl.pallas_call(
        _mem_kv_kernel,
        grid=(rows // TM,),
        in_specs=[pl.BlockSpec((TM, D_MODEL), lambda i: (i, 0)),
                  _layer_spec((1, D_MODEL), l),
                  _layer_spec((D_MODEL, 2 * D_MODEL), l)],
        out_specs=[pl.BlockSpec((TM, D_MODEL), lambda i: (i, 0))] * 2,
        out_shape=[jax.ShapeDtypeStruct((rows, D_MODEL), _BF16)] * 2,
        compiler_params=_params(("arbitrary",)),
        name="mem_kv",
    )(mem2d, g, w_xkv)


def _mix_in_kernel(tiles_per_seq,
                   x_ref, g_ref, w_ref, bf_ref, wpool_ref, pscale_ref, sgug_ref, sguw_ref,
                   bmat_ref, e_ref, en_ref, cq_ref, ck_ref,
                   pa_ref, q2_ref, k2_ref, v_ref, pc_ref,
                   halo_ref, fcarry_ref):
    jj = pl.program_id(0) % tiles_per_seq

    @pl.when(jj == 0)
    def _():
        halo_ref[...] = jnp.zeros_like(halo_ref)
        fcarry_ref[...] = jnp.zeros_like(fcarry_ref)

    h = _rms(x_ref[...], g_ref[...]).astype(_BF16)

    wf = jnp.concatenate([w_ref[OFF_F:OFF_C, :], jnp.zeros((LANES - FOX_HEADS, D_MODEL), _F32)], axis=0)
    z = _dot_t(h, wf.astype(_BF16)) + bf_ref[...]
    logf = jnp.minimum(z, 0.0) - jnp.log1p(jnp.exp(-jnp.abs(z)))
    row = lax.broadcasted_iota(jnp.int32, (TM, LANES), 0)
    c = logf
    shift = 1
    while shift < TM:
        c = c + jnp.where(row >= shift, pltpu.roll(c, shift, 0), 0.0)
        shift *= 2
    fcum = (c + fcarry_ref[...]) * LOG2E
    fcarry_ref[...] = fcarry_ref[...] + jnp.sum(logf, axis=0, keepdims=True)
    hi = fcum.astype(_BF16).astype(_F32)
    r1 = fcum - hi
    mid = r1.astype(_BF16).astype(_F32)
    lo = r1 - mid
    lane = lax.broadcasted_iota(jnp.int32, (TM, LANES), 1)
    packed = jnp.where(lane < FOX_HEADS, hi,
                       jnp.where(lane < 2 * FOX_HEADS, pltpu.roll(mid, FOX_HEADS, 1),
                                 jnp.where(lane < 3 * FOX_HEADS, pltpu.roll(lo, 2 * FOX_HEADS, 1), 0.0)))

    a = _dot_t(h, w_ref[OFF_A:OFF_Q, :].astype(_BF16))
    ext = jnp.concatenate([halo_ref[...], a], axis=0)
    halo_ref[...] = a[TM - POOL_HALO:, :]
    w2 = ext + pltpu.roll(ext, 1, 0)
    w4 = w2 + pltpu.roll(w2, 2, 0)
    w8 = w4 + pltpu.roll(w4, 4, 0)
    w16 = w8 + pltpu.roll(w8, 8, 0)
    grp = lax.broadcasted_iota(jnp.int32, (TM, POOL_WIDTH), 1) // POOL_GROUP_DIM
    wsel = jnp.where(grp == 0, w2[POOL_HALO:],
                     jnp.where(grp == 1, w4[POOL_HALO:],
                               jnp.where(grp == 2, w8[POOL_HALO:], w16[POOL_HALO:])))
    pos = jj * TM + lax.broadcasted_iota(jnp.int32, (TM, POOL_WIDTH), 0)
    win = jnp.left_shift(2, grp)
    cnt = jnp.minimum(pos + 1, win).astype(_F32)
    d = (wsel / cnt - a).astype(_BF16)

    q = _dot_t(h, w_ref[OFF_Q:OFF_K, :].astype(_BF16)) * (FOX_HEAD_DIM ** -0.5 * LOG2E)
    k = _dot_t(h, w_ref[OFF_K:OFF_V, :].astype(_BF16))
    v_ref[...] = _dot_t(h, w_ref[OFF_V:OFF_F, :].astype(_BF16)).astype(_BF16)
    qb = q.astype(_BF16)
    kb = k.astype(_BF16)
    for p in range(HEAD_PAIRS):
        q2_ref[:, p * PAIR_W:p * PAIR_W + LANES] = qb[:, p * LANES:(p + 1) * LANES]
        k2_ref[:, p * PAIR_W:p * PAIR_W + LANES] = kb[:, p * LANES:(p + 1) * LANES]
    qsq = (q * q).astype(_BF16)
    ksq = (k * k).astype(_BF16)

    cz = _dot_t(h, w_ref[OFF_C:OFF_G, :].astype(_BF16))
    pa_ref[...] = (_dot(d, wpool_ref[...]) * pscale_ref[...]).astype(_BF16)
    zc = cz * (0.5 * (1.0 + jnp.tanh(math.sqrt(2.0 / math.pi) * (cz + 0.044715 * (cz * cz * cz)))))
    u = zc[:, :SGU_WIDTH]
    vv = _rms(zc[:, SGU_WIDTH:], sgug_ref[...])
    wrow = lax.broadcasted_iota(jnp.int32, (SGU_GROUPS * SGU_CHUNK, SGU_CHUNK), 0) % SGU_CHUNK
    wcol = lax.broadcasted_iota(jnp.int32, (SGU_GROUPS * SGU_CHUNK, SGU_CHUNK), 1)
    wst = jnp.where(wcol <= wrow, sguw_ref[...], 0.0).astype(_BF16)
    cgrp = lax.broadcasted_iota(jnp.int32, (SGU_CHUNK, SGU_WIDTH), 1) // (SGU_WIDTH // SGU_GROUPS)
    for ci in range(TM // SGU_CHUNK):
        rs = slice(ci * SGU_CHUNK, (ci + 1) * SGU_CHUNK)
        r = _dot(wst, vv[rs].astype(_BF16))
        mixed = jnp.where(cgrp == 0, r[0:SGU_CHUNK],
                          jnp.where(cgrp == 1, r[SGU_CHUNK:2 * SGU_CHUNK],
                                    jnp.where(cgrp == 2, r[2 * SGU_CHUNK:3 * SGU_CHUNK],
                                              r[3 * SGU_CHUNK:])))
        pc_ref[rs, :] = (u[rs] * (mixed + bmat_ref[...])).astype(_BF16)

    aug = _dot(packed.astype(_BF16), e_ref[...])
    nq = jnp.minimum(_dot(qsq, en_ref[:, :LANES]), NORM_CAP)
    nk = jnp.minimum(_dot(ksq, en_ref[:, LANES:]), NORM_CAP)
    aq = (aug[:, :LANES] + cq_ref[...] + nq).astype(_BF16)
    ak = (aug[:, LANES:] + ck_ref[...] + nk).astype(_BF16)
    for p in range(HEAD_PAIRS):
        q2_ref[:, p * PAIR_W + LANES:(p + 1) * PAIR_W] = aq
        k2_ref[:, p * PAIR_W + LANES:(p + 1) * PAIR_W] = ak


def _mix_in(l, x2d, seq, g, w_in_t, bf, wpool, pscale, sgug, sguw, bmat, e, en, cq, ck):
    t = x2d.shape[0]
    row_spec = lambda w: pl.BlockSpec((TM, w), lambda i: (i, 0))
    return pl.pallas_call(
        functools.partial(_mix_in_kernel, seq // TM),
        grid=(t // TM,),
        in_specs=[row_spec(D_MODEL),
                  _layer_spec((1, D_MODEL), l),
                  _layer_spec((OFF_G, D_MODEL), l),
                  _layer_spec((1, LANES), l),
                  _layer_spec((POOL_WIDTH, POOL_WIDTH), l),
                  _layer_spec((1, POOL_WIDTH), l),
                  _layer_spec((1, SGU_WIDTH), l),
                  _layer_spec((SGU_GROUPS * SGU_CHUNK, SGU_CHUNK), l),
                  _layer_spec((SGU_CHUNK, SGU_WIDTH), l),
                  _const_spec((LANES, 2 * LANES)),
                  _const_spec((FOX_WIDTH, 2 * LANES)),
                  _const_spec((1, LANES)),
                  _const_spec((1, LANES))],
        out_specs=[row_spec(POOL_WIDTH), row_spec(HEAD_PAIRS * PAIR_W), row_spec(HEAD_PAIRS * PAIR_W),
                   row_spec(FOX_WIDTH), row_spec(SGU_WIDTH)],
        out_shape=[jax.ShapeDtypeStruct((t, POOL_WIDTH), _BF16),
                   jax.ShapeDtypeStruct((t, HEAD_PAIRS * PAIR_W), _BF16),
                   jax.ShapeDtypeStruct((t, HEAD_PAIRS * PAIR_W), _BF16),
                   jax.ShapeDtypeStruct((t, FOX_WIDTH), _BF16),
                   jax.ShapeDtypeStruct((t, SGU_WIDTH), _BF16)],
        scratch_shapes=[pltpu.VMEM((POOL_HALO, POOL_WIDTH), _F32),
                        pltpu.VMEM((1, LANES), _F32)],
        compiler_params=_params(("arbitrary",)),
        name="mix_in",
    )(x2d, g, w_in_t, bf, wpool, pscale, sgug, sguw, bmat, e, en, cq, ck)


FOX_LOGIT_BOUND = 64.0
NORM_SLACK = 1.1
COLMAX_SPLIT = 8


def _aug_colmax(ref):
    rows = ref.shape[0] // COLMAX_SPLIT
    parts = [ref[r * rows:(r + 1) * rows, LANES:] for r in range(COLMAX_SPLIT)]
    while len(parts) > 1:
        parts = [jnp.maximum(a, b) for a, b in zip(parts[::2], parts[1::2])]
    return jnp.max(parts[0], axis=0, keepdims=True).astype(_F32)


def _pair_logit_bound2(q2_ref, k2_ref, pair):
    qmax, kmax = _aug_colmax(q2_ref), _aug_colmax(k2_ref)
    prod = qmax * pltpu.roll(kmax, LANES - (KNORM_LANE - QNORM_LANE), 1)
    lane = lax.broadcasted_iota(jnp.int32, prod.shape, 1)
    first = 2 * pair * AUG_LANES + QNORM_LANE
    return jnp.max(jnp.where((lane == first) | (lane == first + AUG_LANES), prod, 0.0))


def _fox_kernel(seq, q2_ref, k2_ref, v_ref, o_ref, s_ref, p_ref, vext_ref):
    pair = pl.program_id(1)
    lane = lax.broadcasted_iota(jnp.int32, (TQ, PAIR_W), 1)
    keep = []
    for hh in range(2):
        aug_lo = LANES + (2 * pair + hh) * AUG_LANES
        keep.append(((lane >= hh * FOX_HEAD_DIM) & (lane < (hh + 1) * FOX_HEAD_DIM))
                    | ((lane >= aug_lo) & (lane < aug_lo + AUG_LANES)))
    drow = lax.broadcasted_iota(jnp.int32, (2 * TQ, TQ), 0) % TQ
    dcol = lax.broadcasted_iota(jnp.int32, (2 * TQ, TQ), 1)
    causal = dcol <= drow
    out_lane = lax.broadcasted_iota(jnp.int32, (TQ, LANES), 1)

    def stacked_q(i):
        qb = q2_ref[i * TQ:(i + 1) * TQ, :]
        zero = jnp.zeros_like(qb)
        return jnp.concatenate([jnp.where(keep[0], qb, zero), jnp.where(keep[1], qb, zero)], axis=0)

    def store_heads(i, o):
        o_ref[i * TQ:(i + 1) * TQ, :] = jnp.where(out_lane < FOX_HEAD_DIM, o[:TQ], o[TQ:]).astype(_BF16)

    bounded = _pair_logit_bound2(q2_ref, k2_ref, pair) <= FOX_LOGIT_BOUND * FOX_LOGIT_BOUND / NORM_SLACK

    @pl.when(bounded)
    def _():
        vext_ref[:, :LANES] = v_ref[...]
        vext_ref[:, LANES:] = jnp.ones((seq, LANES), _BF16)

        def probabilities(i):
            qm = stacked_q(i)
            for c in range(i + 1):
                sc = _dot_t(qm, k2_ref[c * TQ:(c + 1) * TQ, :])
                if c == i:
                    sc = jnp.where(causal, sc, NEG)
                p_ref[i % 2, :, c * TQ:(c + 1) * TQ] = jnp.exp2(sc).astype(_BF16)

        n_blocks = seq // TQ
        probabilities(0)
        for i in range(n_blocks):
            if i + 1 < n_blocks:
                probabilities(i + 1)
            kend = (i + 1) * TQ
            o = _dot(p_ref[i % 2, :, :kend], vext_ref[:kend, :])
            store_heads(i, o[:, :LANES] * (1.0 / o[:, LANES:]))

    @pl.when(jnp.logical_not(bounded))
    def _():
        for i in range(seq // TQ):
            kend = (i + 1) * TQ
            s_ref[:, :kend] = _dot_t(stacked_q(i), k2_ref[:kend, :])

            mrun = jnp.full((2 * TQ, LANES), NEG, _F32)
            for c in range(i + 1):
                sc = s_ref[:, c * TQ:(c + 1) * TQ]
                if c == i:
                    sc = jnp.where(causal, sc, NEG)
                    s_ref[:, c * TQ:(c + 1) * TQ] = sc
                mrun = jnp.maximum(mrun, jnp.maximum(sc[:, :LANES], sc[:, LANES:]))
            mb = jnp.broadcast_to(jnp.max(mrun, axis=1, keepdims=True), (2 * TQ, LANES))

            lrun = jnp.zeros((2 * TQ, LANES), _F32)
            for c in range(i + 1):
                p0 = jnp.exp2(s_ref[:, c * TQ:c * TQ + LANES] - mb)
                p1 = jnp.exp2(s_ref[:, c * TQ + LANES:(c + 1) * TQ] - mb)
                lrun = lrun + (p0 + p1)
                p_ref[0, :, c * TQ:c * TQ + LANES] = p0.astype(_BF16)
                p_ref[0, :, c * TQ + LANES:(c + 1) * TQ] = p1.astype(_BF16)
            linv = 1.0 / jnp.sum(lrun, axis=1, keepdims=True)
            store_heads(i, _dot(p_ref[0, :, :kend], v_ref[:kend, :]) * linv)


def _fox(q2, k2, v):
    b, seq, _ = q2.shape
    return pl.pallas_call(
        functools.partial(_fox_kernel, seq),
        grid=(b, HEAD_PAIRS),
        in_specs=[pl.BlockSpec((None, seq, PAIR_W), lambda bi, p: (bi, 0, p)),
                  pl.BlockSpec((None, seq, PAIR_W), lambda bi, p: (bi, 0, p)),
                  pl.BlockSpec((None, seq, LANES), lambda bi, p: (bi, 0, p))],
        out_specs=pl.BlockSpec((None, seq, LANES), lambda bi, p: (bi, 0, p)),
        out_shape=jax.ShapeDtypeStruct((b, seq, FOX_WIDTH), _BF16),
        scratch_shapes=[pltpu.VMEM((2 * TQ, seq), _F32),
                        pltpu.VMEM((2, 2 * TQ, seq), _BF16),
                        pltpu.VMEM((seq, 2 * LANES), _BF16)],
        compiler_params=_params(("arbitrary", "arbitrary")),
        name="fox_attention",
    )(q2, k2, v)


def _merge_xattn_kernel(x_ref, pa_ref, att_ref, pc_ref, gmix_ref, wg_ref, bg_ref,
                        wba_ref, wbb_ref, wbc_ref, wout_ref, gx_ref, wxq_ref,
                        xk_ref, xv_ref, wxo_ref, o_ref):
    branches = [_dot(b_ref[...], wb_ref[...].astype(_BF16))
                for b_ref, wb_ref in ((pa_ref, wba_ref), (att_ref, wbb_ref), (pc_ref, wbc_ref))]
    x = x_ref[...]
    h = _rms(x, gmix_ref[...]).astype(_BF16)

    def gated(j):
        cols = slice(j * D_MODEL, (j + 1) * D_MODEL)
        gate = 1.0 / (1.0 + jnp.exp(-(_dot_t(h, wg_ref[cols, :].astype(_BF16)) + bg_ref[:, cols])))
        return gate * branches[j]

    merged = gated(0) + gated(1) + gated(2)
    x1 = x + _dot(merged.astype(_BF16), wout_ref[...].astype(_BF16))

    hx = _rms(x1, gx_ref[...]).astype(_BF16)
    xq = (_dot(hx, wxq_ref[...].astype(_BF16)) * (XATTN_HEAD_DIM ** -0.5 * LOG2E)).astype(_BF16)
    head_cols = [slice(hd * XATTN_HEAD_DIM, (hd + 1) * XATTN_HEAD_DIM) for hd in range(XATTN_HEADS)]
    scores = [_dot_t(xq[:, cols], xk_ref[:, cols]) for cols in head_cols]
    heads = []
    for s, cols in zip(scores, head_cols):
        p = jnp.exp2(s - jnp.max(s, axis=1, keepdims=True))
        linv = 1.0 / jnp.sum(p, axis=1, keepdims=True)
        heads.append((_dot(p.astype(_BF16), xv_ref[:, cols]) * linv).astype(_BF16))
    o = jnp.concatenate(heads, axis=1)
    o_ref[...] = x1 + _dot(o, wxo_ref[...].astype(_BF16))


def _merge_xattn(l, x2d, seq, pa, att, pc, gmix, w_in_t, b_gate, wba, wbb, wbc, w_out, gx, w_xq, xk, xv, w_xo):
    t = x2d.shape[0]
    mlen = xk.shape[1]
    tiles_per_seq = seq // TM
    row_spec = lambda w: pl.BlockSpec((TM, w), lambda i: (i, 0))
    mem_spec = pl.BlockSpec((None, mlen, D_MODEL), lambda i: (i // tiles_per_seq, 0, 0))
    return pl.pallas_call(
        _merge_xattn_kernel,
        grid=(t // TM,),
        in_specs=[row_spec(D_MODEL), row_spec(POOL_WIDTH), row_spec(FOX_WIDTH), row_spec(SGU_WIDTH),
                  _layer_spec((1, D_MODEL), l),
                  pl.BlockSpec((None, pl.Element(3 * D_MODEL), pl.Element(D_MODEL)),
                               lambda i: (l, OFF_G, 0), pipeline_mode=pl.Buffered(1)),
                  _layer_spec((1, 3 * D_MODEL), l),
                  _layer_spec((POOL_WIDTH, D_MODEL), l),
                  _layer_spec((FOX_WIDTH, D_MODEL), l),
                  _layer_spec((SGU_WIDTH, D_MODEL), l),
                  _layer_spec((D_MODEL, D_MODEL), l),
                  _layer_spec((1, D_MODEL), l),
                  _layer_spec((D_MODEL, D_MODEL), l),
                  mem_spec, mem_spec,
                  _layer_spec((D_MODEL, D_MODEL), l)],
        out_specs=row_spec(D_MODEL),
        out_shape=jax.ShapeDtypeStruct((t, D_MODEL), _F32),
        compiler_params=_params(("arbitrary",)),
        name="merge_xattn",
    )(x2d, pa, att, pc, gmix, w_in_t, b_gate, wba, wbb, wbc, w_out, gx, w_xq, xk, xv, w_xo)


FF_CHUNK = 1024


def _ffn_kernel(final_norm, x_ref, g_ref, w1_ref, w2_ref, gfin_ref, o_ref, act_ref):
    x = x_ref[...]
    hf = _rms(x, g_ref[...]).astype(_BF16)
    for c in range(D_FF // FF_CHUNK):
        cols = slice(c * FF_CHUNK, (c + 1) * FF_CHUNK)
        r = jnp.maximum(_dot(hf, w1_ref[:, cols].astype(_BF16)), 0.0)
        act_ref[:, cols] = (r * r).astype(_BF16)
    y = x + _dot(act_ref[...], w2_ref[...].astype(_BF16))
    if final_norm:
        y = _rms(y, gfin_ref[...])
    o_ref[...] = y


def _ffn(l, x2d, g, w1, w2, gfin, final_norm):
    t = x2d.shape[0]
    return pl.pallas_call(
        functools.partial(_ffn_kernel, final_norm),
        grid=(t // TM,),
        in_specs=[pl.BlockSpec((TM, D_MODEL), lambda i: (i, 0)),
                  _layer_spec((1, D_MODEL), l),
                  _layer_spec((D_MODEL, D_FF), l),
                  _layer_spec((D_FF, D_MODEL), l),
                  _const_spec((1, D_MODEL))],
        out_specs=pl.BlockSpec((TM, D_MODEL), lambda i: (i, 0)),
        out_shape=jax.ShapeDtypeStruct((t, D_MODEL), _F32),
        scratch_shapes=[pltpu.VMEM((TM, D_FF), _BF16)],
        compiler_params=_params(("arbitrary",)),
        name="ffn",
    )(x2d, g, w1, w2, gfin)


def _aug_constants():
    e = np.zeros((LANES, 2 * LANES), np.float32)
    en = np.zeros((FOX_WIDTH, 2 * LANES), np.float32)
    cq = np.zeros((1, LANES), np.float32)
    ck = np.zeros((1, LANES), np.float32)
    for h in range(FOX_HEADS):
        for piece in range(3):
            e[piece * FOX_HEADS + h, h * AUG_LANES + piece] = 1.0
            e[piece * FOX_HEADS + h, LANES + h * AUG_LANES + 3 + piece] = -1.0
            cq[0, h * AUG_LANES + 3 + piece] = 1.0
            ck[0, h * AUG_LANES + piece] = 1.0
        cols = slice(h * FOX_HEAD_DIM, (h + 1) * FOX_HEAD_DIM)
        en[cols, h * AUG_LANES + QNORM_LANE] = 1.0
        en[cols, LANES + h * AUG_LANES + KNORM_LANE] = 1.0
    return jnp.asarray(e, _BF16), jnp.asarray(en, _BF16), jnp.asarray(cq), jnp.asarray(ck)


def kernel(x, mem, norm_mix_g, w_in, b_forget, pool_w, pool_scale, sgu_norm_g, sgu_w, sgu_b,
           w_branch_a, w_branch_b, w_branch_c, b_gate, w_out, norm_xattn_g, norm_mem_g,
           w_xq, w_xkv, w_xo, norm_ffn_g, w_ff1, w_ff2, final_norm_g):
    b, seq, d = x.shape
    mlen = mem.shape[1]
    t = b * seq
    e, en, cq, ck = _aug_constants()
    rows = lambda a: a.reshape(a.shape[0], 1, -1)
    n_groups = len(POOL_WINDOWS)

    w_in_t = jnp.swapaxes(w_in, 1, 2)
    bf = rows(jnp.pad(b_forget, ((0, 0), (0, LANES - FOX_HEADS))))
    eye = jnp.eye(n_groups, dtype=pool_w.dtype)
    wpool = jnp.einsum('lgcd,gh->lgchd', pool_w, eye).reshape(DEPTH, POOL_WIDTH, POOL_WIDTH).astype(_BF16)
    sguw = sgu_w.reshape(DEPTH, SGU_GROUPS * SGU_CHUNK, SGU_CHUNK)
    bmat = jnp.repeat(jnp.swapaxes(sgu_b, 1, 2), SGU_WIDTH // SGU_GROUPS, axis=2)
    g_mix, g_mem, g_x, g_ffn = rows(norm_mix_g), rows(norm_mem_g), rows(norm_xattn_g), rows(norm_ffn_g)
    pscale, sgug, bg = rows(pool_scale), rows(sgu_norm_g), rows(b_gate)

    xc = x.reshape(t, d)
    mem2d = mem.reshape(b * mlen, d)
    for l in range(DEPTH):
        xk, xv = _mem_kv(l, mem2d, g_mem, w_xkv)
        pa, q2, k2, v, pc = _mix_in(l, xc, seq, g_mix, w_in_t, bf, wpool, pscale, sgug, sguw, bmat, e, en, cq, ck)
        att = _fox(q2.reshape(b, seq, -1), k2.reshape(b, seq, -1), v.reshape(b, seq, -1))
        xc = _merge_xattn(l, xc, seq, pa, att.reshape(t, -1), pc, g_mix, w_in_t, bg,
                          w_branch_a, w_branch_b, w_branch_c, w_out, g_x, w_xq,
                          xk.reshape(b, mlen, d), xv.reshape(b, mlen, d), w_xo)
        xc = _ffn(l, xc, g_ffn, w_ff1, w_ff2, final_norm_g.reshape(1, d), l == DEPTH - 1)
    return xc.reshape(b, seq, d)
```

```python
import functools
import math

import numpy as np
import jax
import jax.numpy as jnp
from jax import lax
from jax.experimental import pallas as pl
from jax.experimental.pallas import tpu as pltpu

D_MODEL = 1024
DEPTH = 2
POOL_GROUP_DIM = 64
POOL_WIDTH = 256
POOL_WINDOWS = (2, 4, 8, 16)
POOL_HALO = 16
FOX_HEADS = 8
FOX_HEAD_DIM = 64
FOX_WIDTH = 512
SGU_WIDTH = 256
SGU_GROUPS = 4
SGU_CHUNK = 128
XATTN_HEADS = 4
XATTN_HEAD_DIM = 256
D_FF = 4096
EPS = 1e-6
NEG = -1e30
LOG2E = math.log2(math.e)

LANES = 128
AUG_LANES = 8
QNORM_LANE = 6
KNORM_LANE = 7
NORM_CAP = 1e30
HEAD_PAIRS = FOX_HEADS // 2
PAIR_W = 2 * LANES

OFF_A = 0
OFF_Q = OFF_A + POOL_WIDTH
OFF_K = OFF_Q + FOX_WIDTH
OFF_V = OFF_K + FOX_WIDTH
OFF_F = OFF_V + FOX_WIDTH
OFF_C = OFF_F + FOX_HEADS
OFF_G = OFF_C + 2 * SGU_WIDTH
N_IN = OFF_G + 3 * D_MODEL

TM = 512
TQ = 256
VMEM_LIMIT = 56 * 1024 * 1024

_F32 = jnp.float32
_BF16 = jnp.bfloat16


def _dot(a, b):
    return jnp.dot(a, b, preferred_element_type=_F32)


def _dot_t(a, b):
    return lax.dot_general(a, b, (((1,), (1,)), ((), ())), preferred_element_type=_F32)


def _rms(x, g):
    ms = jnp.mean(x * x, axis=-1, keepdims=True)
    return x * lax.rsqrt(ms + EPS) * g


def _const_spec(shape):
    nd = len(shape)
    return pl.BlockSpec(shape, lambda *_: (0,) * nd, pipeline_mode=pl.Buffered(1))


def _layer_spec(tail, l):
    nd = len(tail)
    return pl.BlockSpec((None,) + tuple(tail), lambda *_: (l,) + (0,) * nd, pipeline_mode=pl.Buffered(1))


def _params(sem):
    return pltpu.CompilerParams(dimension_semantics=sem, vmem_limit_bytes=VMEM_LIMIT)


def _mem_kv_kernel(mem_ref, g_ref, w_ref, xk_ref, xv_ref):
    hm = _rms(mem_ref[...], g_ref[...]).astype(_BF16)
    xk_ref[...] = _dot(hm, w_ref[:, :D_MODEL].astype(_BF16)).astype(_BF16)
    xv_ref[...] = _dot(hm, w_ref[:, D_MODEL:].astype(_BF16)).astype(_BF16)


def _mem_kv(l, mem2d, g, w_xkv):
    rows = mem2d.shape[0]
    return pl.pallas_call(
        _mem_kv_kernel,
        grid=(rows // TM,),
        in_specs=[pl.BlockSpec((TM, D_MODEL), lambda i: (i, 0)),
                  _layer_spec((1, D_MODEL), l),
                  _layer_spec((D_MODEL, 2 * D_MODEL), l)],
        out_specs=[pl.BlockSpec((TM, D_MODEL), lambda i: (i, 0))] * 2,
        out_shape=[jax.ShapeDtypeStruct((rows, D_MODEL), _BF16)] * 2,
        compiler_params=_params(("arbitrary",)),
        name="mem_kv",
    )(mem2d, g, w_xkv)


def _mix_in_kernel(tiles_per_seq,
                   x_ref, g_ref, w_ref, bf_ref, wpool_ref, pscale_ref, sgug_ref, sguw_ref,
                   bmat_ref, e_ref, en_ref, cq_ref, ck_ref,
                   pa_ref, q2_ref, k2_ref, v_ref, pc_ref,
                   halo_ref, fcarry_ref):
    jj = pl.program_id(0) % tiles_per_seq

    @pl.when(jj == 0)
    def _():
        halo_ref[...] = jnp.zeros_like(halo_ref)
        fcarry_ref[...] = jnp.zeros_like(fcarry_ref)

    h = _rms(x_ref[...], g_ref[...]).astype(_BF16)

    wf = jnp.concatenate([w_ref[OFF_F:OFF_C, :], jnp.zeros((LANES - FOX_HEADS, D_MODEL), _F32)], axis=0)
    z = _dot_t(h, wf.astype(_BF16)) + bf_ref[...]
    logf = jnp.minimum(z, 0.0) - jnp.log1p(jnp.exp(-jnp.abs(z)))
    row = lax.broadcasted_iota(jnp.int32, (TM, LANES), 0)
    c = logf
    shift = 1
    while shift < TM:
        c = c + jnp.where(row >= shift, pltpu.roll(c, shift, 0), 0.0)
        shift *= 2
    fcum = (c + fcarry_ref[...]) * LOG2E
    fcarry_ref[...] = fcarry_ref[...] + jnp.sum(logf, axis=0, keepdims=True)
    hi = fcum.astype(_BF16).astype(_F32)
    r1 = fcum - hi
    mid = r1.astype(_BF16).astype(_F32)
    lo = r1 - mid
    lane = lax.broadcasted_iota(jnp.int32, (TM, LANES), 1)
    packed = jnp.where(lane < FOX_HEADS, hi,
                       jnp.where(lane < 2 * FOX_HEADS, pltpu.roll(mid, FOX_HEADS, 1),
                                 jnp.where(lane < 3 * FOX_HEADS, pltpu.roll(lo, 2 * FOX_HEADS, 1), 0.0)))

    a = _dot_t(h, w_ref[OFF_A:OFF_Q, :].astype(_BF16))
    ext = jnp.concatenate([halo_ref[...], a], axis=0)
    halo_ref[...] = a[TM - POOL_HALO:, :]
    w2 = ext + pltpu.roll(ext, 1, 0)
    w4 = w2 + pltpu.roll(w2, 2, 0)
    w8 = w4 + pltpu.roll(w4, 4, 0)
    w16 = w8 + pltpu.roll(w8, 8, 0)
    grp = lax.broadcasted_iota(jnp.int32, (TM, POOL_WIDTH), 1) // POOL_GROUP_DIM
    wsel = jnp.where(grp == 0, w2[POOL_HALO:],
                     jnp.where(grp == 1, w4[POOL_HALO:],
                               jnp.where(grp == 2, w8[POOL_HALO:], w16[POOL_HALO:])))
    pos = jj * TM + lax.broadcasted_iota(jnp.int32, (TM, POOL_WIDTH), 0)
    win = jnp.left_shift(2, grp)
    cnt = jnp.minimum(pos + 1, win).astype(_F32)
    d = (wsel / cnt - a).astype(_BF16)

    q = _dot_t(h, w_ref[OFF_Q:OFF_K, :].astype(_BF16)) * (FOX_HEAD_DIM ** -0.5 * LOG2E)
    k = _dot_t(h, w_ref[OFF_K:OFF_V, :].astype(_BF16))
    v_ref[...] = _dot_t(h, w_ref[OFF_V:OFF_F, :].astype(_BF16)).astype(_BF16)
    qb = q.astype(_BF16)
    kb = k.astype(_BF16)
    for p in range(HEAD_PAIRS):
        q2_ref[:, p * PAIR_W:p * PAIR_W + LANES] = qb[:, p * LANES:(p + 1) * LANES]
        k2_ref[:, p * PAIR_W:p * PAIR_W + LANES] = kb[:, p * LANES:(p + 1) * LANES]
    qsq = (q * q).astype(_BF16)
    ksq = (k * k).astype(_BF16)

    cz = _dot_t(h, w_ref[OFF_C:OFF_G, :].astype(_BF16))
    pa_ref[...] = (_dot(d, wpool_ref[...]) * pscale_ref[...]).astype(_BF16)
    zc = cz * (0.5 * (1.0 + jnp.tanh(math.sqrt(2.0 / math.pi) * (cz + 0.044715 * (cz * cz * cz)))))
    u = zc[:, :SGU_WIDTH]
    vv = _rms(zc[:, SGU_WIDTH:], sgug_ref[...])
    wrow = lax.broadcasted_iota(jnp.int32, (SGU_GROUPS * SGU_CHUNK, SGU_CHUNK), 0) % SGU_CHUNK
    wcol = lax.broadcasted_iota(jnp.int32, (SGU_GROUPS * SGU_CHUNK, SGU_CHUNK), 1)
    wst = jnp.where(wcol <= wrow, sguw_ref[...], 0.0).astype(_BF16)
    cgrp = lax.broadcasted_iota(jnp.int32, (SGU_CHUNK, SGU_WIDTH), 1) // (SGU_WIDTH // SGU_GROUPS)
    for ci in range(TM // SGU_CHUNK):
        rs = slice(ci * SGU_CHUNK, (ci + 1) * SGU_CHUNK)
        r = _dot(wst, vv[rs].astype(_BF16))
        mixed = jnp.where(cgrp == 0, r[0:SGU_CHUNK],
                          jnp.where(cgrp == 1, r[SGU_CHUNK:2 * SGU_CHUNK],
                                    jnp.where(cgrp == 2, r[2 * SGU_CHUNK:3 * SGU_CHUNK],
                                              r[3 * SGU_CHUNK:])))
        pc_ref[rs, :] = (u[rs] * (mixed + bmat_ref[...])).astype(_BF16)

    aug = _dot(packed.astype(_BF16), e_ref[...])
    nq = jnp.minimum(_dot(qsq, en_ref[:, :LANES]), NORM_CAP)
    nk = jnp.minimum(_dot(ksq, en_ref[:, LANES:]), NORM_CAP)
    aq = (aug[:, :LANES] + cq_ref[...] + nq).astype(_BF16)
    ak = (aug[:, LANES:] + ck_ref[...] + nk).astype(_BF16)
    for p in range(HEAD_PAIRS):
        q2_ref[:, p * PAIR_W + LANES:(p + 1) * PAIR_W] = aq
        k2_ref[:, p * PAIR_W + LANES:(p + 1) * PAIR_W] = ak


def _mix_in(l, x2d, seq, g, w_in_t, bf, wpool, pscale, sgug, sguw, bmat, e, en, cq, ck):
    t = x2d.shape[0]
    row_spec = lambda w: pl.BlockSpec((TM, w), lambda i: (i, 0))
    return pl.pallas_call(
        functools.partial(_mix_in_kernel, seq // TM),
        grid=(t // TM,),
        in_specs=[row_spec(D_MODEL),
                  _layer_spec((1, D_MODEL), l),
                  _layer_spec((OFF_G, D_MODEL), l),
                  _layer_spec((1, LANES), l),
                  _layer_spec((POOL_WIDTH, POOL_WIDTH), l),
                  _layer_spec((1, POOL_WIDTH), l),
                  _layer_spec((1, SGU_WIDTH), l),
                  _layer_spec((SGU_GROUPS * SGU_CHUNK, SGU_CHUNK), l),
                  _layer_spec((SGU_CHUNK, SGU_WIDTH), l),
                  _const_spec((LANES, 2 * LANES)),
                  _const_spec((FOX_WIDTH, 2 * LANES)),
                  _const_spec((1, LANES)),
                  _const_spec((1, LANES))],
        out_specs=[row_spec(POOL_WIDTH), row_spec(HEAD_PAIRS * PAIR_W), row_spec(HEAD_PAIRS * PAIR_W),
                   row_spec(FOX_WIDTH), row_spec(SGU_WIDTH)],
        out_shape=[jax.ShapeDtypeStruct((t, POOL_WIDTH), _BF16),
                   jax.ShapeDtypeStruct((t, HEAD_PAIRS * PAIR_W), _BF16),
                   jax.ShapeDtypeStruct((t, HEAD_PAIRS * PAIR_W), _BF16),
                   jax.ShapeDtypeStruct((t, FOX_WIDTH), _BF16),
                   jax.ShapeDtypeStruct((t, SGU_WIDTH), _BF16)],
        scratch_shapes=[pltpu.VMEM((POOL_HALO, POOL_WIDTH), _F32),
                        pltpu.VMEM((1, LANES), _F32)],
        compiler_params=_params(("arbitrary",)),
        name="mix_in",
    )(x2d, g, w_in_t, bf, wpool, pscale, sgug, sguw, bmat, e, en, cq, ck)


FOX_LOGIT_BOUND = 64.0
NORM_SLACK = 1.1
COLMAX_SPLIT = 8


def _aug_colmax(ref):
    rows = ref.shape[0] // COLMAX_SPLIT
    parts = [ref[r * rows:(r + 1) * rows, LANES:] for r in range(COLMAX_SPLIT)]
    while len(parts) > 1:
        parts = [jnp.maximum(a, b) for a, b in zip(parts[::2], parts[1::2])]
    return jnp.max(parts[0], axis=0, keepdims=True).astype(_F32)


def _pair_logit_bound2(q2_ref, k2_ref, pair):
    qmax, kmax = _aug_colmax(q2_ref), _aug_colmax(k2_ref)
    prod = qmax * pltpu.roll(kmax, LANES - (KNORM_LANE - QNORM_LANE), 1)
    lane = lax.broadcasted_iota(jnp.int32, prod.shape, 1)
    first = 2 * pair * AUG_LANES + QNORM_LANE
    return jnp.max(jnp.where((lane == first) | (lane == first + AUG_LANES), prod, 0.0))


def _fox_kernel(seq, q2_ref, k2_ref, v_ref, o_ref, s_ref, p_ref, vext_ref):
    pair = pl.program_id(1)
    lane = lax.broadcasted_iota(jnp.int32, (TQ, PAIR_W), 1)
    keep = []
    for hh in range(2):
        aug_lo = LANES + (2 * pair + hh) * AUG_LANES
        keep.append(((lane >= hh * FOX_HEAD_DIM) & (lane < (hh + 1) * FOX_HEAD_DIM))
                    | ((lane >= aug_lo) & (lane < aug_lo + AUG_LANES)))
    drow = lax.broadcasted_iota(jnp.int32, (2 * TQ, TQ), 0) % TQ
    dcol = lax.broadcasted_iota(jnp.int32, (2 * TQ, TQ), 1)
    causal = dcol <= drow
    out_lane = lax.broadcasted_iota(jnp.int32, (TQ, LANES), 1)

    def stacked_q(i):
        qb = q2_ref[i * TQ:(i + 1) * TQ, :]
        zero = jnp.zeros_like(qb)
        return jnp.concatenate([jnp.where(keep[0], qb, zero), jnp.where(keep[1], qb, zero)], axis=0)

    def store_heads(i, o):
        o_ref[i * TQ:(i + 1) * TQ, :] = jnp.where(out_lane < FOX_HEAD_DIM, o[:TQ], o[TQ:]).astype(_BF16)

    bounded = _pair_logit_bound2(q2_ref, k2_ref, pair) <= FOX_LOGIT_BOUND * FOX_LOGIT_BOUND / NORM_SLACK

    @pl.when(bounded)
    def _():
        vext_ref[:, :LANES] = v_ref[...]
        vext_ref[:, LANES:] = jnp.ones((seq, LANES), _BF16)

        def probabilities(i):
            qm = stacked_q(i)
            for c in range(i + 1):
                sc = _dot_t(qm, k2_ref[c * TQ:(c + 1) * TQ, :])
                if c == i:
                    sc = jnp.where(causal, sc, NEG)
                p_ref[i % 2, :, c * TQ:(c + 1) * TQ] = jnp.exp2(sc).astype(_BF16)

        n_blocks = seq // TQ
        probabilities(0)
        for i in range(n_blocks):
            if i + 1 < n_blocks:
                probabilities(i + 1)
            kend = (i + 1) * TQ
            o = _dot(p_ref[i % 2, :, :kend], vext_ref[:kend, :])
            store_heads(i, o[:, :LANES] * (1.0 / o[:, LANES:]))

    @pl.when(jnp.logical_not(bounded))
    def _():
        for i in range(seq // TQ):
            kend = (i + 1) * TQ
            s_ref[:, :kend] = _dot_t(stacked_q(i), k2_ref[:kend, :])

            mrun = jnp.full((2 * TQ, LANES), NEG, _F32)
            for c in range(i + 1):
                sc = s_ref[:, c * TQ:(c + 1) * TQ]
                if c == i:
                    sc = jnp.where(causal, sc, NEG)
                    s_ref[:, c * TQ:(c + 1) * TQ] = sc
                mrun = jnp.maximum(mrun, jnp.maximum(sc[:, :LANES], sc[:, LANES:]))
            mb = jnp.broadcast_to(jnp.max(mrun, axis=1, keepdims=True), (2 * TQ, LANES))

            lrun = jnp.zeros((2 * TQ, LANES), _F32)
            for c in range(i + 1):
                p0 = jnp.exp2(s_ref[:, c * TQ:c * TQ + LANES] - mb)
                p1 = jnp.exp2(s_ref[:, c * TQ + LANES:(c + 1) * TQ] - mb)
                lrun = lrun + (p0 + p1)
                p_ref[0, :, c * TQ:c * TQ + LANES] = p0.astype(_BF16)
                p_ref[0, :, c * TQ + LANES:(c + 1) * TQ] = p1.astype(_BF16)
            linv = 1.0 / jnp.sum(lrun, axis=1, keepdims=True)
            store_heads(i, _dot(p_ref[0, :, :kend], v_ref[:kend, :]) * linv)


def _fox(q2, k2, v):
    b, seq, _ = q2.shape
    return pl.pallas_call(
        functools.partial(_fox_kernel, seq),
        grid=(b, HEAD_PAIRS),
        in_specs=[pl.BlockSpec((None, seq, PAIR_W), lambda bi, p: (bi, 0, p)),
                  pl.BlockSpec((None, seq, PAIR_W), lambda bi, p: (bi, 0, p)),
                  pl.BlockSpec((None, seq, LANES), lambda bi, p: (bi, 0, p))],
        out_specs=pl.BlockSpec((None, seq, LANES), lambda bi, p: (bi, 0, p)),
        out_shape=jax.ShapeDtypeStruct((b, seq, FOX_WIDTH), _BF16),
        scratch_shapes=[pltpu.VMEM((2 * TQ, seq), _F32),
                        pltpu.VMEM((2, 2 * TQ, seq), _BF16),
                        pltpu.VMEM((seq, 2 * LANES), _BF16)],
        compiler_params=_params(("arbitrary", "arbitrary")),
        name="fox_attention",
    )(q2, k2, v)


MERGE_CHUNK = 256


def _merge_xattn_kernel(x_ref, pa_ref, att_ref, pc_ref, gmix_ref, wg_ref, bg_ref,
                        wba_ref, wbb_ref, wbc_ref, wout_ref, gx_ref, wxq_ref,
                        xk_ref, xv_ref, wxo_ref, o_ref, merged_ref, heads_ref):
    x = x_ref[...]
    h = _rms(x, gmix_ref[...]).astype(_BF16)
    branches = ((pa_ref, wba_ref), (att_ref, wbb_ref), (pc_ref, wbc_ref))
    for n in range(D_MODEL // MERGE_CHUNK):
        cols = slice(n * MERGE_CHUNK, (n + 1) * MERGE_CHUNK)
        acc = None
        for j, (b_ref, wb_ref) in enumerate(branches):
            gcols = slice(j * D_MODEL + n * MERGE_CHUNK, j * D_MODEL + (n + 1) * MERGE_CHUNK)
            zg = _dot_t(h, wg_ref[gcols, :].astype(_BF16)) + bg_ref[:, gcols]
            gate = 0.5 * jnp.tanh(0.5 * zg) + 0.5
            term = gate * _dot(b_ref[...], wb_ref[:, cols].astype(_BF16))
            acc = term if acc is None else acc + term
        merged_ref[:, cols] = acc.astype(_BF16)
    x1 = x + _dot(merged_ref[...], wout_ref[...].astype(_BF16))

    hx = _rms(x1, gx_ref[...]).astype(_BF16)
    xq = (_dot(hx, wxq_ref[...].astype(_BF16)) * (XATTN_HEAD_DIM ** -0.5 * LOG2E)).astype(_BF16)
    for hd in range(XATTN_HEADS):
        cols = slice(hd * XATTN_HEAD_DIM, (hd + 1) * XATTN_HEAD_DIM)
        s = _dot_t(xq[:, cols], xk_ref[:, cols])
        p = jnp.exp2(s - jnp.max(s, axis=1, keepdims=True))
        linv = 1.0 / jnp.sum(p, axis=1, keepdims=True)
        heads_ref[:, cols] = (_dot(p.astype(_BF16), xv_ref[:, cols]) * linv).astype(_BF16)
    o_ref[...] = x1 + _dot(heads_ref[...], wxo_ref[...].astype(_BF16))


def _merge_xattn(l, x2d, seq, pa, att, pc, gmix, w_in_t, b_gate, wba, wbb, wbc, w_out, gx, w_xq, xk, xv, w_xo):
    t = x2d.shape[0]
    mlen = xk.shape[1]
    tiles_per_seq = seq // TM
    row_spec = lambda w: pl.BlockSpec((TM, w), lambda i: (i, 0))
    mem_spec = pl.BlockSpec((None, mlen, D_MODEL), lambda i: (i // tiles_per_seq, 0, 0))
    return pl.pallas_call(
        _merge_xattn_kernel,
        grid=(t // TM,),
        in_specs=[row_spec(D_MODEL), row_spec(POOL_WIDTH), row_spec(FOX_WIDTH), row_spec(SGU_WIDTH),
                  _layer_spec((1, D_MODEL), l),
                  pl.BlockSpec((None, pl.Element(3 * D_MODEL), pl.Element(D_MODEL)),
                               lambda i: (l, OFF_G, 0), pipeline_mode=pl.Buffered(1)),
                  _layer_spec((1, 3 * D_MODEL), l),
                  _layer_spec((POOL_WIDTH, D_MODEL), l),
                  _layer_spec((FOX_WIDTH, D_MODEL), l),
                  _layer_spec((SGU_WIDTH, D_MODEL), l),
                  _layer_spec((D_MODEL, D_MODEL), l),
                  _layer_spec((1, D_MODEL), l),
                  _layer_spec((D_MODEL, D_MODEL), l),
                  mem_spec, mem_spec,
                  _layer_spec((D_MODEL, D_MODEL), l)],
        out_specs=row_spec(D_MODEL),
        out_shape=jax.ShapeDtypeStruct((t, D_MODEL), _F32),
        scratch_shapes=[pltpu.VMEM((TM, D_MODEL), _BF16),
                        pltpu.VMEM((TM, D_MODEL), _BF16)],
        compiler_params=_params(("arbitrary",)),
        name="merge_xattn",
    )(x2d, pa, att, pc, gmix, w_in_t, b_gate, wba, wbb, wbc, w_out, gx, w_xq, xk, xv, w_xo)


FF_CHUNK = 1024


def _ffn_kernel(final_norm, x_ref, g_ref, w1_ref, w2_ref, gfin_ref, o_ref, act_ref):
    x = x_ref[...]
    hf = _rms(x, g_ref[...]).astype(_BF16)
    for c in range(D_FF // FF_CHUNK):
        cols = slice(c * FF_CHUNK, (c + 1) * FF_CHUNK)
        r = jnp.maximum(_dot(hf, w1_ref[:, cols].astype(_BF16)), 0.0)
        act_ref[:, cols] = (r * r).astype(_BF16)
    y = x + _dot(act_ref[...], w2_ref[...].astype(_BF16))
    if final_norm:
        y = _rms(y, gfin_ref[...])
    o_ref[...] = y


def _ffn(l, x2d, g, w1, w2, gfin, final_norm):
    t = x2d.shape[0]
    return pl.pallas_call(
        functools.partial(_ffn_kernel, final_norm),
        grid=(t // TM,),
        in_specs=[pl.BlockSpec((TM, D_MODEL), lambda i: (i, 0)),
                  _layer_spec((1, D_MODEL), l),
                  _layer_spec((D_MODEL, D_FF), l),
                  _layer_spec((D_FF, D_MODEL), l),
                  _const_spec((1, D_MODEL))],
        out_specs=pl.BlockSpec((TM, D_MODEL), lambda i: (i, 0)),
        out_shape=jax.ShapeDtypeStruct((t, D_MODEL), _F32),
        scratch_shapes=[pltpu.VMEM((TM, D_FF), _BF16)],
        compiler_params=_params(("arbitrary",)),
        name="ffn",
    )(x2d, g, w1, w2, gfin)


def _aug_constants():
    e = np.zeros((LANES, 2 * LANES), np.float32)
    en = np.zeros((FOX_WIDTH, 2 * LANES), np.float32)
    cq = np.zeros((1, LANES), np.float32)
    ck = np.zeros((1, LANES), np.float32)
    for h in range(FOX_HEADS):
        for piece in range(3):
            e[piece * FOX_HEADS + h, h * AUG_LANES + piece] = 1.0
            e[piece * FOX_HEADS + h, LANES + h * AUG_LANES + 3 + piece] = -1.0
            cq[0, h * AUG_LANES + 3 + piece] = 1.0
            ck[0, h * AUG_LANES + piece] = 1.0
        cols = slice(h * FOX_HEAD_DIM, (h + 1) * FOX_HEAD_DIM)
        en[cols, h * AUG_LANES + QNORM_LANE] = 1.0
        en[cols, LANES + h * AUG_LANES + KNORM_LANE] = 1.0
    return jnp.asarray(e, _BF16), jnp.asarray(en, _BF16), jnp.asarray(cq), jnp.asarray(ck)


def kernel(x, mem, norm_mix_g, w_in, b_forget, pool_w, pool_scale, sgu_norm_g, sgu_w, sgu_b,
           w_branch_a, w_branch_b, w_branch_c, b_gate, w_out, norm_xattn_g, norm_mem_g,
           w_xq, w_xkv, w_xo, norm_ffn_g, w_ff1, w_ff2, final_norm_g):
    b, seq, d = x.shape
    mlen = mem.shape[1]
    t = b * seq
    e, en, cq, ck = _aug_constants()
    rows = lambda a: a.reshape(a.shape[0], 1, -1)
    n_groups = len(POOL_WINDOWS)

    w_in_t = jnp.swapaxes(w_in, 1, 2)
    bf = rows(jnp.pad(b_forget, ((0, 0), (0, LANES - FOX_HEADS))))
    eye = jnp.eye(n_groups, dtype=pool_w.dtype)
    wpool = jnp.einsum('lgcd,gh->lgchd', pool_w, eye).reshape(DEPTH, POOL_WIDTH, POOL_WIDTH).astype(_BF16)
    sguw = sgu_w.reshape(DEPTH, SGU_GROUPS * SGU_CHUNK, SGU_CHUNK)
    bmat = jnp.repeat(jnp.swapaxes(sgu_b, 1, 2), SGU_WIDTH // SGU_GROUPS, axis=2)
    g_mix, g_mem, g_x, g_ffn = rows(norm_mix_g), rows(norm_mem_g), rows(norm_xattn_g), rows(norm_ffn_g)
    pscale, sgug, bg = rows(pool_scale), rows(sgu_norm_g), rows(b_gate)

    xc = x.reshape(t, d)
    mem2d = mem.reshape(b * mlen, d)
    for l in range(DEPTH):
        xk, xv = _mem_kv(l, mem2d, g_mem, w_xkv)
        pa, q2, k2, v, pc = _mix_in(l, xc, seq, g_mix, w_in_t, bf, wpool, pscale, sgug, sguw, bmat, e, en, cq, ck)
        att = _fox(q2.reshape(b, seq, -1), k2.reshape(b, seq, -1), v.reshape(b, seq, -1))
        xc = _merge_xattn(l, xc, seq, pa, att.reshape(t, -1), pc, g_mix, w_in_t, bg,
                          w_branch_a, w_branch_b, w_branch_c, w_out, g_x, w_xq,
                          xk.reshape(b, mlen, d), xv.reshape(b, mlen, d), w_xo)
        xc = _ffn(l, xc, g_ffn, w_ff1, w_ff2, final_norm_g.reshape(1, d), l == DEPTH - 1)
    return xc.reshape(b, seq, d)
```

```python
import functools
import math

import numpy as np
import jax
import jax.numpy as jnp
from jax import lax
from jax.experimental import pallas as pl
from jax.experimental.pallas import tpu as pltpu

D_MODEL = 1024
DEPTH = 2
POOL_GROUP_DIM = 64
POOL_WIDTH = 256
POOL_WINDOWS = (2, 4, 8, 16)
POOL_HALO = 16
FOX_HEADS = 8
FOX_HEAD_DIM = 64
FOX_WIDTH = 512
SGU_WIDTH = 256
SGU_GROUPS = 4
SGU_CHUNK = 128
XATTN_HEADS = 4
XATTN_HEAD_DIM = 256
D_FF = 4096
EPS = 1e-6
NEG = -1e30
LOG2E = math.log2(math.e)

LANES = 128
AUG_LANES = 8
QNORM_LANE = 6
KNORM_LANE = 7
NORM_CAP = 1e30
HEAD_PAIRS = FOX_HEADS // 2
PAIR_W = 2 * LANES

OFF_A = 0
OFF_Q = OFF_A + POOL_WIDTH
OFF_K = OFF_Q + FOX_WIDTH
OFF_V = OFF_K + FOX_WIDTH
OFF_F = OFF_V + FOX_WIDTH
OFF_C = OFF_F + FOX_HEADS
OFF_G = OFF_C + 2 * SGU_WIDTH
N_IN = OFF_G + 3 * D_MODEL

TM = 512
TQ = 256
VMEM_LIMIT = 56 * 1024 * 1024

_F32 = jnp.float32
_BF16 = jnp.bfloat16


def _dot(a, b):
    return jnp.dot(a, b, preferred_element_type=_F32)


def _dot_t(a, b):
    return lax.dot_general(a, b, (((1,), (1,)), ((), ())), preferred_element_type=_F32)


def _rms(x, g):
    ms = jnp.mean(x * x, axis=-1, keepdims=True)
    return x * lax.rsqrt(ms + EPS) * g


def _rms_factored(x, g):
    ms = jnp.mean(x * x, axis=-1, keepdims=True)
    return (x * g).astype(_BF16), lax.rsqrt(ms + EPS)


def _const_spec(shape):
    nd = len(shape)
    return pl.BlockSpec(shape, lambda *_: (0,) * nd, pipeline_mode=pl.Buffered(1))


def _layer_spec(tail, l):
    nd = len(tail)
    return pl.BlockSpec((None,) + tuple(tail), lambda *_: (l,) + (0,) * nd, pipeline_mode=pl.Buffered(1))


def _params(sem):
    return pltpu.CompilerParams(dimension_semantics=sem, vmem_limit_bytes=VMEM_LIMIT)


def _mem_kv_kernel(mem_ref, g_ref, w_ref, xk_ref, xv_ref):
    hm, r = _rms_factored(mem_ref[...], g_ref[...])
    xk_ref[...] = (r * _dot(hm, w_ref[:, :D_MODEL].astype(_BF16))).astype(_BF16)
    xv_ref[...] = (r * _dot(hm, w_ref[:, D_MODEL:].astype(_BF16))).astype(_BF16)


def _mem_kv(l, mem2d, g, w_xkv):
    rows = mem2d.shape[0]
    return pl.pallas_call(
        _mem_kv_kernel,
        grid=(rows // TM,),
        in_specs=[pl.BlockSpec((TM, D_MODEL), lambda i: (i, 0)),
                  _layer_spec((1, D_MODEL), l),
                  _layer_spec((D_MODEL, 2 * D_MODEL), l)],
        out_specs=[pl.BlockSpec((TM, D_MODEL), lambda i: (i, 0))] * 2,
        out_shape=[jax.ShapeDtypeStruct((rows, D_MODEL), _BF16)] * 2,
        compiler_params=_params(("arbitrary",)),
        name="mem_kv",
    )(mem2d, g, w_xkv)


def _mix_in_kernel(tiles_per_seq,
                   x_ref, g_ref, w_ref, bf_ref, wpool_ref, pscale_ref, sgug_ref, sguw_ref,
                   bmat_ref, e_ref, en_ref, cq_ref, ck_ref,
                   pa_ref, q2_ref, k2_ref, v_ref, pc_ref,
                   halo_ref, fcarry_ref):
    jj = pl.program_id(0) % tiles_per_seq

    @pl.when(jj == 0)
    def _():
        halo_ref[...] = jnp.zeros_like(halo_ref)
        fcarry_ref[...] = jnp.zeros_like(fcarry_ref)

    h, r = _rms_factored(x_ref[...], g_ref[...])

    wf = jnp.concatenate([w_ref[OFF_F:OFF_C, :], jnp.zeros((LANES - FOX_HEADS, D_MODEL), _F32)], axis=0)
    z = r * _dot_t(h, wf.astype(_BF16)) + bf_ref[...]
    logf = jnp.minimum(z, 0.0) - jnp.log1p(jnp.exp(-jnp.abs(z)))
    row = lax.broadcasted_iota(jnp.int32, (TM, LANES), 0)
    c = logf
    shift = 1
    while shift < TM:
        c = c + jnp.where(row >= shift, pltpu.roll(c, shift, 0), 0.0)
        shift *= 2
    fcum = (c + fcarry_ref[...]) * LOG2E
    fcarry_ref[...] = fcarry_ref[...] + jnp.sum(logf, axis=0, keepdims=True)
    hi = fcum.astype(_BF16).astype(_F32)
    r1 = fcum - hi
    mid = r1.astype(_BF16).astype(_F32)
    lo = r1 - mid
    lane = lax.broadcasted_iota(jnp.int32, (TM, LANES), 1)
    packed = jnp.where(lane < FOX_HEADS, hi,
                       jnp.where(lane < 2 * FOX_HEADS, pltpu.roll(mid, FOX_HEADS, 1),
                                 jnp.where(lane < 3 * FOX_HEADS, pltpu.roll(lo, 2 * FOX_HEADS, 1), 0.0)))

    a = r * _dot_t(h, w_ref[OFF_A:OFF_Q, :].astype(_BF16))
    ext = jnp.concatenate([halo_ref[...], a], axis=0)
    halo_ref[...] = a[TM - POOL_HALO:, :]
    w2 = ext + pltpu.roll(ext, 1, 0)
    w4 = w2 + pltpu.roll(w2, 2, 0)
    w8 = w4 + pltpu.roll(w4, 4, 0)
    w16 = w8 + pltpu.roll(w8, 8, 0)
    grp = lax.broadcasted_iota(jnp.int32, (TM, POOL_WIDTH), 1) // POOL_GROUP_DIM
    wsel = jnp.where(grp == 0, w2[POOL_HALO:],
                     jnp.where(grp == 1, w4[POOL_HALO:],
                               jnp.where(grp == 2, w8[POOL_HALO:], w16[POOL_HALO:])))
    pos = jj * TM + lax.broadcasted_iota(jnp.int32, (TM, POOL_WIDTH), 0)
    win = jnp.left_shift(2, grp)
    cnt = jnp.minimum(pos + 1, win).astype(_F32)
    d = (wsel / cnt - a).astype(_BF16)

    q = _dot_t(h, w_ref[OFF_Q:OFF_K, :].astype(_BF16)) * (r * (FOX_HEAD_DIM ** -0.5 * LOG2E))
    k = r * _dot_t(h, w_ref[OFF_K:OFF_V, :].astype(_BF16))
    v_ref[...] = (r * _dot_t(h, w_ref[OFF_V:OFF_F, :].astype(_BF16))).astype(_BF16)
    qb = q.astype(_BF16)
    kb = k.astype(_BF16)
    for p in range(HEAD_PAIRS):
        q2_ref[:, p * PAIR_W:p * PAIR_W + LANES] = qb[:, p * LANES:(p + 1) * LANES]
        k2_ref[:, p * PAIR_W:p * PAIR_W + LANES] = kb[:, p * LANES:(p + 1) * LANES]
    qsq = (q * q).astype(_BF16)
    ksq = (k * k).astype(_BF16)

    cz = r * _dot_t(h, w_ref[OFF_C:OFF_G, :].astype(_BF16))
    pa_ref[...] = (_dot(d, wpool_ref[...]) * pscale_ref[...]).astype(_BF16)
    zc = cz * (0.5 * (1.0 + jnp.tanh(math.sqrt(2.0 / math.pi) * (cz + 0.044715 * (cz * cz * cz)))))
    u = zc[:, :SGU_WIDTH]
    vv = _rms(zc[:, SGU_WIDTH:], sgug_ref[...])
    wrow = lax.broadcasted_iota(jnp.int32, (SGU_GROUPS * SGU_CHUNK, SGU_CHUNK), 0) % SGU_CHUNK
    wcol = lax.broadcasted_iota(jnp.int32, (SGU_GROUPS * SGU_CHUNK, SGU_CHUNK), 1)
    wst = jnp.where(wcol <= wrow, sguw_ref[...], 0.0).astype(_BF16)
    cgrp = lax.broadcasted_iota(jnp.int32, (SGU_CHUNK, SGU_WIDTH), 1) // (SGU_WIDTH // SGU_GROUPS)
    for ci in range(TM // SGU_CHUNK):
        rs = slice(ci * SGU_CHUNK, (ci + 1) * SGU_CHUNK)
        r = _dot(wst, vv[rs].astype(_BF16))
        mixed = jnp.where(cgrp == 0, r[0:SGU_CHUNK],
                          jnp.where(cgrp == 1, r[SGU_CHUNK:2 * SGU_CHUNK],
                                    jnp.where(cgrp == 2, r[2 * SGU_CHUNK:3 * SGU_CHUNK],
                                              r[3 * SGU_CHUNK:])))
        pc_ref[rs, :] = (u[rs] * (mixed + bmat_ref[...])).astype(_BF16)

    aug = _dot(packed.astype(_BF16), e_ref[...])
    nq = jnp.minimum(_dot(qsq, en_ref[:, :LANES]), NORM_CAP)
    nk = jnp.minimum(_dot(ksq, en_ref[:, LANES:]), NORM_CAP)
    aq = (aug[:, :LANES] + cq_ref[...] + nq).astype(_BF16)
    ak = (aug[:, LANES:] + ck_ref[...] + nk).astype(_BF16)
    for p in range(HEAD_PAIRS):
        q2_ref[:, p * PAIR_W + LANES:(p + 1) * PAIR_W] = aq
        k2_ref[:, p * PAIR_W + LANES:(p + 1) * PAIR_W] = ak


def _mix_in(l, x2d, seq, g, w_in_t, bf, wpool, pscale, sgug, sguw, bmat, e, en, cq, ck):
    t = x2d.shape[0]
    row_spec = lambda w: pl.BlockSpec((TM, w), lambda i: (i, 0))
    return pl.pallas_call(
        functools.partial(_mix_in_kernel, seq // TM),
        grid=(t // TM,),
        in_specs=[row_spec(D_MODEL),
                  _layer_spec((1, D_MODEL), l),
                  _layer_spec((OFF_G, D_MODEL), l),
                  _layer_spec((1, LANES), l),
                  _layer_spec((POOL_WIDTH, POOL_WIDTH), l),
                  _layer_spec((1, POOL_WIDTH), l),
                  _layer_spec((1, SGU_WIDTH), l),
                  _layer_spec((SGU_GROUPS * SGU_CHUNK, SGU_CHUNK), l),
                  _layer_spec((SGU_CHUNK, SGU_WIDTH), l),
                  _const_spec((LANES, 2 * LANES)),
                  _const_spec((FOX_WIDTH, 2 * LANES)),
                  _const_spec((1, LANES)),
                  _const_spec((1, LANES))],
        out_specs=[row_spec(POOL_WIDTH), row_spec(HEAD_PAIRS * PAIR_W), row_spec(HEAD_PAIRS * PAIR_W),
                   row_spec(FOX_WIDTH), row_spec(SGU_WIDTH)],
        out_shape=[jax.ShapeDtypeStruct((t, POOL_WIDTH), _BF16),
                   jax.ShapeDtypeStruct((t, HEAD_PAIRS * PAIR_W), _BF16),
                   jax.ShapeDtypeStruct((t, HEAD_PAIRS * PAIR_W), _BF16),
                   jax.ShapeDtypeStruct((t, FOX_WIDTH), _BF16),
                   jax.ShapeDtypeStruct((t, SGU_WIDTH), _BF16)],
        scratch_shapes=[pltpu.VMEM((POOL_HALO, POOL_WIDTH), _F32),
                        pltpu.VMEM((1, LANES), _F32)],
        compiler_params=_params(("arbitrary",)),
        name="mix_in",
    )(x2d, g, w_in_t, bf, wpool, pscale, sgug, sguw, bmat, e, en, cq, ck)


FOX_LOGIT_BOUND = 64.0
NORM_SLACK = 1.1
COLMAX_SPLIT = 8


def _aug_colmax(ref):
    rows = ref.shape[0] // COLMAX_SPLIT
    parts = [ref[r * rows:(r + 1) * rows, LANES:] for r in range(COLMAX_SPLIT)]
    while len(parts) > 1:
        parts = [jnp.maximum(a, b) for a, b in zip(parts[::2], parts[1::2])]
    return jnp.max(parts[0], axis=0, keepdims=True).astype(_F32)


def _pair_logit_bound2(q2_ref, k2_ref, pair):
    qmax, kmax = _aug_colmax(q2_ref), _aug_colmax(k2_ref)
    prod = qmax * pltpu.roll(kmax, LANES - (KNORM_LANE - QNORM_LANE), 1)
    lane = lax.broadcasted_iota(jnp.int32, prod.shape, 1)
    first = 2 * pair * AUG_LANES + QNORM_LANE
    return jnp.max(jnp.where((lane == first) | (lane == first + AUG_LANES), prod, 0.0))


def _fox_kernel(seq, q2_ref, k2_ref, v_ref, o_ref, s_ref, p_ref, vext_ref):
    pair = pl.program_id(1)
    lane = lax.broadcasted_iota(jnp.int32, (TQ, PAIR_W), 1)
    keep = []
    for hh in range(2):
        aug_lo = LANES + (2 * pair + hh) * AUG_LANES
        keep.append(((lane >= hh * FOX_HEAD_DIM) & (lane < (hh + 1) * FOX_HEAD_DIM))
                    | ((lane >= aug_lo) & (lane < aug_lo + AUG_LANES)))
    drow = lax.broadcasted_iota(jnp.int32, (2 * TQ, TQ), 0) % TQ
    dcol = lax.broadcasted_iota(jnp.int32, (2 * TQ, TQ), 1)
    causal = dcol <= drow
    out_lane = lax.broadcasted_iota(jnp.int32, (TQ, LANES), 1)

    def stacked_q(i):
        qb = q2_ref[i * TQ:(i + 1) * TQ, :]
        zero = jnp.zeros_like(qb)
        return jnp.concatenate([jnp.where(keep[0], qb, zero), jnp.where(keep[1], qb, zero)], axis=0)

    def store_heads(i, o):
        o_ref[i * TQ:(i + 1) * TQ, :] = jnp.where(out_lane < FOX_HEAD_DIM, o[:TQ], o[TQ:]).astype(_BF16)

    bounded = _pair_logit_bound2(q2_ref, k2_ref, pair) <= FOX_LOGIT_BOUND * FOX_LOGIT_BOUND / NORM_SLACK

    @pl.when(bounded)
    def _():
        vext_ref[:, :LANES] = v_ref[...]
        vext_ref[:, LANES:] = jnp.ones((seq, LANES), _BF16)

        def probabilities(i):
            qm = stacked_q(i)
            for c in range(i + 1):
                sc = _dot_t(qm, k2_ref[c * TQ:(c + 1) * TQ, :])
                if c == i:
                    sc = jnp.where(causal, sc, NEG)
                p_ref[i % 2, :, c * TQ:(c + 1) * TQ] = jnp.exp2(sc).astype(_BF16)

        n_blocks = seq // TQ
        probabilities(0)
        for i in range(n_blocks):
            if i + 1 < n_blocks:
                probabilities(i + 1)
            kend = (i + 1) * TQ
            o = _dot(p_ref[i % 2, :, :kend], vext_ref[:kend, :])
            store_heads(i, o[:, :LANES] * (1.0 / o[:, LANES:]))

    @pl.when(jnp.logical_not(bounded))
    def _():
        for i in range(seq // TQ):
            kend = (i + 1) * TQ
            s_ref[:, :kend] = _dot_t(stacked_q(i), k2_ref[:kend, :])

            mrun = jnp.full((2 * TQ, LANES), NEG, _F32)
            for c in range(i + 1):
                sc = s_ref[:, c * TQ:(c + 1) * TQ]
                if c == i:
                    sc = jnp.where(causal, sc, NEG)
                    s_ref[:, c * TQ:(c + 1) * TQ] = sc
                mrun = jnp.maximum(mrun, jnp.maximum(sc[:, :LANES], sc[:, LANES:]))
            mb = jnp.broadcast_to(jnp.max(mrun, axis=1, keepdims=True), (2 * TQ, LANES))

            lrun = jnp.zeros((2 * TQ, LANES), _F32)
            for c in range(i + 1):
                p0 = jnp.exp2(s_ref[:, c * TQ:c * TQ + LANES] - mb)
                p1 = jnp.exp2(s_ref[:, c * TQ + LANES:(c + 1) * TQ] - mb)
                lrun = lrun + (p0 + p1)
                p_ref[0, :, c * TQ:c * TQ + LANES] = p0.astype(_BF16)
                p_ref[0, :, c * TQ + LANES:(c + 1) * TQ] = p1.astype(_BF16)
            linv = 1.0 / jnp.sum(lrun, axis=1, keepdims=True)
            store_heads(i, _dot(p_ref[0, :, :kend], v_ref[:kend, :]) * linv)


def _fox(q2, k2, v):
    b, seq, _ = q2.shape
    return pl.pallas_call(
        functools.partial(_fox_kernel, seq),
        grid=(b, HEAD_PAIRS),
        in_specs=[pl.BlockSpec((None, seq, PAIR_W), lambda bi, p: (bi, 0, p)),
                  pl.BlockSpec((None, seq, PAIR_W), lambda bi, p: (bi, 0, p)),
                  pl.BlockSpec((None, seq, LANES), lambda bi, p: (bi, 0, p))],
        out_specs=pl.BlockSpec((None, seq, LANES), lambda bi, p: (bi, 0, p)),
        out_shape=jax.ShapeDtypeStruct((b, seq, FOX_WIDTH), _BF16),
        scratch_shapes=[pltpu.VMEM((2 * TQ, seq), _F32),
                        pltpu.VMEM((2, 2 * TQ, seq), _BF16),
                        pltpu.VMEM((seq, 2 * LANES), _BF16)],
        compiler_params=_params(("arbitrary", "arbitrary")),
        name="fox_attention",
    )(q2, k2, v)


MERGE_CHUNK = 256


def _merge_xattn_kernel(x_ref, pa_ref, att_ref, pc_ref, gmix_ref, wg_ref, bg_ref,
                        wba_ref, wbb_ref, wbc_ref, wout_ref, gx_ref, wxq_ref,
                        xk_ref, xv_ref, wxo_ref, o_ref, merged_ref):
    x = x_ref[...]
    h, r = _rms_factored(x, gmix_ref[...])
    branches = ((pa_ref, wba_ref), (att_ref, wbb_ref), (pc_ref, wbc_ref))
    for n in range(D_MODEL // MERGE_CHUNK):
        cols = slice(n * MERGE_CHUNK, (n + 1) * MERGE_CHUNK)
        acc = None
        for j, (b_ref, wb_ref) in enumerate(branches):
            gcols = slice(j * D_MODEL + n * MERGE_CHUNK, j * D_MODEL + (n + 1) * MERGE_CHUNK)
            zg = r * _dot_t(h, wg_ref[gcols, :].astype(_BF16)) + bg_ref[:, gcols]
            gate = 0.5 * jnp.tanh(0.5 * zg) + 0.5
            term = gate * _dot(b_ref[...], wb_ref[:, cols].astype(_BF16))
            acc = term if acc is None else acc + term
        merged_ref[:, cols] = acc.astype(_BF16)
    x1 = x + _dot(merged_ref[...], wout_ref[...].astype(_BF16))

    hx, r1 = _rms_factored(x1, gx_ref[...])
    qscale = r1 * (XATTN_HEAD_DIM ** -0.5 * LOG2E)
    head_cols = [slice(hd * XATTN_HEAD_DIM, (hd + 1) * XATTN_HEAD_DIM) for hd in range(XATTN_HEADS)]
    xq, s, o = {}, {}, {}
    y = x1
    for t in range(XATTN_HEADS + 3):
        if t < XATTN_HEADS:
            xq[t] = (_dot(hx, wxq_ref[:, head_cols[t]].astype(_BF16)) * qscale).astype(_BF16)
        if 0 <= t - 1 < XATTN_HEADS:
            s[t - 1] = _dot_t(xq[t - 1], xk_ref[:, head_cols[t - 1]])
        if 0 <= t - 2 < XATTN_HEADS:
            p = jnp.exp2(s[t - 2] - jnp.max(s[t - 2], axis=1, keepdims=True))
            linv = 1.0 / jnp.sum(p, axis=1, keepdims=True)
            o[t - 2] = (_dot(p.astype(_BF16), xv_ref[:, head_cols[t - 2]]) * linv).astype(_BF16)
        if 0 <= t - 3 < XATTN_HEADS:
            y = y + _dot(o[t - 3], wxo_ref[head_cols[t - 3], :].astype(_BF16))
    o_ref[...] = y


def _merge_xattn(l, x2d, seq, pa, att, pc, gmix, w_in_t, b_gate, wba, wbb, wbc, w_out, gx, w_xq, xk, xv, w_xo):
    t = x2d.shape[0]
    mlen = xk.shape[1]
    tiles_per_seq = seq // TM
    row_spec = lambda w: pl.BlockSpec((TM, w), lambda i: (i, 0))
    mem_spec = pl.BlockSpec((None, mlen, D_MODEL), lambda i: (i // tiles_per_seq, 0, 0))
    return pl.pallas_call(
        _merge_xattn_kernel,
        grid=(t // TM,),
        in_specs=[row_spec(D_MODEL), row_spec(POOL_WIDTH), row_spec(FOX_WIDTH), row_spec(SGU_WIDTH),
                  _layer_spec((1, D_MODEL), l),
                  pl.BlockSpec((None, pl.Element(3 * D_MODEL), pl.Element(D_MODEL)),
                               lambda i: (l, OFF_G, 0), pipeline_mode=pl.Buffered(1)),
                  _layer_spec((1, 3 * D_MODEL), l),
                  _layer_spec((POOL_WIDTH, D_MODEL), l),
                  _layer_spec((FOX_WIDTH, D_MODEL), l),
                  _layer_spec((SGU_WIDTH, D_MODEL), l),
                  _layer_spec((D_MODEL, D_MODEL), l),
                  _layer_spec((1, D_MODEL), l),
                  _layer_spec((D_MODEL, D_MODEL), l),
                  mem_spec, mem_spec,
                  _layer_spec((D_MODEL, D_MODEL), l)],
        out_specs=row_spec(D_MODEL),
        out_shape=jax.ShapeDtypeStruct((t, D_MODEL), _F32),
        scratch_shapes=[pltpu.VMEM((TM, D_MODEL), _BF16)],
        compiler_params=_params(("arbitrary",)),
        name="merge_xattn",
    )(x2d, pa, att, pc, gmix, w_in_t, b_gate, wba, wbb, wbc, w_out, gx, w_xq, xk, xv, w_xo)


FF_CHUNK = 1024


def _ffn_kernel(final_norm, x_ref, g_ref, w1_ref, w2_ref, gfin_ref, o_ref, act_ref):
    x = x_ref[...]
    hf, r = _rms_factored(x, g_ref[...])
    for c in range(D_FF // FF_CHUNK):
        cols = slice(c * FF_CHUNK, (c + 1) * FF_CHUNK)
        u = jnp.maximum(_dot(hf, w1_ref[:, cols].astype(_BF16)), 0.0)
        act_ref[:, cols] = (u * u).astype(_BF16)
    y = x + (r * r) * _dot(act_ref[...], w2_ref[...].astype(_BF16))
    if final_norm:
        y = _rms(y, gfin_ref[...])
    o_ref[...] = y


def _ffn(l, x2d, g, w1, w2, gfin, final_norm):
    t = x2d.shape[0]
    return pl.pallas_call(
        functools.partial(_ffn_kernel, final_norm),
        grid=(t // TM,),
        in_specs=[pl.BlockSpec((TM, D_MODEL), lambda i: (i, 0)),
                  _layer_spec((1, D_MODEL), l),
                  _layer_spec((D_MODEL, D_FF), l),
                  _layer_spec((D_FF, D_MODEL), l),
                  _const_spec((1, D_MODEL))],
        out_specs=pl.BlockSpec((TM, D_MODEL), lambda i: (i, 0)),
        out_shape=jax.ShapeDtypeStruct((t, D_MODEL), _F32),
        scratch_shapes=[pltpu.VMEM((TM, D_FF), _BF16)],
        compiler_params=_params(("arbitrary",)),
        name="ffn",
    )(x2d, g, w1, w2, gfin)


def _aug_constants():
    e = np.zeros((LANES, 2 * LANES), np.float32)
    en = np.zeros((FOX_WIDTH, 2 * LANES), np.float32)
    cq = np.zeros((1, LANES), np.float32)
    ck = np.zeros((1, LANES), np.float32)
    for h in range(FOX_HEADS):
        for piece in range(3):
            e[piece * FOX_HEADS + h, h * AUG_LANES + piece] = 1.0
            e[piece * FOX_HEADS + h, LANES + h * AUG_LANES + 3 + piece] = -1.0
            cq[0, h * AUG_LANES + 3 + piece] = 1.0
            ck[0, h * AUG_LANES + piece] = 1.0
        cols = slice(h * FOX_HEAD_DIM, (h + 1) * FOX_HEAD_DIM)
        en[cols, h * AUG_LANES + QNORM_LANE] = 1.0
        en[cols, LANES + h * AUG_LANES + KNORM_LANE] = 1.0
    return jnp.asarray(e, _BF16), jnp.asarray(en, _BF16), jnp.asarray(cq), jnp.asarray(ck)


def kernel(x, mem, norm_mix_g, w_in, b_forget, pool_w, pool_scale, sgu_norm_g, sgu_w, sgu_b,
           w_branch_a, w_branch_b, w_branch_c, b_gate, w_out, norm_xattn_g, norm_mem_g,
           w_xq, w_xkv, w_xo, norm_ffn_g, w_ff1, w_ff2, final_norm_g):
    b, seq, d = x.shape
    mlen = mem.shape[1]
    t = b * seq
    e, en, cq, ck = _aug_constants()
    rows = lambda a: a.reshape(a.shape[0], 1, -1)
    n_groups = len(POOL_WINDOWS)

    w_in_t = jnp.swapaxes(w_in, 1, 2)
    bf = rows(jnp.pad(b_forget, ((0, 0), (0, LANES - FOX_HEADS))))
    eye = jnp.eye(n_groups, dtype=pool_w.dtype)
    wpool = jnp.einsum('lgcd,gh->lgchd', pool_w, eye).reshape(DEPTH, POOL_WIDTH, POOL_WIDTH).astype(_BF16)
    sguw = sgu_w.reshape(DEPTH, SGU_GROUPS * SGU_CHUNK, SGU_CHUNK)
    bmat = jnp.repeat(jnp.swapaxes(sgu_b, 1, 2), SGU_WIDTH // SGU_GROUPS, axis=2)
    g_mix, g_mem, g_x, g_ffn = rows(norm_mix_g), rows(norm_mem_g), rows(norm_xattn_g), rows(norm_ffn_g)
    pscale, sgug, bg = rows(pool_scale), rows(sgu_norm_g), rows(b_gate)

    xc = x.reshape(t, d)
    mem2d = mem.reshape(b * mlen, d)
    for l in range(DEPTH):
        xk, xv = _mem_kv(l, mem2d, g_mem, w_xkv)
        pa, q2, k2, v, pc = _mix_in(l, xc, seq, g_mix, w_in_t, bf, wpool, pscale, sgug, sguw, bmat, e, en, cq, ck)
        att = _fox(q2.reshape(b, seq, -1), k2.reshape(b, seq, -1), v.reshape(b, seq, -1))
        xc = _merge_xattn(l, xc, seq, pa, att.reshape(t, -1), pc, g_mix, w_in_t, bg,
                          w_branch_a, w_branch_b, w_branch_c, w_out, g_x, w_xq,
                          xk.reshape(b, mlen, d), xv.reshape(b, mlen, d), w_xo)
        xc = _ffn(l, xc, g_ffn, w_ff1, w_ff2, final_norm_g.reshape(1, d), l == DEPTH - 1)
    return xc.reshape(b, seq, d)
```

```python
import functools
import math

import numpy as np
import jax
import jax.numpy as jnp
from jax import lax
from jax.experimental import pallas as pl
from jax.experimental.pallas import tpu as pltpu

D_MODEL = 1024
DEPTH = 2
POOL_GROUP_DIM = 64
POOL_WIDTH = 256
POOL_WINDOWS = (2, 4, 8, 16)
POOL_HALO = 16
FOX_HEADS = 8
FOX_HEAD_DIM = 64
FOX_WIDTH = 512
SGU_WIDTH = 256
SGU_GROUPS = 4
SGU_CHUNK = 128
XATTN_HEADS = 4
XATTN_HEAD_DIM = 256
D_FF = 4096
EPS = 1e-6
NEG = -1e30
LOG2E = math.log2(math.e)

LANES = 128
AUG_LANES = 8
QNORM_LANE = 6
KNORM_LANE = 7
NORM_CAP = 1e30
HEAD_PAIRS = FOX_HEADS // 2
PAIR_W = 2 * LANES

OFF_A = 0
OFF_Q = OFF_A + POOL_WIDTH
OFF_K = OFF_Q + FOX_WIDTH
OFF_V = OFF_K + FOX_WIDTH
OFF_F = OFF_V + FOX_WIDTH
OFF_C = OFF_F + FOX_HEADS
OFF_G = OFF_C + 2 * SGU_WIDTH
N_IN = OFF_G + 3 * D_MODEL

TM = 512
TQ = 256
VMEM_LIMIT = 56 * 1024 * 1024

_F32 = jnp.float32
_BF16 = jnp.bfloat16


def _dot(a, b):
    return jnp.dot(a, b, preferred_element_type=_F32)


def _dot_t(a, b):
    return lax.dot_general(a, b, (((1,), (1,)), ((), ())), preferred_element_type=_F32)


def _rms(x, g):
    ms = jnp.mean(x * x, axis=-1, keepdims=True)
    return x * lax.rsqrt(ms + EPS) * g


def _rms_factored(x, g):
    ms = jnp.mean(x * x, axis=-1, keepdims=True)
    return (x * g).astype(_BF16), lax.rsqrt(ms + EPS)


def _const_spec(shape):
    nd = len(shape)
    return pl.BlockSpec(shape, lambda *_: (0,) * nd, pipeline_mode=pl.Buffered(1))


def _layer_spec(tail, l):
    nd = len(tail)
    return pl.BlockSpec((None,) + tuple(tail), lambda *_: (l,) + (0,) * nd, pipeline_mode=pl.Buffered(1))


def _params(sem):
    return pltpu.CompilerParams(dimension_semantics=sem, vmem_limit_bytes=VMEM_LIMIT)


def _mem_kv_kernel(mem_ref, g_ref, w_ref, xk_ref, xv_ref):
    hm, r = _rms_factored(mem_ref[...], g_ref[...])
    xk_ref[...] = (r * _dot(hm, w_ref[:, :D_MODEL].astype(_BF16))).astype(_BF16)
    xv_ref[...] = (r * _dot(hm, w_ref[:, D_MODEL:].astype(_BF16))).astype(_BF16)


def _mem_kv(l, mem2d, g, w_xkv):
    rows = mem2d.shape[0]
    return pl.pallas_call(
        _mem_kv_kernel,
        grid=(rows // TM,),
        in_specs=[pl.BlockSpec((TM, D_MODEL), lambda i: (i, 0)),
                  _layer_spec((1, D_MODEL), l),
                  _layer_spec((D_MODEL, 2 * D_MODEL), l)],
        out_specs=[pl.BlockSpec((TM, D_MODEL), lambda i: (i, 0))] * 2,
        out_shape=[jax.ShapeDtypeStruct((rows, D_MODEL), _BF16)] * 2,
        compiler_params=_params(("arbitrary",)),
        name="mem_kv",
    )(mem2d, g, w_xkv)


def _mix_in_kernel(tiles_per_seq,
                   x_ref, g_ref, w_ref, bf_ref, wpool_ref, pscale_ref, sgug_ref, sguw_ref,
                   bmat_ref, e_ref, en_ref, cq_ref, ck_ref,
                   pa_ref, q2_ref, k2_ref, v_ref, pc_ref,
                   halo_ref, fcarry_ref):
    jj = pl.program_id(0) % tiles_per_seq

    @pl.when(jj == 0)
    def _():
        halo_ref[...] = jnp.zeros_like(halo_ref)
        fcarry_ref[...] = jnp.zeros_like(fcarry_ref)

    h = _rms(x_ref[...], g_ref[...]).astype(_BF16)

    wf = jnp.concatenate([w_ref[OFF_F:OFF_C, :], jnp.zeros((LANES - FOX_HEADS, D_MODEL), _F32)], axis=0)
    z = _dot_t(h, wf.astype(_BF16)) + bf_ref[...]
    logf = jnp.minimum(z, 0.0) - jnp.log1p(jnp.exp(-jnp.abs(z)))
    row = lax.broadcasted_iota(jnp.int32, (TM, LANES), 0)
    c = logf
    shift = 1
    while shift < TM:
        c = c + jnp.where(row >= shift, pltpu.roll(c, shift, 0), 0.0)
        shift *= 2
    fcum = (c + fcarry_ref[...]) * LOG2E
    fcarry_ref[...] = fcarry_ref[...] + jnp.sum(logf, axis=0, keepdims=True)
    hi = fcum.astype(_BF16).astype(_F32)
    r1 = fcum - hi
    mid = r1.astype(_BF16).astype(_F32)
    lo = r1 - mid
    lane = lax.broadcasted_iota(jnp.int32, (TM, LANES), 1)
    packed = jnp.where(lane < FOX_HEADS, hi,
                       jnp.where(lane < 2 * FOX_HEADS, pltpu.roll(mid, FOX_HEADS, 1),
                                 jnp.where(lane < 3 * FOX_HEADS, pltpu.roll(lo, 2 * FOX_HEADS, 1), 0.0)))

    a = _dot_t(h, w_ref[OFF_A:OFF_Q, :].astype(_BF16))
    cz = _dot_t(h, w_ref[OFF_C:OFF_G, :].astype(_BF16))
    ext = jnp.concatenate([halo_ref[...], a], axis=0)
    halo_ref[...] = a[TM - POOL_HALO:, :]
    w2 = ext + pltpu.roll(ext, 1, 0)
    w4 = w2 + pltpu.roll(w2, 2, 0)
    w8 = w4 + pltpu.roll(w4, 4, 0)
    w16 = w8 + pltpu.roll(w8, 8, 0)
    grp = lax.broadcasted_iota(jnp.int32, (TM, POOL_WIDTH), 1) // POOL_GROUP_DIM
    wsel = jnp.where(grp == 0, w2[POOL_HALO:],
                     jnp.where(grp == 1, w4[POOL_HALO:],
                               jnp.where(grp == 2, w8[POOL_HALO:], w16[POOL_HALO:])))
    pos = jj * TM + lax.broadcasted_iota(jnp.int32, (TM, POOL_WIDTH), 0)
    win = jnp.left_shift(2, grp)
    cnt = jnp.minimum(pos + 1, win).astype(_F32)
    d = (wsel / cnt - a).astype(_BF16)

    q = _dot_t(h, w_ref[OFF_Q:OFF_K, :].astype(_BF16)) * (FOX_HEAD_DIM ** -0.5 * LOG2E)
    k = _dot_t(h, w_ref[OFF_K:OFF_V, :].astype(_BF16))
    v_ref[...] = _dot_t(h, w_ref[OFF_V:OFF_F, :].astype(_BF16)).astype(_BF16)
    qb = q.astype(_BF16)
    kb = k.astype(_BF16)
    for p in range(HEAD_PAIRS):
        q2_ref[:, p * PAIR_W:p * PAIR_W + LANES] = qb[:, p * LANES:(p + 1) * LANES]
        k2_ref[:, p * PAIR_W:p * PAIR_W + LANES] = kb[:, p * LANES:(p + 1) * LANES]
    qsq = (q * q).astype(_BF16)
    ksq = (k * k).astype(_BF16)

    pa_ref[...] = (_dot(d, wpool_ref[...]) * pscale_ref[...]).astype(_BF16)
    zc = cz * (0.5 * (1.0 + jnp.tanh(math.sqrt(2.0 / math.pi) * (cz + 0.044715 * (cz * cz * cz)))))
    u = zc[:, :SGU_WIDTH]
    vv = _rms(zc[:, SGU_WIDTH:], sgug_ref[...])
    wrow = lax.broadcasted_iota(jnp.int32, (SGU_GROUPS * SGU_CHUNK, SGU_CHUNK), 0) % SGU_CHUNK
    wcol = lax.broadcasted_iota(jnp.int32, (SGU_GROUPS * SGU_CHUNK, SGU_CHUNK), 1)
    wst = jnp.where(wcol <= wrow, sguw_ref[...], 0.0).astype(_BF16)
    cgrp = lax.broadcasted_iota(jnp.int32, (SGU_CHUNK, SGU_WIDTH), 1) // (SGU_WIDTH // SGU_GROUPS)
    for ci in range(TM // SGU_CHUNK):
        rs = slice(ci * SGU_CHUNK, (ci + 1) * SGU_CHUNK)
        r = _dot(wst, vv[rs].astype(_BF16))
        mixed = jnp.where(cgrp == 0, r[0:SGU_CHUNK],
                          jnp.where(cgrp == 1, r[SGU_CHUNK:2 * SGU_CHUNK],
                                    jnp.where(cgrp == 2, r[2 * SGU_CHUNK:3 * SGU_CHUNK],
                                              r[3 * SGU_CHUNK:])))
        pc_ref[rs, :] = (u[rs] * (mixed + bmat_ref[...])).astype(_BF16)

    aug = _dot(packed.astype(_BF16), e_ref[...])
    nq = jnp.minimum(_dot(qsq, en_ref[:, :LANES]), NORM_CAP)
    nk = jnp.minimum(_dot(ksq, en_ref[:, LANES:]), NORM_CAP)
    aq = (aug[:, :LANES] + cq_ref[...] + nq).astype(_BF16)
    ak = (aug[:, LANES:] + ck_ref[...] + nk).astype(_BF16)
    for p in range(HEAD_PAIRS):
        q2_ref[:, p * PAIR_W + LANES:(p + 1) * PAIR_W] = aq
        k2_ref[:, p * PAIR_W + LANES:(p + 1) * PAIR_W] = ak


def _mix_in(l, x2d, seq, g, w_in_t, bf, wpool, pscale, sgug, sguw, bmat, e, en, cq, ck):
    t = x2d.shape[0]
    row_spec = lambda w: pl.BlockSpec((TM, w), lambda i: (i, 0))
    return pl.pallas_call(
        functools.partial(_mix_in_kernel, seq // TM),
        grid=(t // TM,),
        in_specs=[row_spec(D_MODEL),
                  _layer_spec((1, D_MODEL), l),
                  _layer_spec((OFF_G, D_MODEL), l),
                  _layer_spec((1, LANES), l),
                  _layer_spec((POOL_WIDTH, POOL_WIDTH), l),
                  _layer_spec((1, POOL_WIDTH), l),
                  _layer_spec((1, SGU_WIDTH), l),
                  _layer_spec((SGU_GROUPS * SGU_CHUNK, SGU_CHUNK), l),
                  _layer_spec((SGU_CHUNK, SGU_WIDTH), l),
                  _const_spec((LANES, 2 * LANES)),
                  _const_spec((FOX_WIDTH, 2 * LANES)),
                  _const_spec((1, LANES)),
                  _const_spec((1, LANES))],
        out_specs=[row_spec(POOL_WIDTH), row_spec(HEAD_PAIRS * PAIR_W), row_spec(HEAD_PAIRS * PAIR_W),
                   row_spec(FOX_WIDTH), row_spec(SGU_WIDTH)],
        out_shape=[jax.ShapeDtypeStruct((t, POOL_WIDTH), _BF16),
                   jax.ShapeDtypeStruct((t, HEAD_PAIRS * PAIR_W), _BF16),
                   jax.ShapeDtypeStruct((t, HEAD_PAIRS * PAIR_W), _BF16),
                   jax.ShapeDtypeStruct((t, FOX_WIDTH), _BF16),
                   jax.ShapeDtypeStruct((t, SGU_WIDTH), _BF16)],
        scratch_shapes=[pltpu.VMEM((POOL_HALO, POOL_WIDTH), _F32),
                        pltpu.VMEM((1, LANES), _F32)],
        compiler_params=_params(("arbitrary",)),
        name="mix_in",
    )(x2d, g, w_in_t, bf, wpool, pscale, sgug, sguw, bmat, e, en, cq, ck)


FOX_LOGIT_BOUND = 64.0
NORM_SLACK = 1.1
COLMAX_SPLIT = 8


def _aug_colmax(ref):
    rows = ref.shape[0] // COLMAX_SPLIT
    parts = [ref[r * rows:(r + 1) * rows, LANES:] for r in range(COLMAX_SPLIT)]
    while len(parts) > 1:
        parts = [jnp.maximum(a, b) for a, b in zip(parts[::2], parts[1::2])]
    return jnp.max(parts[0], axis=0, keepdims=True).astype(_F32)


def _pair_logit_bound2(q2_ref, k2_ref, pair):
    qmax, kmax = _aug_colmax(q2_ref), _aug_colmax(k2_ref)
    prod = qmax * pltpu.roll(kmax, LANES - (KNORM_LANE - QNORM_LANE), 1)
    lane = lax.broadcasted_iota(jnp.int32, prod.shape, 1)
    first = 2 * pair * AUG_LANES + QNORM_LANE
    return jnp.max(jnp.where((lane == first) | (lane == first + AUG_LANES), prod, 0.0))


def _fox_kernel(seq, q2_ref, k2_ref, v_ref, o_ref, s_ref, p_ref, vext_ref):
    pair = pl.program_id(1)
    lane = lax.broadcasted_iota(jnp.int32, (TQ, PAIR_W), 1)
    keep = []
    for hh in range(2):
        aug_lo = LANES + (2 * pair + hh) * AUG_LANES
        keep.append(((lane >= hh * FOX_HEAD_DIM) & (lane < (hh + 1) * FOX_HEAD_DIM))
                    | ((lane >= aug_lo) & (lane < aug_lo + AUG_LANES)))
    drow = lax.broadcasted_iota(jnp.int32, (2 * TQ, TQ), 0) % TQ
    dcol = lax.broadcasted_iota(jnp.int32, (2 * TQ, TQ), 1)
    causal = dcol <= drow
    out_lane = lax.broadcasted_iota(jnp.int32, (TQ, LANES), 1)

    def stacked_q(i):
        qb = q2_ref[i * TQ:(i + 1) * TQ, :]
        zero = jnp.zeros_like(qb)
        return jnp.concatenate([jnp.where(keep[0], qb, zero), jnp.where(keep[1], qb, zero)], axis=0)

    def store_heads(i, o):
        o_ref[i * TQ:(i + 1) * TQ, :] = jnp.where(out_lane < FOX_HEAD_DIM, o[:TQ], o[TQ:]).astype(_BF16)

    bounded = _pair_logit_bound2(q2_ref, k2_ref, pair) <= FOX_LOGIT_BOUND * FOX_LOGIT_BOUND / NORM_SLACK

    @pl.when(bounded)
    def _():
        vext_ref[:, :LANES] = v_ref[...]
        vext_ref[:, LANES:] = jnp.ones((seq, LANES), _BF16)

        def probabilities(i):
            qm = stacked_q(i)
            for c in range(i + 1):
                sc = _dot_t(qm, k2_ref[c * TQ:(c + 1) * TQ, :])
                if c == i:
                    sc = jnp.where(causal, sc, NEG)
                p_ref[i % 2, :, c * TQ:(c + 1) * TQ] = jnp.exp2(sc).astype(_BF16)

        n_blocks = seq // TQ
        probabilities(0)
        for i in range(n_blocks):
            if i + 1 < n_blocks:
                probabilities(i + 1)
            kend = (i + 1) * TQ
            o = _dot(p_ref[i % 2, :, :kend], vext_ref[:kend, :])
            store_heads(i, o[:, :LANES] * (1.0 / o[:, LANES:]))

    @pl.when(jnp.logical_not(bounded))
    def _():
        for i in range(seq // TQ):
            kend = (i + 1) * TQ
            s_ref[:, :kend] = _dot_t(stacked_q(i), k2_ref[:kend, :])

            mrun = jnp.full((2 * TQ, LANES), NEG, _F32)
            for c in range(i + 1):
                sc = s_ref[:, c * TQ:(c + 1) * TQ]
                if c == i:
                    sc = jnp.where(causal, sc, NEG)
                    s_ref[:, c * TQ:(c + 1) * TQ] = sc
                mrun = jnp.maximum(mrun, jnp.maximum(sc[:, :LANES], sc[:, LANES:]))
            mb = jnp.broadcast_to(jnp.max(mrun, axis=1, keepdims=True), (2 * TQ, LANES))

            lrun = jnp.zeros((2 * TQ, LANES), _F32)
            for c in range(i + 1):
                p0 = jnp.exp2(s_ref[:, c * TQ:c * TQ + LANES] - mb)
                p1 = jnp.exp2(s_ref[:, c * TQ + LANES:(c + 1) * TQ] - mb)
                lrun = lrun + (p0 + p1)
                p_ref[0, :, c * TQ:c * TQ + LANES] = p0.astype(_BF16)
                p_ref[0, :, c * TQ + LANES:(c + 1) * TQ] = p1.astype(_BF16)
            linv = 1.0 / jnp.sum(lrun, axis=1, keepdims=True)
            store_heads(i, _dot(p_ref[0, :, :kend], v_ref[:kend, :]) * linv)


def _fox(q2, k2, v):
    b, seq, _ = q2.shape
    return pl.pallas_call(
        functools.partial(_fox_kernel, seq),
        grid=(b, HEAD_PAIRS),
        in_specs=[pl.BlockSpec((None, seq, PAIR_W), lambda bi, p: (bi, 0, p)),
                  pl.BlockSpec((None, seq, PAIR_W), lambda bi, p: (bi, 0, p)),
                  pl.BlockSpec((None, seq, LANES), lambda bi, p: (bi, 0, p))],
        out_specs=pl.BlockSpec((None, seq, LANES), lambda bi, p: (bi, 0, p)),
        out_shape=jax.ShapeDtypeStruct((b, seq, FOX_WIDTH), _BF16),
        scratch_shapes=[pltpu.VMEM((2 * TQ, seq), _F32),
                        pltpu.VMEM((2, 2 * TQ, seq), _BF16),
                        pltpu.VMEM((seq, 2 * LANES), _BF16)],
        compiler_params=_params(("arbitrary", "arbitrary")),
        name="fox_attention",
    )(q2, k2, v)


MERGE_CHUNK = 256


def _merge_xattn_kernel(x_ref, pa_ref, att_ref, pc_ref, gmix_ref, wg_ref, bg_ref,
                        wba_ref, wbb_ref, wbc_ref, wout_ref, gx_ref, wxq_ref,
                        xk_ref, xv_ref, wxo_ref, o_ref, merged_ref):
    x = x_ref[...]
    h, r = _rms_factored(x, gmix_ref[...])
    branches = ((pa_ref, wba_ref), (att_ref, wbb_ref), (pc_ref, wbc_ref))
    for n in range(D_MODEL // MERGE_CHUNK):
        cols = slice(n * MERGE_CHUNK, (n + 1) * MERGE_CHUNK)
        acc = None
        for j, (b_ref, wb_ref) in enumerate(branches):
            gcols = slice(j * D_MODEL + n * MERGE_CHUNK, j * D_MODEL + (n + 1) * MERGE_CHUNK)
            zg = r * _dot_t(h, wg_ref[gcols, :].astype(_BF16)) + bg_ref[:, gcols]
            gate = 0.5 * jnp.tanh(0.5 * zg) + 0.5
            term = gate * _dot(b_ref[...], wb_ref[:, cols].astype(_BF16))
            acc = term if acc is None else acc + term
        merged_ref[:, cols] = acc.astype(_BF16)
    x1 = x + _dot(merged_ref[...], wout_ref[...].astype(_BF16))

    hx, r1 = _rms_factored(x1, gx_ref[...])
    qscale = r1 * (XATTN_HEAD_DIM ** -0.5 * LOG2E)
    head_cols = [slice(hd * XATTN_HEAD_DIM, (hd + 1) * XATTN_HEAD_DIM) for hd in range(XATTN_HEADS)]
    xq, s, o = {}, {}, {}
    y = x1
    for t in range(XATTN_HEADS + 3):
        if t < XATTN_HEADS:
            xq[t] = (_dot(hx, wxq_ref[:, head_cols[t]].astype(_BF16)) * qscale).astype(_BF16)
        if 0 <= t - 1 < XATTN_HEADS:
            s[t - 1] = _dot_t(xq[t - 1], xk_ref[:, head_cols[t - 1]])
        if 0 <= t - 2 < XATTN_HEADS:
            p = jnp.exp2(s[t - 2] - jnp.max(s[t - 2], axis=1, keepdims=True))
            linv = 1.0 / jnp.sum(p, axis=1, keepdims=True)
            o[t - 2] = (_dot(p.astype(_BF16), xv_ref[:, head_cols[t - 2]]) * linv).astype(_BF16)
        if 0 <= t - 3 < XATTN_HEADS:
            y = y + _dot(o[t - 3], wxo_ref[head_cols[t - 3], :].astype(_BF16))
    o_ref[...] = y


def _merge_xattn(l, x2d, seq, pa, att, pc, gmix, w_in_t, b_gate, wba, wbb, wbc, w_out, gx, w_xq, xk, xv, w_xo):
    t = x2d.shape[0]
    mlen = xk.shape[1]
    tiles_per_seq = seq // TM
    row_spec = lambda w: pl.BlockSpec((TM, w), lambda i: (i, 0))
    mem_spec = pl.BlockSpec((None, mlen, D_MODEL), lambda i: (i // tiles_per_seq, 0, 0))
    return pl.pallas_call(
        _merge_xattn_kernel,
        grid=(t // TM,),
        in_specs=[row_spec(D_MODEL), row_spec(POOL_WIDTH), row_spec(FOX_WIDTH), row_spec(SGU_WIDTH),
                  _layer_spec((1, D_MODEL), l),
                  pl.BlockSpec((None, pl.Element(3 * D_MODEL), pl.Element(D_MODEL)),
                               lambda i: (l, OFF_G, 0), pipeline_mode=pl.Buffered(1)),
                  _layer_spec((1, 3 * D_MODEL), l),
                  _layer_spec((POOL_WIDTH, D_MODEL), l),
                  _layer_spec((FOX_WIDTH, D_MODEL), l),
                  _layer_spec((SGU_WIDTH, D_MODEL), l),
                  _layer_spec((D_MODEL, D_MODEL), l),
                  _layer_spec((1, D_MODEL), l),
                  _layer_spec((D_MODEL, D_MODEL), l),
                  mem_spec, mem_spec,
                  _layer_spec((D_MODEL, D_MODEL), l)],
        out_specs=row_spec(D_MODEL),
        out_shape=jax.ShapeDtypeStruct((t, D_MODEL), _F32),
        scratch_shapes=[pltpu.VMEM((TM, D_MODEL), _BF16)],
        compiler_params=_params(("arbitrary",)),
        name="merge_xattn",
    )(x2d, pa, att, pc, gmix, w_in_t, b_gate, wba, wbb, wbc, w_out, gx, w_xq, xk, xv, w_xo)


FF_CHUNK = 1024


def _ffn_kernel(final_norm, x_ref, g_ref, w1_ref, w2_ref, gfin_ref, o_ref, act_ref):
    x = x_ref[...]
    hf, r = _rms_factored(x, g_ref[...])
    for c in range(D_FF // FF_CHUNK):
        cols = slice(c * FF_CHUNK, (c + 1) * FF_CHUNK)
        u = jnp.maximum(_dot(hf, w1_ref[:, cols].astype(_BF16)), 0.0)
        act_ref[:, cols] = (u * u).astype(_BF16)
    y = x + (r * r) * _dot(act_ref[...], w2_ref[...].astype(_BF16))
    if final_norm:
        y = _rms(y, gfin_ref[...])
    o_ref[...] = y


def _ffn(l, x2d, g, w1, w2, gfin, final_norm):
    t = x2d.shape[0]
    return pl.pallas_call(
        functools.partial(_ffn_kernel, final_norm),
        grid=(t // TM,),
        in_specs=[pl.BlockSpec((TM, D_MODEL), lambda i: (i, 0)),
                  _layer_spec((1, D_MODEL), l),
                  _layer_spec((D_MODEL, D_FF), l),
                  _layer_spec((D_FF, D_MODEL), l),
                  _const_spec((1, D_MODEL))],
        out_specs=pl.BlockSpec((TM, D_MODEL), lambda i: (i, 0)),
        out_shape=jax.ShapeDtypeStruct((t, D_MODEL), _F32),
        scratch_shapes=[pltpu.VMEM((TM, D_FF), _BF16)],
        compiler_params=_params(("arbitrary",)),
        name="ffn",
    )(x2d, g, w1, w2, gfin)


def _aug_constants():
    e = np.zeros((LANES, 2 * LANES), np.float32)
    en = np.zeros((FOX_WIDTH, 2 * LANES), np.float32)
    cq = np.zeros((1, LANES), np.float32)
    ck = np.zeros((1, LANES), np.float32)
    for h in range(FOX_HEADS):
        for piece in range(3):
            e[piece * FOX_HEADS + h, h * AUG_LANES + piece] = 1.0
            e[piece * FOX_HEADS + h, LANES + h * AUG_LANES + 3 + piece] = -1.0
            cq[0, h * AUG_LANES + 3 + piece] = 1.0
            ck[0, h * AUG_LANES + piece] = 1.0
        cols = slice(h * FOX_HEAD_DIM, (h + 1) * FOX_HEAD_DIM)
        en[cols, h * AUG_LANES + QNORM_LANE] = 1.0
        en[cols, LANES + h * AUG_LANES + KNORM_LANE] = 1.0
    return jnp.asarray(e, _BF16), jnp.asarray(en, _BF16), jnp.asarray(cq), jnp.asarray(ck)


def kernel(x, mem, norm_mix_g, w_in, b_forget, pool_w, pool_scale, sgu_norm_g, sgu_w, sgu_b,
           w_branch_a, w_branch_b, w_branch_c, b_gate, w_out, norm_xattn_g, norm_mem_g,
           w_xq, w_xkv, w_xo, norm_ffn_g, w_ff1, w_ff2, final_norm_g):
    b, seq, d = x.shape
    mlen = mem.shape[1]
    t = b * seq
    e, en, cq, ck = _aug_constants()
    rows = lambda a: a.reshape(a.shape[0], 1, -1)
    n_groups = len(POOL_WINDOWS)

    w_in_t = jnp.swapaxes(w_in, 1, 2)
    bf = rows(jnp.pad(b_forget, ((0, 0), (0, LANES - FOX_HEADS))))
    eye = jnp.eye(n_groups, dtype=pool_w.dtype)
    wpool = jnp.einsum('lgcd,gh->lgchd', pool_w, eye).reshape(DEPTH, POOL_WIDTH, POOL_WIDTH).astype(_BF16)
    sguw = sgu_w.reshape(DEPTH, SGU_GROUPS * SGU_CHUNK, SGU_CHUNK)
    bmat = jnp.repeat(jnp.swapaxes(sgu_b, 1, 2), SGU_WIDTH // SGU_GROUPS, axis=2)
    g_mix, g_mem, g_x, g_ffn = rows(norm_mix_g), rows(norm_mem_g), rows(norm_xattn_g), rows(norm_ffn_g)
    pscale, sgug, bg = rows(pool_scale), rows(sgu_norm_g), rows(b_gate)

    xc = x.reshape(t, d)
    mem2d = mem.reshape(b * mlen, d)
    for l in range(DEPTH):
        xk, xv = _mem_kv(l, mem2d, g_mem, w_xkv)
        pa, q2, k2, v, pc = _mix_in(l, xc, seq, g_mix, w_in_t, bf, wpool, pscale, sgug, sguw, bmat, e, en, cq, ck)
        att = _fox(q2.reshape(b, seq, -1), k2.reshape(b, seq, -1), v.reshape(b, seq, -1))
        xc = _merge_xattn(l, xc, seq, pa, att.reshape(t, -1), pc, g_mix, w_in_t, bg,
                          w_branch_a, w_branch_b, w_branch_c, w_out, g_x, w_xq,
                          xk.reshape(b, mlen, d), xv.reshape(b, mlen, d), w_xo)
        xc = _ffn(l, xc, g_ffn, w_ff1, w_ff2, final_norm_g.reshape(1, d), l == DEPTH - 1)
    return xc.reshape(b, seq, d)
```

```python
import functools
import math

import numpy as np
import jax
import jax.numpy as jnp
from jax import lax
from jax.experimental import pallas as pl
from jax.experimental.pallas import tpu as pltpu

D_MODEL = 1024
DEPTH = 2
POOL_GROUP_DIM = 64
POOL_WIDTH = 256
POOL_WINDOWS = (2, 4, 8, 16)
POOL_HALO = 16
FOX_HEADS = 8
FOX_HEAD_DIM = 64
FOX_WIDTH = 512
SGU_WIDTH = 256
SGU_GROUPS = 4
SGU_CHUNK = 128
XATTN_HEADS = 4
XATTN_HEAD_DIM = 256
D_FF = 4096
EPS = 1e-6
NEG = -1e30
LOG2E = math.log2(math.e)

LANES = 128
AUG_LANES = 8
QNORM_LANE = 6
KNORM_LANE = 7
NORM_CAP = 1e30
HEAD_PAIRS = FOX_HEADS // 2
PAIR_W = 2 * LANES

OFF_A = 0
OFF_Q = OFF_A + POOL_WIDTH
OFF_K = OFF_Q + FOX_WIDTH
OFF_V = OFF_K + FOX_WIDTH
OFF_F = OFF_V + FOX_WIDTH
OFF_C = OFF_F + FOX_HEADS
OFF_G = OFF_C + 2 * SGU_WIDTH
N_IN = OFF_G + 3 * D_MODEL

TM = 512
TM_MIX = 1024
TQ = 256
VMEM_LIMIT = 56 * 1024 * 1024

_F32 = jnp.float32
_BF16 = jnp.bfloat16


def _dot(a, b):
    return jnp.dot(a, b, preferred_element_type=_F32)


def _dot_t(a, b):
    return lax.dot_general(a, b, (((1,), (1,)), ((), ())), preferred_element_type=_F32)


def _rms(x, g):
    ms = jnp.mean(x * x, axis=-1, keepdims=True)
    return x * lax.rsqrt(ms + EPS) * g


def _rms_factored(x, g):
    ms = jnp.mean(x * x, axis=-1, keepdims=True)
    return (x * g).astype(_BF16), lax.rsqrt(ms + EPS)


def _const_spec(shape):
    nd = len(shape)
    return pl.BlockSpec(shape, lambda *_: (0,) * nd, pipeline_mode=pl.Buffered(1))


def _layer_spec(tail, l):
    nd = len(tail)
    return pl.BlockSpec((None,) + tuple(tail), lambda *_: (l,) + (0,) * nd, pipeline_mode=pl.Buffered(1))


def _params(sem):
    return pltpu.CompilerParams(dimension_semantics=sem, vmem_limit_bytes=VMEM_LIMIT)


def _mem_kv_kernel(mem_ref, g_ref, w_ref, xk_ref, xv_ref):
    hm, r = _rms_factored(mem_ref[...], g_ref[...])
    xk_ref[...] = (r * _dot(hm, w_ref[:, :D_MODEL].astype(_BF16))).astype(_BF16)
    xv_ref[...] = (r * _dot(hm, w_ref[:, D_MODEL:].astype(_BF16))).astype(_BF16)


def _mem_kv(l, mem2d, g, w_xkv):
    rows = mem2d.shape[0]
    return pl.pallas_call(
        _mem_kv_kernel,
        grid=(rows // TM,),
        in_specs=[pl.BlockSpec((TM, D_MODEL), lambda i: (i, 0)),
                  _layer_spec((1, D_MODEL), l),
                  _layer_spec((D_MODEL, 2 * D_MODEL), l)],
        out_specs=[pl.BlockSpec((TM, D_MODEL), lambda i: (i, 0))] * 2,
        out_shape=[jax.ShapeDtypeStruct((rows, D_MODEL), _BF16)] * 2,
        compiler_params=_params(("arbitrary",)),
        name="mem_kv",
    )(mem2d, g, w_xkv)


def _mix_in_kernel(tm, tiles_per_seq,
                   x_ref, g_ref, w_ref, bf_ref, wpool_ref, pscale_ref, sgug_ref, sguw_ref,
                   bmat_ref, e_ref, en_ref, cq_ref, ck_ref,
                   pa_ref, q2_ref, k2_ref, v_ref, pc_ref,
                   halo_ref, fcarry_ref):
    jj = pl.program_id(0) % tiles_per_seq

    @pl.when(jj == 0)
    def _():
        halo_ref[...] = jnp.zeros_like(halo_ref)
        fcarry_ref[...] = jnp.zeros_like(fcarry_ref)

    h = _rms(x_ref[...], g_ref[...]).astype(_BF16)

    wf = jnp.concatenate([w_ref[OFF_F:OFF_C, :], jnp.zeros((LANES - FOX_HEADS, D_MODEL), _F32)], axis=0)
    z = _dot_t(h, wf.astype(_BF16)) + bf_ref[...]
    logf = jnp.minimum(z, 0.0) - jnp.log1p(jnp.exp(-jnp.abs(z)))
    row = lax.broadcasted_iota(jnp.int32, (tm, LANES), 0)
    c = logf
    shift = 1
    while shift < tm:
        c = c + jnp.where(row >= shift, pltpu.roll(c, shift, 0), 0.0)
        shift *= 2
    fcum = (c + fcarry_ref[...]) * LOG2E
    fcarry_ref[...] = fcarry_ref[...] + jnp.sum(logf, axis=0, keepdims=True)
    hi = fcum.astype(_BF16).astype(_F32)
    r1 = fcum - hi
    mid = r1.astype(_BF16).astype(_F32)
    lo = r1 - mid
    lane = lax.broadcasted_iota(jnp.int32, (tm, LANES), 1)
    packed = jnp.where(lane < FOX_HEADS, hi,
                       jnp.where(lane < 2 * FOX_HEADS, pltpu.roll(mid, FOX_HEADS, 1),
                                 jnp.where(lane < 3 * FOX_HEADS, pltpu.roll(lo, 2 * FOX_HEADS, 1), 0.0)))

    a = _dot_t(h, w_ref[OFF_A:OFF_Q, :].astype(_BF16))
    cz = _dot_t(h, w_ref[OFF_C:OFF_G, :].astype(_BF16))
    ext = jnp.concatenate([halo_ref[...], a], axis=0)
    halo_ref[...] = a[tm - POOL_HALO:, :]
    w2 = ext + pltpu.roll(ext, 1, 0)
    w4 = w2 + pltpu.roll(w2, 2, 0)
    w8 = w4 + pltpu.roll(w4, 4, 0)
    w16 = w8 + pltpu.roll(w8, 8, 0)
    grp = lax.broadcasted_iota(jnp.int32, (tm, POOL_WIDTH), 1) // POOL_GROUP_DIM
    wsel = jnp.where(grp == 0, w2[POOL_HALO:],
                     jnp.where(grp == 1, w4[POOL_HALO:],
                               jnp.where(grp == 2, w8[POOL_HALO:], w16[POOL_HALO:])))
    pos = jj * tm + lax.broadcasted_iota(jnp.int32, (tm, POOL_WIDTH), 0)
    win = jnp.left_shift(2, grp)
    cnt = jnp.minimum(pos + 1, win).astype(_F32)
    d = (wsel / cnt - a).astype(_BF16)

    q = _dot_t(h, w_ref[OFF_Q:OFF_K, :].astype(_BF16)) * (FOX_HEAD_DIM ** -0.5 * LOG2E)
    k = _dot_t(h, w_ref[OFF_K:OFF_V, :].astype(_BF16))
    v_ref[...] = _dot_t(h, w_ref[OFF_V:OFF_F, :].astype(_BF16)).astype(_BF16)
    qb = q.astype(_BF16)
    kb = k.astype(_BF16)
    for p in range(HEAD_PAIRS):
        q2_ref[:, p * PAIR_W:p * PAIR_W + LANES] = qb[:, p * LANES:(p + 1) * LANES]
        k2_ref[:, p * PAIR_W:p * PAIR_W + LANES] = kb[:, p * LANES:(p + 1) * LANES]
    qsq = (q * q).astype(_BF16)
    ksq = (k * k).astype(_BF16)

    pa_ref[...] = (_dot(d, wpool_ref[...]) * pscale_ref[...]).astype(_BF16)
    zc = cz * (0.5 * (1.0 + jnp.tanh(math.sqrt(2.0 / math.pi) * (cz + 0.044715 * (cz * cz * cz)))))
    u = zc[:, :SGU_WIDTH]
    vv = _rms(zc[:, SGU_WIDTH:], sgug_ref[...])
    wrow = lax.broadcasted_iota(jnp.int32, (SGU_GROUPS * SGU_CHUNK, SGU_CHUNK), 0) % SGU_CHUNK
    wcol = lax.broadcasted_iota(jnp.int32, (SGU_GROUPS * SGU_CHUNK, SGU_CHUNK), 1)
    wst = jnp.where(wcol <= wrow, sguw_ref[...], 0.0).astype(_BF16)
    cgrp = lax.broadcasted_iota(jnp.int32, (SGU_CHUNK, SGU_WIDTH), 1) // (SGU_WIDTH // SGU_GROUPS)
    for ci in range(tm // SGU_CHUNK):
        rs = slice(ci * SGU_CHUNK, (ci + 1) * SGU_CHUNK)
        r = _dot(wst, vv[rs].astype(_BF16))
        mixed = jnp.where(cgrp == 0, r[0:SGU_CHUNK],
                          jnp.where(cgrp == 1, r[SGU_CHUNK:2 * SGU_CHUNK],
                                    jnp.where(cgrp == 2, r[2 * SGU_CHUNK:3 * SGU_CHUNK],
                                              r[3 * SGU_CHUNK:])))
        pc_ref[rs, :] = (u[rs] * (mixed + bmat_ref[...])).astype(_BF16)

    aug = _dot(packed.astype(_BF16), e_ref[...])
    nq = jnp.minimum(_dot(qsq, en_ref[:, :LANES]), NORM_CAP)
    nk = jnp.minimum(_dot(ksq, en_ref[:, LANES:]), NORM_CAP)
    aq = (aug[:, :LANES] + cq_ref[...] + nq).astype(_BF16)
    ak = (aug[:, LANES:] + ck_ref[...] + nk).astype(_BF16)
    for p in range(HEAD_PAIRS):
        q2_ref[:, p * PAIR_W + LANES:(p + 1) * PAIR_W] = aq
        k2_ref[:, p * PAIR_W + LANES:(p + 1) * PAIR_W] = ak


def _mix_in(l, x2d, seq, g, w_in_t, bf, wpool, pscale, sgug, sguw, bmat, e, en, cq, ck):
    t = x2d.shape[0]
    tm = TM_MIX
    row_spec = lambda w: pl.BlockSpec((tm, w), lambda i: (i, 0))
    return pl.pallas_call(
        functools.partial(_mix_in_kernel, tm, seq // tm),
        grid=(t // tm,),
        in_specs=[row_spec(D_MODEL),
                  _layer_spec((1, D_MODEL), l),
                  _layer_spec((OFF_G, D_MODEL), l),
                  _layer_spec((1, LANES), l),
                  _layer_spec((POOL_WIDTH, POOL_WIDTH), l),
                  _layer_spec((1, POOL_WIDTH), l),
                  _layer_spec((1, SGU_WIDTH), l),
                  _layer_spec((SGU_GROUPS * SGU_CHUNK, SGU_CHUNK), l),
                  _layer_spec((SGU_CHUNK, SGU_WIDTH), l),
                  _const_spec((LANES, 2 * LANES)),
                  _const_spec((FOX_WIDTH, 2 * LANES)),
                  _const_spec((1, LANES)),
                  _const_spec((1, LANES))],
        out_specs=[row_spec(POOL_WIDTH), row_spec(HEAD_PAIRS * PAIR_W), row_spec(HEAD_PAIRS * PAIR_W),
                   row_spec(FOX_WIDTH), row_spec(SGU_WIDTH)],
        out_shape=[jax.ShapeDtypeStruct((t, POOL_WIDTH), _BF16),
                   jax.ShapeDtypeStruct((t, HEAD_PAIRS * PAIR_W), _BF16),
                   jax.ShapeDtypeStruct((t, HEAD_PAIRS * PAIR_W), _BF16),
                   jax.ShapeDtypeStruct((t, FOX_WIDTH), _BF16),
                   jax.ShapeDtypeStruct((t, SGU_WIDTH), _BF16)],
        scratch_shapes=[pltpu.VMEM((POOL_HALO, POOL_WIDTH), _F32),
                        pltpu.VMEM((1, LANES), _F32)],
        compiler_params=_params(("arbitrary",)),
        name="mix_in",
    )(x2d, g, w_in_t, bf, wpool, pscale, sgug, sguw, bmat, e, en, cq, ck)


FOX_LOGIT_BOUND = 64.0
NORM_SLACK = 1.1
COLMAX_SPLIT = 8


def _aug_colmax(ref):
    rows = ref.shape[0] // COLMAX_SPLIT
    parts = [ref[r * rows:(r + 1) * rows, LANES:] for r in range(COLMAX_SPLIT)]
    while len(parts) > 1:
        parts = [jnp.maximum(a, b) for a, b in zip(parts[::2], parts[1::2])]
    return jnp.max(parts[0], axis=0, keepdims=True).astype(_F32)


def _pair_logit_bound2(q2_ref, k2_ref, pair):
    qmax, kmax = _aug_colmax(q2_ref), _aug_colmax(k2_ref)
    prod = qmax * pltpu.roll(kmax, LANES - (KNORM_LANE - QNORM_LANE), 1)
    lane = lax.broadcasted_iota(jnp.int32, prod.shape, 1)
    first = 2 * pair * AUG_LANES + QNORM_LANE
    return jnp.max(jnp.where((lane == first) | (lane == first + AUG_LANES), prod, 0.0))


def _fox_kernel(seq, q2_ref, k2_ref, v_ref, o_ref, s_ref, p_ref, vext_ref):
    pair = pl.program_id(1)
    lane = lax.broadcasted_iota(jnp.int32, (TQ, PAIR_W), 1)
    keep = []
    for hh in range(2):
        aug_lo = LANES + (2 * pair + hh) * AUG_LANES
        keep.append(((lane >= hh * FOX_HEAD_DIM) & (lane < (hh + 1) * FOX_HEAD_DIM))
                    | ((lane >= aug_lo) & (lane < aug_lo + AUG_LANES)))
    drow = lax.broadcasted_iota(jnp.int32, (2 * TQ, TQ), 0) % TQ
    dcol = lax.broadcasted_iota(jnp.int32, (2 * TQ, TQ), 1)
    causal = dcol <= drow
    out_lane = lax.broadcasted_iota(jnp.int32, (TQ, LANES), 1)

    def stacked_q(i):
        qb = q2_ref[i * TQ:(i + 1) * TQ, :]
        zero = jnp.zeros_like(qb)
        return jnp.concatenate([jnp.where(keep[0], qb, zero), jnp.where(keep[1], qb, zero)], axis=0)

    def store_heads(i, o):
        o_ref[i * TQ:(i + 1) * TQ, :] = jnp.where(out_lane < FOX_HEAD_DIM, o[:TQ], o[TQ:]).astype(_BF16)

    bounded = _pair_logit_bound2(q2_ref, k2_ref, pair) <= FOX_LOGIT_BOUND * FOX_LOGIT_BOUND / NORM_SLACK

    @pl.when(bounded)
    def _():
        vext_ref[:, :LANES] = v_ref[...]
        vext_ref[:, LANES:] = jnp.ones((seq, LANES), _BF16)

        def probabilities(i):
            qm = stacked_q(i)
            for c in range(i + 1):
                sc = _dot_t(qm, k2_ref[c * TQ:(c + 1) * TQ, :])
                if c == i:
                    sc = jnp.where(causal, sc, NEG)
                p_ref[i % 2, :, c * TQ:(c + 1) * TQ] = jnp.exp2(sc).astype(_BF16)

        n_blocks = seq // TQ
        probabilities(0)
        for i in range(n_blocks):
            if i + 1 < n_blocks:
                probabilities(i + 1)
            kend = (i + 1) * TQ
            o = _dot(p_ref[i % 2, :, :kend], vext_ref[:kend, :])
            store_heads(i, o[:, :LANES] * (1.0 / o[:, LANES:]))

    @pl.when(jnp.logical_not(bounded))
    def _():
        for i in range(seq // TQ):
            kend = (i + 1) * TQ
            s_ref[:, :kend] = _dot_t(stacked_q(i), k2_ref[:kend, :])

            mrun = jnp.full((2 * TQ, LANES), NEG, _F32)
            for c in range(i + 1):
                sc = s_ref[:, c * TQ:(c + 1) * TQ]
                if c == i:
                    sc = jnp.where(causal, sc, NEG)
                    s_ref[:, c * TQ:(c + 1) * TQ] = sc
                mrun = jnp.maximum(mrun, jnp.maximum(sc[:, :LANES], sc[:, LANES:]))
            mb = jnp.broadcast_to(jnp.max(mrun, axis=1, keepdims=True), (2 * TQ, LANES))

            lrun = jnp.zeros((2 * TQ, LANES), _F32)
            for c in range(i + 1):
                p0 = jnp.exp2(s_ref[:, c * TQ:c * TQ + LANES] - mb)
                p1 = jnp.exp2(s_ref[:, c * TQ + LANES:(c + 1) * TQ] - mb)
                lrun = lrun + (p0 + p1)
                p_ref[0, :, c * TQ:c * TQ + LANES] = p0.astype(_BF16)
                p_ref[0, :, c * TQ + LANES:(c + 1) * TQ] = p1.astype(_BF16)
            linv = 1.0 / jnp.sum(lrun, axis=1, keepdims=True)
            store_heads(i, _dot(p_ref[0, :, :kend], v_ref[:kend, :]) * linv)


def _fox(q2, k2, v):
    b, seq, _ = q2.shape
    return pl.pallas_call(
        functools.partial(_fox_kernel, seq),
        grid=(b, HEAD_PAIRS),
        in_specs=[pl.BlockSpec((None, seq, PAIR_W), lambda bi, p: (bi, 0, p)),
                  pl.BlockSpec((None, seq, PAIR_W), lambda bi, p: (bi, 0, p)),
                  pl.BlockSpec((None, seq, LANES), lambda bi, p: (bi, 0, p))],
        out_specs=pl.BlockSpec((None, seq, LANES), lambda bi, p: (bi, 0, p)),
        out_shape=jax.ShapeDtypeStruct((b, seq, FOX_WIDTH), _BF16),
        scratch_shapes=[pltpu.VMEM((2 * TQ, seq), _F32),
                        pltpu.VMEM((2, 2 * TQ, seq), _BF16),
                        pltpu.VMEM((seq, 2 * LANES), _BF16)],
        compiler_params=_params(("arbitrary", "arbitrary")),
        name="fox_attention",
    )(q2, k2, v)


MERGE_CHUNK = 256


def _merge_xattn_kernel(x_ref, pa_ref, att_ref, pc_ref, gmix_ref, wg_ref, bg_ref,
                        wba_ref, wbb_ref, wbc_ref, wout_ref, gx_ref, wxq_ref,
                        xk_ref, xv_ref, wxo_ref, o_ref, merged_ref):
    x = x_ref[...]
    h, r = _rms_factored(x, gmix_ref[...])
    branches = ((pa_ref, wba_ref), (att_ref, wbb_ref), (pc_ref, wbc_ref))
    for n in range(D_MODEL // MERGE_CHUNK):
        cols = slice(n * MERGE_CHUNK, (n + 1) * MERGE_CHUNK)
        acc = None
        for j, (b_ref, wb_ref) in enumerate(branches):
            gcols = slice(j * D_MODEL + n * MERGE_CHUNK, j * D_MODEL + (n + 1) * MERGE_CHUNK)
            zg = r * _dot_t(h, wg_ref[gcols, :].astype(_BF16)) + bg_ref[:, gcols]
            gate = 0.5 * jnp.tanh(0.5 * zg) + 0.5
            term = gate * _dot(b_ref[...], wb_ref[:, cols].astype(_BF16))
            acc = term if acc is None else acc + term
        merged_ref[:, cols] = acc.astype(_BF16)
    x1 = x + _dot(merged_ref[...], wout_ref[...].astype(_BF16))

    hx, r1 = _rms_factored(x1, gx_ref[...])
    qscale = r1 * (XATTN_HEAD_DIM ** -0.5 * LOG2E)
    head_cols = [slice(hd * XATTN_HEAD_DIM, (hd + 1) * XATTN_HEAD_DIM) for hd in range(XATTN_HEADS)]
    xq, s, o = {}, {}, {}
    y = x1
    for t in range(XATTN_HEADS + 3):
        if t < XATTN_HEADS:
            xq[t] = (_dot(hx, wxq_ref[:, head_cols[t]].astype(_BF16)) * qscale).astype(_BF16)
        if 0 <= t - 1 < XATTN_HEADS:
            s[t - 1] = _dot_t(xq[t - 1], xk_ref[:, head_cols[t - 1]])
        if 0 <= t - 2 < XATTN_HEADS:
            p = jnp.exp2(s[t - 2] - jnp.max(s[t - 2], axis=1, keepdims=True))
            linv = 1.0 / jnp.sum(p, axis=1, keepdims=True)
            o[t - 2] = (_dot(p.astype(_BF16), xv_ref[:, head_cols[t - 2]]) * linv).astype(_BF16)
        if 0 <= t - 3 < XATTN_HEADS:
            y = y + _dot(o[t - 3], wxo_ref[head_cols[t - 3], :].astype(_BF16))
    o_ref[...] = y


def _merge_xattn(l, x2d, seq, pa, att, pc, gmix, w_in_t, b_gate, wba, wbb, wbc, w_out, gx, w_xq, xk, xv, w_xo):
    t = x2d.shape[0]
    mlen = xk.shape[1]
    tiles_per_seq = seq // TM
    row_spec = lambda w: pl.BlockSpec((TM, w), lambda i: (i, 0))
    mem_spec = pl.BlockSpec((None, mlen, D_MODEL), lambda i: (i // tiles_per_seq, 0, 0))
    return pl.pallas_call(
        _merge_xattn_kernel,
        grid=(t // TM,),
        in_specs=[row_spec(D_MODEL), row_spec(POOL_WIDTH), row_spec(FOX_WIDTH), row_spec(SGU_WIDTH),
                  _layer_spec((1, D_MODEL), l),
                  pl.BlockSpec((None, pl.Element(3 * D_MODEL), pl.Element(D_MODEL)),
                               lambda i: (l, OFF_G, 0), pipeline_mode=pl.Buffered(1)),
                  _layer_spec((1, 3 * D_MODEL), l),
                  _layer_spec((POOL_WIDTH, D_MODEL), l),
                  _layer_spec((FOX_WIDTH, D_MODEL), l),
                  _layer_spec((SGU_WIDTH, D_MODEL), l),
                  _layer_spec((D_MODEL, D_MODEL), l),
                  _layer_spec((1, D_MODEL), l),
                  _layer_spec((D_MODEL, D_MODEL), l),
                  mem_spec, mem_spec,
                  _layer_spec((D_MODEL, D_MODEL), l)],
        out_specs=row_spec(D_MODEL),
        out_shape=jax.ShapeDtypeStruct((t, D_MODEL), _F32),
        scratch_shapes=[pltpu.VMEM((TM, D_MODEL), _BF16)],
        compiler_params=_params(("arbitrary",)),
        name="merge_xattn",
    )(x2d, pa, att, pc, gmix, w_in_t, b_gate, wba, wbb, wbc, w_out, gx, w_xq, xk, xv, w_xo)


FF_CHUNK = 1024


def _ffn_kernel(final_norm, x_ref, g_ref, w1_ref, w2_ref, gfin_ref, o_ref, act_ref):
    x = x_ref[...]
    hf, r = _rms_factored(x, g_ref[...])
    for c in range(D_FF // FF_CHUNK):
        cols = slice(c * FF_CHUNK, (c + 1) * FF_CHUNK)
        u = jnp.maximum(_dot(hf, w1_ref[:, cols].astype(_BF16)), 0.0)
        act_ref[:, cols] = (u * u).astype(_BF16)
    y = x + (r * r) * _dot(act_ref[...], w2_ref[...].astype(_BF16))
    if final_norm:
        y = _rms(y, gfin_ref[...])
    o_ref[...] = y


def _ffn(l, x2d, g, w1, w2, gfin, final_norm):
    t = x2d.shape[0]
    return pl.pallas_call(
        functools.partial(_ffn_kernel, final_norm),
        grid=(t // TM,),
        in_specs=[pl.BlockSpec((TM, D_MODEL), lambda i: (i, 0)),
                  _layer_spec((1, D_MODEL), l),
                  _layer_spec((D_MODEL, D_FF), l),
                  _layer_spec((D_FF, D_MODEL), l),
                  _const_spec((1, D_MODEL))],
        out_specs=pl.BlockSpec((TM, D_MODEL), lambda i: (i, 0)),
        out_shape=jax.ShapeDtypeStruct((t, D_MODEL), _F32),
        scratch_shapes=[pltpu.VMEM((TM, D_FF), _BF16)],
        compiler_params=_params(("arbitrary",)),
        name="ffn",
    )(x2d, g, w1, w2, gfin)


def _aug_constants():
    e = np.zeros((LANES, 2 * LANES), np.float32)
    en = np.zeros((FOX_WIDTH, 2 * LANES), np.float32)
    cq = np.zeros((1, LANES), np.float32)
    ck = np.zeros((1, LANES), np.float32)
    for h in range(FOX_HEADS):
        for piece in range(3):
            e[piece * FOX_HEADS + h, h * AUG_LANES + piece] = 1.0
            e[piece * FOX_HEADS + h, LANES + h * AUG_LANES + 3 + piece] = -1.0
            cq[0, h * AUG_LANES + 3 + piece] = 1.0
            ck[0, h * AUG_LANES + piece] = 1.0
        cols = slice(h * FOX_HEAD_DIM, (h + 1) * FOX_HEAD_DIM)
        en[cols, h * AUG_LANES + QNORM_LANE] = 1.0
        en[cols, LANES + h * AUG_LANES + KNORM_LANE] = 1.0
    return jnp.asarray(e, _BF16), jnp.asarray(en, _BF16), jnp.asarray(cq), jnp.asarray(ck)


def kernel(x, mem, norm_mix_g, w_in, b_forget, pool_w, pool_scale, sgu_norm_g, sgu_w, sgu_b,
           w_branch_a, w_branch_b, w_branch_c, b_gate, w_out, norm_xattn_g, norm_mem_g,
           w_xq, w_xkv, w_xo, norm_ffn_g, w_ff1, w_ff2, final_norm_g):
    b, seq, d = x.shape
    mlen = mem.shape[1]
    t = b * seq
    e, en, cq, ck = _aug_constants()
    rows = lambda a: a.reshape(a.shape[0], 1, -1)
    n_groups = len(POOL_WINDOWS)

    w_in_t = jnp.swapaxes(w_in, 1, 2)
    bf = rows(jnp.pad(b_forget, ((0, 0), (0, LANES - FOX_HEADS))))
    eye = jnp.eye(n_groups, dtype=pool_w.dtype)
    wpool = jnp.einsum('lgcd,gh->lgchd', pool_w, eye).reshape(DEPTH, POOL_WIDTH, POOL_WIDTH).astype(_BF16)
    sguw = sgu_w.reshape(DEPTH, SGU_GROUPS * SGU_CHUNK, SGU_CHUNK)
    bmat = jnp.repeat(jnp.swapaxes(sgu_b, 1, 2), SGU_WIDTH // SGU_GROUPS, axis=2)
    g_mix, g_mem, g_x, g_ffn = rows(norm_mix_g), rows(norm_mem_g), rows(norm_xattn_g), rows(norm_ffn_g)
    pscale, sgug, bg = rows(pool_scale), rows(sgu_norm_g), rows(b_gate)

    xc = x.reshape(t, d)
    mem2d = mem.reshape(b * mlen, d)
    for l in range(DEPTH):
        xk, xv = _mem_kv(l, mem2d, g_mem, w_xkv)
        pa, q2, k2, v, pc = _mix_in(l, xc, seq, g_mix, w_in_t, bf, wpool, pscale, sgug, sguw, bmat, e, en, cq, ck)
        att = _fox(q2.reshape(b, seq, -1), k2.reshape(b, seq, -1), v.reshape(b, seq, -1))
        xc = _merge_xattn(l, xc, seq, pa, att.reshape(t, -1), pc, g_mix, w_in_t, bg,
                          w_branch_a, w_branch_b, w_branch_c, w_out, g_x, w_xq,
                          xk.reshape(b, mlen, d), xv.reshape(b, mlen, d), w_xo)
        xc = _ffn(l, xc, g_ffn, w_ff1, w_ff2, final_norm_g.reshape(1, d), l == DEPTH - 1)
    return xc.reshape(b, seq, d)
```

```python
import functools
import math

import numpy as np
import jax
import jax.numpy as jnp
from jax import lax
from jax.experimental import pallas as pl
from jax.experimental.pallas import tpu as pltpu

D_MODEL = 1024
DEPTH = 2
POOL_GROUP_DIM = 64
POOL_WIDTH = 256
POOL_WINDOWS = (2, 4, 8, 16)
POOL_HALO = 16
FOX_HEADS = 8
FOX_HEAD_DIM = 64
FOX_WIDTH = 512
SGU_WIDTH = 256
SGU_GROUPS = 4
SGU_CHUNK = 128
XATTN_HEADS = 4
XATTN_HEAD_DIM = 256
D_FF = 4096
EPS = 1e-6
NEG = -1e30
LOG2E = math.log2(math.e)

LANES = 128
AUG_LANES = 8
QNORM_LANE = 6
KNORM_LANE = 7
NORM_CAP = 1e30
HEAD_PAIRS = FOX_HEADS // 2
PAIR_W = 2 * LANES

OFF_A = 0
OFF_Q = OFF_A + POOL_WIDTH
OFF_K = OFF_Q + FOX_WIDTH
OFF_V = OFF_K + FOX_WIDTH
OFF_F = OFF_V + FOX_WIDTH
OFF_C = OFF_F + FOX_HEADS
OFF_G = OFF_C + 2 * SGU_WIDTH
N_IN = OFF_G + 3 * D_MODEL

TM = 512
TM_MIX = 1024
TQ = 256
VMEM_LIMIT = 56 * 1024 * 1024

_F32 = jnp.float32
_BF16 = jnp.bfloat16


def _dot(a, b):
    return jnp.dot(a, b, preferred_element_type=_F32)


def _dot_t(a, b):
    return lax.dot_general(a, b, (((1,), (1,)), ((), ())), preferred_element_type=_F32)


def _rms(x, g):
    ms = jnp.mean(x * x, axis=-1, keepdims=True)
    return x * lax.rsqrt(ms + EPS) * g


def _rms_factored(x, g):
    ms = jnp.mean(x * x, axis=-1, keepdims=True)
    return (x * g).astype(_BF16), lax.rsqrt(ms + EPS)


def _const_spec(shape):
    nd = len(shape)
    return pl.BlockSpec(shape, lambda *_: (0,) * nd, pipeline_mode=pl.Buffered(1))


def _layer_spec(tail, l):
    nd = len(tail)
    return pl.BlockSpec((None,) + tuple(tail), lambda *_: (l,) + (0,) * nd, pipeline_mode=pl.Buffered(1))


def _params(sem):
    return pltpu.CompilerParams(dimension_semantics=sem, vmem_limit_bytes=VMEM_LIMIT)


def _mem_kv_kernel(mem_ref, g_ref, w_ref, xk_ref, xv_ref):
    hm, r = _rms_factored(mem_ref[...], g_ref[...])
    xk_ref[...] = (r * _dot(hm, w_ref[:, :D_MODEL].astype(_BF16))).astype(_BF16)
    xv_ref[...] = (r * _dot(hm, w_ref[:, D_MODEL:].astype(_BF16))).astype(_BF16)


def _mem_kv(l, mem2d, g, w_xkv):
    rows = mem2d.shape[0]
    return pl.pallas_call(
        _mem_kv_kernel,
        grid=(rows // TM,),
        in_specs=[pl.BlockSpec((TM, D_MODEL), lambda i: (i, 0)),
                  _layer_spec((1, D_MODEL), l),
                  _layer_spec((D_MODEL, 2 * D_MODEL), l)],
        out_specs=[pl.BlockSpec((TM, D_MODEL), lambda i: (i, 0))] * 2,
        out_shape=[jax.ShapeDtypeStruct((rows, D_MODEL), _BF16)] * 2,
        compiler_params=_params(("arbitrary",)),
        name="mem_kv",
    )(mem2d, g, w_xkv)


def _mix_in_kernel(tm, tiles_per_seq,
                   x_ref, g_ref, w_ref, bf_ref, wpool_ref, pscale_ref, sgug_ref, sguw_ref,
                   bmat_ref, e_ref, en_ref, cq_ref, ck_ref,
                   pa_ref, q2_ref, k2_ref, v_ref, pc_ref,
                   halo_ref, fcarry_ref):
    jj = pl.program_id(0) % tiles_per_seq

    @pl.when(jj == 0)
    def _():
        halo_ref[...] = jnp.zeros_like(halo_ref)
        fcarry_ref[...] = jnp.zeros_like(fcarry_ref)

    h = _rms(x_ref[...], g_ref[...]).astype(_BF16)

    wf = jnp.concatenate([w_ref[OFF_F:OFF_C, :], jnp.zeros((LANES - FOX_HEADS, D_MODEL), _F32)], axis=0)
    z = _dot_t(h, wf.astype(_BF16)) + bf_ref[...]
    logf = jnp.minimum(z, 0.0) - jnp.log1p(jnp.exp(-jnp.abs(z)))
    row = lax.broadcasted_iota(jnp.int32, (tm, LANES), 0)
    c = logf
    shift = 1
    while shift < tm:
        c = c + jnp.where(row >= shift, pltpu.roll(c, shift, 0), 0.0)
        shift *= 2
    fcum = (c + fcarry_ref[...]) * LOG2E
    fcarry_ref[...] = fcarry_ref[...] + jnp.sum(logf, axis=0, keepdims=True)
    hi = fcum.astype(_BF16).astype(_F32)
    r1 = fcum - hi
    mid = r1.astype(_BF16).astype(_F32)
    lo = r1 - mid
    lane = lax.broadcasted_iota(jnp.int32, (tm, LANES), 1)
    packed = jnp.where(lane < FOX_HEADS, hi,
                       jnp.where(lane < 2 * FOX_HEADS, pltpu.roll(mid, FOX_HEADS, 1),
                                 jnp.where(lane < 3 * FOX_HEADS, pltpu.roll(lo, 2 * FOX_HEADS, 1), 0.0)))

    a = _dot_t(h, w_ref[OFF_A:OFF_Q, :].astype(_BF16))
    cz = _dot_t(h, w_ref[OFF_C:OFF_G, :].astype(_BF16))
    ext = jnp.concatenate([halo_ref[...], a], axis=0)
    halo_ref[...] = a[tm - POOL_HALO:, :]
    w2 = ext + pltpu.roll(ext, 1, 0)
    w4 = w2 + pltpu.roll(w2, 2, 0)
    w8 = w4 + pltpu.roll(w4, 4, 0)
    w16 = w8 + pltpu.roll(w8, 8, 0)
    grp = lax.broadcasted_iota(jnp.int32, (tm, POOL_WIDTH), 1) // POOL_GROUP_DIM
    wsel = jnp.where(grp == 0, w2[POOL_HALO:],
                     jnp.where(grp == 1, w4[POOL_HALO:],
                               jnp.where(grp == 2, w8[POOL_HALO:], w16[POOL_HALO:])))
    pos = jj * tm + lax.broadcasted_iota(jnp.int32, (tm, POOL_WIDTH), 0)
    win = jnp.left_shift(2, grp)
    cnt = jnp.minimum(pos + 1, win).astype(_F32)
    d = (wsel / cnt - a).astype(_BF16)

    q = _dot_t(h, w_ref[OFF_Q:OFF_K, :].astype(_BF16)) * (FOX_HEAD_DIM ** -0.5 * LOG2E)
    k = _dot_t(h, w_ref[OFF_K:OFF_V, :].astype(_BF16))
    v_ref[...] = _dot_t(h, w_ref[OFF_V:OFF_F, :].astype(_BF16)).astype(_BF16)
    qb = q.astype(_BF16)
    kb = k.astype(_BF16)
    for p in range(HEAD_PAIRS):
        q2_ref[:, p * PAIR_W:p * PAIR_W + LANES] = qb[:, p * LANES:(p + 1) * LANES]
        k2_ref[:, p * PAIR_W:p * PAIR_W + LANES] = kb[:, p * LANES:(p + 1) * LANES]
    qsq = (q * q).astype(_BF16)
    ksq = (k * k).astype(_BF16)

    pa_ref[...] = (_dot(d, wpool_ref[...]) * pscale_ref[...]).astype(_BF16)
    zc = cz * (0.5 * (1.0 + jnp.tanh(math.sqrt(2.0 / math.pi) * (cz + 0.044715 * (cz * cz * cz)))))
    u = zc[:, :SGU_WIDTH]
    vv = _rms(zc[:, SGU_WIDTH:], sgug_ref[...])
    wrow = lax.broadcasted_iota(jnp.int32, (SGU_GROUPS * SGU_CHUNK, SGU_CHUNK), 0) % SGU_CHUNK
    wcol = lax.broadcasted_iota(jnp.int32, (SGU_GROUPS * SGU_CHUNK, SGU_CHUNK), 1)
    wst = jnp.where(wcol <= wrow, sguw_ref[...], 0.0).astype(_BF16)
    cgrp = lax.broadcasted_iota(jnp.int32, (SGU_CHUNK, SGU_WIDTH), 1) // (SGU_WIDTH // SGU_GROUPS)
    for ci in range(tm // SGU_CHUNK):
        rs = slice(ci * SGU_CHUNK, (ci + 1) * SGU_CHUNK)
        r = _dot(wst, vv[rs].astype(_BF16))
        mixed = jnp.where(cgrp == 0, r[0:SGU_CHUNK],
                          jnp.where(cgrp == 1, r[SGU_CHUNK:2 * SGU_CHUNK],
                                    jnp.where(cgrp == 2, r[2 * SGU_CHUNK:3 * SGU_CHUNK],
                                              r[3 * SGU_CHUNK:])))
        pc_ref[rs, :] = (u[rs] * (mixed + bmat_ref[...])).astype(_BF16)

    aug = _dot(packed.astype(_BF16), e_ref[...])
    nq = jnp.minimum(_dot(qsq, en_ref[:, :LANES]), NORM_CAP)
    nk = jnp.minimum(_dot(ksq, en_ref[:, LANES:]), NORM_CAP)
    aq = (aug[:, :LANES] + cq_ref[...] + nq).astype(_BF16)
    ak = (aug[:, LANES:] + ck_ref[...] + nk).astype(_BF16)
    for p in range(HEAD_PAIRS):
        q2_ref[:, p * PAIR_W + LANES:(p + 1) * PAIR_W] = aq
        k2_ref[:, p * PAIR_W + LANES:(p + 1) * PAIR_W] = ak


def _mix_in(l, x2d, seq, g, w_in_t, bf, wpool, pscale, sgug, sguw, bmat, e, en, cq, ck):
    t = x2d.shape[0]
    tm = TM_MIX
    row_spec = lambda w: pl.BlockSpec((tm, w), lambda i: (i, 0))
    return pl.pallas_call(
        functools.partial(_mix_in_kernel, tm, seq // tm),
        grid=(t // tm,),
        in_specs=[row_spec(D_MODEL),
                  _layer_spec((1, D_MODEL), l),
                  _layer_spec((OFF_G, D_MODEL), l),
                  _layer_spec((1, LANES), l),
                  _layer_spec((POOL_WIDTH, POOL_WIDTH), l),
                  _layer_spec((1, POOL_WIDTH), l),
                  _layer_spec((1, SGU_WIDTH), l),
                  _layer_spec((SGU_GROUPS * SGU_CHUNK, SGU_CHUNK), l),
                  _layer_spec((SGU_CHUNK, SGU_WIDTH), l),
                  _const_spec((LANES, 2 * LANES)),
                  _const_spec((FOX_WIDTH, 2 * LANES)),
                  _const_spec((1, LANES)),
                  _const_spec((1, LANES))],
        out_specs=[row_spec(POOL_WIDTH), row_spec(HEAD_PAIRS * PAIR_W), row_spec(HEAD_PAIRS * PAIR_W),
                   row_spec(FOX_WIDTH), row_spec(SGU_WIDTH)],
        out_shape=[jax.ShapeDtypeStruct((t, POOL_WIDTH), _BF16),
                   jax.ShapeDtypeStruct((t, HEAD_PAIRS * PAIR_W), _BF16),
                   jax.ShapeDtypeStruct((t, HEAD_PAIRS * PAIR_W), _BF16),
                   jax.ShapeDtypeStruct((t, FOX_WIDTH), _BF16),
                   jax.ShapeDtypeStruct((t, SGU_WIDTH), _BF16)],
        scratch_shapes=[pltpu.VMEM((POOL_HALO, POOL_WIDTH), _F32),
                        pltpu.VMEM((1, LANES), _F32)],
        compiler_params=_params(("arbitrary",)),
        name="mix_in",
    )(x2d, g, w_in_t, bf, wpool, pscale, sgug, sguw, bmat, e, en, cq, ck)


FOX_LOGIT_BOUND = 64.0
NORM_SLACK = 1.1
COLMAX_SPLIT = 8


def _aug_colmax(ref):
    rows = ref.shape[0] // COLMAX_SPLIT
    parts = [ref[r * rows:(r + 1) * rows, LANES:] for r in range(COLMAX_SPLIT)]
    while len(parts) > 1:
        parts = [jnp.maximum(a, b) for a, b in zip(parts[::2], parts[1::2])]
    return jnp.max(parts[0], axis=0, keepdims=True).astype(_F32)


def _pair_logit_bound2(q2_ref, k2_ref, pair):
    qmax, kmax = _aug_colmax(q2_ref), _aug_colmax(k2_ref)
    prod = qmax * pltpu.roll(kmax, LANES - (KNORM_LANE - QNORM_LANE), 1)
    lane = lax.broadcasted_iota(jnp.int32, prod.shape, 1)
    first = 2 * pair * AUG_LANES + QNORM_LANE
    return jnp.max(jnp.where((lane == first) | (lane == first + AUG_LANES), prod, 0.0))


PAIRS_PER_STEP = 2


def _fox_kernel(seq, q4_ref, k4_ref, v4_ref, o4_ref, s_ref, p_ref, vext_ref):
    n_blocks = seq // TQ
    lane = lax.broadcasted_iota(jnp.int32, (TQ, PAIR_W), 1)
    drow = lax.broadcasted_iota(jnp.int32, (2 * TQ, TQ), 0) % TQ
    dcol = lax.broadcasted_iota(jnp.int32, (2 * TQ, TQ), 1)
    causal = dcol <= drow
    out_lane = lax.broadcasted_iota(jnp.int32, (TQ, LANES), 1)

    class Pair:
        def __init__(self, pp):
            self.pp = pp
            self.index = PAIRS_PER_STEP * pl.program_id(1) + pp
            self.q2 = q4_ref.at[:, pp * PAIR_W:(pp + 1) * PAIR_W]
            self.k2 = k4_ref.at[:, pp * PAIR_W:(pp + 1) * PAIR_W]
            self.v = v4_ref.at[:, pp * LANES:(pp + 1) * LANES]
            self.o = o4_ref.at[:, pp * LANES:(pp + 1) * LANES]
            self.keep = []
            for hh in range(2):
                aug_lo = LANES + (2 * self.index + hh) * AUG_LANES
                self.keep.append(((lane >= hh * FOX_HEAD_DIM) & (lane < (hh + 1) * FOX_HEAD_DIM))
                                 | ((lane >= aug_lo) & (lane < aug_lo + AUG_LANES)))

        def stacked_q(self, i):
            qb = self.q2[i * TQ:(i + 1) * TQ, :]
            zero = jnp.zeros_like(qb)
            return jnp.concatenate([jnp.where(self.keep[0], qb, zero), jnp.where(self.keep[1], qb, zero)], axis=0)

        def store_heads(self, i, o):
            self.o[i * TQ:(i + 1) * TQ, :] = jnp.where(out_lane < FOX_HEAD_DIM, o[:TQ], o[TQ:]).astype(_BF16)

    pairs = [Pair(pp) for pp in range(PAIRS_PER_STEP)]
    bound2 = functools.reduce(jnp.maximum, [_pair_logit_bound2(pr.q2, pr.k2, pr.index) for pr in pairs])
    bounded = bound2 <= FOX_LOGIT_BOUND * FOX_LOGIT_BOUND / NORM_SLACK

    @pl.when(bounded)
    def _():
        for pr in pairs:
            vext_ref[pr.pp, :, :LANES] = pr.v[...]
            vext_ref[pr.pp, :, LANES:] = jnp.ones((seq, LANES), _BF16)

        items = [(pr, i) for pr in pairs for i in range(n_blocks)]

        def probabilities(n):
            pr, i = items[n]
            qm = pr.stacked_q(i)
            for c in range(i + 1):
                sc = _dot_t(qm, pr.k2[c * TQ:(c + 1) * TQ, :])
                if c == i:
                    sc = jnp.where(causal, sc, NEG)
                p_ref[n % 2, :, c * TQ:(c + 1) * TQ] = jnp.exp2(sc).astype(_BF16)

        probabilities(0)
        for n, (pr, i) in enumerate(items):
            if n + 1 < len(items):
                probabilities(n + 1)
            kend = (i + 1) * TQ
            o = _dot(p_ref[n % 2, :, :kend], vext_ref[pr.pp, :kend, :])
            pr.store_heads(i, o[:, :LANES] * (1.0 / o[:, LANES:]))

    @pl.when(jnp.logical_not(bounded))
    def _():
        for pr in pairs:
            for i in range(n_blocks):
                kend = (i + 1) * TQ
                s_ref[:, :kend] = _dot_t(pr.stacked_q(i), pr.k2[:kend, :])

                mrun = jnp.full((2 * TQ, LANES), NEG, _F32)
                for c in range(i + 1):
                    sc = s_ref[:, c * TQ:(c + 1) * TQ]
                    if c == i:
                        sc = jnp.where(causal, sc, NEG)
                        s_ref[:, c * TQ:(c + 1) * TQ] = sc
                    mrun = jnp.maximum(mrun, jnp.maximum(sc[:, :LANES], sc[:, LANES:]))
                mb = jnp.broadcast_to(jnp.max(mrun, axis=1, keepdims=True), (2 * TQ, LANES))

                lrun = jnp.zeros((2 * TQ, LANES), _F32)
                for c in range(i + 1):
                    p0 = jnp.exp2(s_ref[:, c * TQ:c * TQ + LANES] - mb)
                    p1 = jnp.exp2(s_ref[:, c * TQ + LANES:(c + 1) * TQ] - mb)
                    lrun = lrun + (p0 + p1)
                    p_ref[0, :, c * TQ:c * TQ + LANES] = p0.astype(_BF16)
                    p_ref[0, :, c * TQ + LANES:(c + 1) * TQ] = p1.astype(_BF16)
                linv = 1.0 / jnp.sum(lrun, axis=1, keepdims=True)
                pr.store_heads(i, _dot(p_ref[0, :, :kend], pr.v[:kend, :]) * linv)


def _fox(q2, k2, v):
    b, seq, _ = q2.shape
    n = PAIRS_PER_STEP
    return pl.pallas_call(
        functools.partial(_fox_kernel, seq),
        grid=(b, HEAD_PAIRS // n),
        in_specs=[pl.BlockSpec((None, seq, n * PAIR_W), lambda bi, g: (bi, 0, g)),
                  pl.BlockSpec((None, seq, n * PAIR_W), lambda bi, g: (bi, 0, g)),
                  pl.BlockSpec((None, seq, n * LANES), lambda bi, g: (bi, 0, g))],
        out_specs=pl.BlockSpec((None, seq, n * LANES), lambda bi, g: (bi, 0, g)),
        out_shape=jax.ShapeDtypeStruct((b, seq, FOX_WIDTH), _BF16),
        scratch_shapes=[pltpu.VMEM((2 * TQ, seq), _F32),
                        pltpu.VMEM((2, 2 * TQ, seq), _BF16),
                        pltpu.VMEM((n, seq, 2 * LANES), _BF16)],
        compiler_params=_params(("arbitrary", "arbitrary")),
        name="fox_attention",
    )(q2, k2, v)


MERGE_CHUNK = 256


def _merge_xattn_kernel(x_ref, pa_ref, att_ref, pc_ref, gmix_ref, wg_ref, bg_ref,
                        wba_ref, wbb_ref, wbc_ref, wout_ref, gx_ref, wxq_ref,
                        xk_ref, xv_ref, wxo_ref, o_ref, merged_ref):
    x = x_ref[...]
    h, r = _rms_factored(x, gmix_ref[...])
    branches = ((pa_ref, wba_ref), (att_ref, wbb_ref), (pc_ref, wbc_ref))
    for n in range(D_MODEL // MERGE_CHUNK):
        cols = slice(n * MERGE_CHUNK, (n + 1) * MERGE_CHUNK)
        acc = None
        for j, (b_ref, wb_ref) in enumerate(branches):
            gcols = slice(j * D_MODEL + n * MERGE_CHUNK, j * D_MODEL + (n + 1) * MERGE_CHUNK)
            zg = r * _dot_t(h, wg_ref[gcols, :].astype(_BF16)) + bg_ref[:, gcols]
            gate = 0.5 * jnp.tanh(0.5 * zg) + 0.5
            term = gate * _dot(b_ref[...], wb_ref[:, cols].astype(_BF16))
            acc = term if acc is None else acc + term
        merged_ref[:, cols] = acc.astype(_BF16)
    x1 = x + _dot(merged_ref[...], wout_ref[...].astype(_BF16))

    hx, r1 = _rms_factored(x1, gx_ref[...])
    qscale = r1 * (XATTN_HEAD_DIM ** -0.5 * LOG2E)
    head_cols = [slice(hd * XATTN_HEAD_DIM, (hd + 1) * XATTN_HEAD_DIM) for hd in range(XATTN_HEADS)]
    xq, s, o = {}, {}, {}
    y = x1
    for t in range(XATTN_HEADS + 3):
        if t < XATTN_HEADS:
            xq[t] = (_dot(hx, wxq_ref[:, head_cols[t]].astype(_BF16)) * qscale).astype(_BF16)
        if 0 <= t - 1 < XATTN_HEADS:
            s[t - 1] = _dot_t(xq[t - 1], xk_ref[:, head_cols[t - 1]])
        if 0 <= t - 2 < XATTN_HEADS:
            p = jnp.exp2(s[t - 2] - jnp.max(s[t - 2], axis=1, keepdims=True))
            linv = 1.0 / jnp.sum(p, axis=1, keepdims=True)
            o[t - 2] = (_dot(p.astype(_BF16), xv_ref[:, head_cols[t - 2]]) * linv).astype(_BF16)
        if 0 <= t - 3 < XATTN_HEADS:
            y = y + _dot(o[t - 3], wxo_ref[head_cols[t - 3], :].astype(_BF16))
    o_ref[...] = y


def _merge_xattn(l, x2d, seq, pa, att, pc, gmix, w_in_t, b_gate, wba, wbb, wbc, w_out, gx, w_xq, xk, xv, w_xo):
    t = x2d.shape[0]
    mlen = xk.shape[1]
    tiles_per_seq = seq // TM
    row_spec = lambda w: pl.BlockSpec((TM, w), lambda i: (i, 0))
    mem_spec = pl.BlockSpec((None, mlen, D_MODEL), lambda i: (i // tiles_per_seq, 0, 0))
    return pl.pallas_call(
        _merge_xattn_kernel,
        grid=(t // TM,),
        in_specs=[row_spec(D_MODEL), row_spec(POOL_WIDTH), row_spec(FOX_WIDTH), row_spec(SGU_WIDTH),
                  _layer_spec((1, D_MODEL), l),
                  pl.BlockSpec((None, pl.Element(3 * D_MODEL), pl.Element(D_MODEL)),
                               lambda i: (l, OFF_G, 0), pipeline_mode=pl.Buffered(1)),
                  _layer_spec((1, 3 * D_MODEL), l),
                  _layer_spec((POOL_WIDTH, D_MODEL), l),
                  _layer_spec((FOX_WIDTH, D_MODEL), l),
                  _layer_spec((SGU_WIDTH, D_MODEL), l),
                  _layer_spec((D_MODEL, D_MODEL), l),
                  _layer_spec((1, D_MODEL), l),
                  _layer_spec((D_MODEL, D_MODEL), l),
                  mem_spec, mem_spec,
                  _layer_spec((D_MODEL, D_MODEL), l)],
        out_specs=row_spec(D_MODEL),
        out_shape=jax.ShapeDtypeStruct((t, D_MODEL), _F32),
        scratch_shapes=[pltpu.VMEM((TM, D_MODEL), _BF16)],
        compiler_params=_params(("arbitrary",)),
        name="merge_xattn",
    )(x2d, pa, att, pc, gmix, w_in_t, b_gate, wba, wbb, wbc, w_out, gx, w_xq, xk, xv, w_xo)


FF_CHUNK = 1024


def _ffn_kernel(final_norm, x_ref, g_ref, w1_ref, w2_ref, gfin_ref, o_ref, act_ref):
    x = x_ref[...]
    hf, r = _rms_factored(x, g_ref[...])
    for c in range(D_FF // FF_CHUNK):
        cols = slice(c * FF_CHUNK, (c + 1) * FF_CHUNK)
        u = jnp.maximum(_dot(hf, w1_ref[:, cols].astype(_BF16)), 0.0)
        act_ref[:, cols] = (u * u).astype(_BF16)
    y = x + (r * r) * _dot(act_ref[...], w2_ref[...].astype(_BF16))
    if final_norm:
        y = _rms(y, gfin_ref[...])
    o_ref[...] = y


def _ffn(l, x2d, g, w1, w2, gfin, final_norm):
    t = x2d.shape[0]
    return pl.pallas_call(
        functools.partial(_ffn_kernel, final_norm),
        grid=(t // TM,),
        in_specs=[pl.BlockSpec((TM, D_MODEL), lambda i: (i, 0)),
                  _layer_spec((1, D_MODEL), l),
                  _layer_spec((D_MODEL, D_FF), l),
                  _layer_spec((D_FF, D_MODEL), l),
                  _const_spec((1, D_MODEL))],
        out_specs=pl.BlockSpec((TM, D_MODEL), lambda i: (i, 0)),
        out_shape=jax.ShapeDtypeStruct((t, D_MODEL), _F32),
        scratch_shapes=[pltpu.VMEM((TM, D_FF), _BF16)],
        compiler_params=_params(("arbitrary",)),
        name="ffn",
    )(x2d, g, w1, w2, gfin)


def _aug_constants():
    e = np.zeros((LANES, 2 * LANES), np.float32)
    en = np.zeros((FOX_WIDTH, 2 * LANES), np.float32)
    cq = np.zeros((1, LANES), np.float32)
    ck = np.zeros((1, LANES), np.float32)
    for h in range(FOX_HEADS):
        for piece in range(3):
            e[piece * FOX_HEADS + h, h * AUG_LANES + piece] = 1.0
            e[piece * FOX_HEADS + h, LANES + h * AUG_LANES + 3 + piece] = -1.0
            cq[0, h * AUG_LANES + 3 + piece] = 1.0
            ck[0, h * AUG_LANES + piece] = 1.0
        cols = slice(h * FOX_HEAD_DIM, (h + 1) * FOX_HEAD_DIM)
        en[cols, h * AUG_LANES + QNORM_LANE] = 1.0
        en[cols, LANES + h * AUG_LANES + KNORM_LANE] = 1.0
    return jnp.asarray(e, _BF16), jnp.asarray(en, _BF16), jnp.asarray(cq), jnp.asarray(ck)


def kernel(x, mem, norm_mix_g, w_in, b_forget, pool_w, pool_scale, sgu_norm_g, sgu_w, sgu_b,
           w_branch_a, w_branch_b, w_branch_c, b_gate, w_out, norm_xattn_g, norm_mem_g,
           w_xq, w_xkv, w_xo, norm_ffn_g, w_ff1, w_ff2, final_norm_g):
    b, seq, d = x.shape
    mlen = mem.shape[1]
    t = b * seq
    e, en, cq, ck = _aug_constants()
    rows = lambda a: a.reshape(a.shape[0], 1, -1)
    n_groups = len(POOL_WINDOWS)

    w_in_t = jnp.swapaxes(w_in, 1, 2)
    bf = rows(jnp.pad(b_forget, ((0, 0), (0, LANES - FOX_HEADS))))
    eye = jnp.eye(n_groups, dtype=pool_w.dtype)
    wpool = jnp.einsum('lgcd,gh->lgchd', pool_w, eye).reshape(DEPTH, POOL_WIDTH, POOL_WIDTH).astype(_BF16)
    sguw = sgu_w.reshape(DEPTH, SGU_GROUPS * SGU_CHUNK, SGU_CHUNK)
    bmat = jnp.repeat(jnp.swapaxes(sgu_b, 1, 2), SGU_WIDTH // SGU_GROUPS, axis=2)
    g_mix, g_mem, g_x, g_ffn = rows(norm_mix_g), rows(norm_mem_g), rows(norm_xattn_g), rows(norm_ffn_g)
    pscale, sgug, bg = rows(pool_scale), rows(sgu_norm_g), rows(b_gate)

    xc = x.reshape(t, d)
    mem2d = mem.reshape(b * mlen, d)
    for l in range(DEPTH):
        xk, xv = _mem_kv(l, mem2d, g_mem, w_xkv)
        pa, q2, k2, v, pc = _mix_in(l, xc, seq, g_mix, w_in_t, bf, wpool, pscale, sgug, sguw, bmat, e, en, cq, ck)
        att = _fox(q2.reshape(b, seq, -1), k2.reshape(b, seq, -1), v.reshape(b, seq, -1))
        xc = _merge_xattn(l, xc, seq, pa, att.reshape(t, -1), pc, g_mix, w_in_t, bg,
                          w_branch_a, w_branch_b, w_branch_c, w_out, g_x, w_xq,
                          xk.reshape(b, mlen, d), xv.reshape(b, mlen, d), w_xo)
        xc = _ffn(l, xc, g_ffn, w_ff1, w_ff2, final_norm_g.reshape(1, d), l == DEPTH - 1)
    return xc.reshape(b, seq, d)
```

```python
import functools
import math

import numpy as np
import jax
import jax.numpy as jnp
from jax import lax
from jax.experimental import pallas as pl
from jax.experimental.pallas import tpu as pltpu

D_MODEL = 1024
DEPTH = 2
POOL_GROUP_DIM = 64
POOL_WIDTH = 256
POOL_WINDOWS = (2, 4, 8, 16)
POOL_HALO = 16
FOX_HEADS = 8
FOX_HEAD_DIM = 64
FOX_WIDTH = 512
SGU_WIDTH = 256
SGU_GROUPS = 4
SGU_CHUNK = 128
XATTN_HEADS = 4
XATTN_HEAD_DIM = 256
D_FF = 4096
EPS = 1e-6
NEG = -1e30
LOG2E = math.log2(math.e)

LANES = 128
AUG_LANES = 8
QNORM_LANE = 6
KNORM_LANE = 7
NORM_CAP = 1e30
HEAD_PAIRS = FOX_HEADS // 2
PAIR_W = 2 * LANES

OFF_A = 0
OFF_Q = OFF_A + POOL_WIDTH
OFF_K = OFF_Q + FOX_WIDTH
OFF_V = OFF_K + FOX_WIDTH
OFF_F = OFF_V + FOX_WIDTH
OFF_C = OFF_F + FOX_HEADS
OFF_G = OFF_C + 2 * SGU_WIDTH
N_IN = OFF_G + 3 * D_MODEL

TM = 512
TM_MIX = 1024
TQ = 256
VMEM_LIMIT = 56 * 1024 * 1024

_F32 = jnp.float32
_BF16 = jnp.bfloat16


def _dot(a, b):
    return jnp.dot(a, b, preferred_element_type=_F32)


def _dot_t(a, b):
    return lax.dot_general(a, b, (((1,), (1,)), ((), ())), preferred_element_type=_F32)


def _rms(x, g):
    ms = jnp.mean(x * x, axis=-1, keepdims=True)
    return x * lax.rsqrt(ms + EPS) * g


def _rms_factored(x, g):
    ms = jnp.mean(x * x, axis=-1, keepdims=True)
    return (x * g).astype(_BF16), lax.rsqrt(ms + EPS)


def _const_spec(shape):
    nd = len(shape)
    return pl.BlockSpec(shape, lambda *_: (0,) * nd, pipeline_mode=pl.Buffered(1))


def _layer_spec(tail, l):
    nd = len(tail)
    return pl.BlockSpec((None,) + tuple(tail), lambda *_: (l,) + (0,) * nd, pipeline_mode=pl.Buffered(1))


def _params(sem):
    return pltpu.CompilerParams(dimension_semantics=sem, vmem_limit_bytes=VMEM_LIMIT)


def _mem_kv_kernel(mem_ref, g_ref, w_ref, xk_ref, xv_ref):
    hm, r = _rms_factored(mem_ref[...], g_ref[...])
    xk_ref[...] = (r * _dot(hm, w_ref[:, :D_MODEL].astype(_BF16))).astype(_BF16)
    xv_ref[...] = (r * _dot(hm, w_ref[:, D_MODEL:].astype(_BF16))).astype(_BF16)


def _mem_kv(l, mem2d, g, w_xkv):
    rows = mem2d.shape[0]
    return pl.pallas_call(
        _mem_kv_kernel,
        grid=(rows // TM,),
        in_specs=[pl.BlockSpec((TM, D_MODEL), lambda i: (i, 0)),
                  _layer_spec((1, D_MODEL), l),
                  _layer_spec((D_MODEL, 2 * D_MODEL), l)],
        out_specs=[pl.BlockSpec((TM, D_MODEL), lambda i: (i, 0))] * 2,
        out_shape=[jax.ShapeDtypeStruct((rows, D_MODEL), _BF16)] * 2,
        compiler_params=_params(("arbitrary",)),
        name="mem_kv",
    )(mem2d, g, w_xkv)


def _mix_in_kernel(tm, tiles_per_seq,
                   x_ref, g_ref, w_ref, bf_ref, wpool_ref, pscale_ref, sgug_ref, sguw_ref,
                   bmat_ref, e_ref, en_ref, cq_ref, ck_ref,
                   pa_ref, q2_ref, k2_ref, v_ref, pc_ref,
                   halo_ref, fcarry_ref):
    jj = pl.program_id(0) % tiles_per_seq

    @pl.when(jj == 0)
    def _():
        halo_ref[...] = jnp.zeros_like(halo_ref)
        fcarry_ref[...] = jnp.zeros_like(fcarry_ref)

    h = _rms(x_ref[...], g_ref[...]).astype(_BF16)

    wf = jnp.concatenate([w_ref[OFF_F:OFF_C, :], jnp.zeros((LANES - FOX_HEADS, D_MODEL), _F32)], axis=0)
    z = _dot_t(h, wf.astype(_BF16)) + bf_ref[...]
    logf = jnp.minimum(z, 0.0) - jnp.log1p(jnp.exp(-jnp.abs(z)))
    row = lax.broadcasted_iota(jnp.int32, (tm, LANES), 0)
    c = logf
    shift = 1
    while shift < tm:
        c = c + jnp.where(row >= shift, pltpu.roll(c, shift, 0), 0.0)
        shift *= 2
    fcum = (c + fcarry_ref[...]) * LOG2E
    fcarry_ref[...] = fcarry_ref[...] + jnp.sum(logf, axis=0, keepdims=True)
    hi = fcum.astype(_BF16).astype(_F32)
    r1 = fcum - hi
    mid = r1.astype(_BF16).astype(_F32)
    lo = r1 - mid
    lane = lax.broadcasted_iota(jnp.int32, (tm, LANES), 1)
    packed = jnp.where(lane < FOX_HEADS, hi,
                       jnp.where(lane < 2 * FOX_HEADS, pltpu.roll(mid, FOX_HEADS, 1),
                                 jnp.where(lane < 3 * FOX_HEADS, pltpu.roll(lo, 2 * FOX_HEADS, 1), 0.0)))

    a = _dot_t(h, w_ref[OFF_A:OFF_Q, :].astype(_BF16))
    cz = _dot_t(h, w_ref[OFF_C:OFF_G, :].astype(_BF16))
    ext = jnp.concatenate([halo_ref[...], a], axis=0)
    halo_ref[...] = a[tm - POOL_HALO:, :]
    w2 = ext + pltpu.roll(ext, 1, 0)
    w4 = w2 + pltpu.roll(w2, 2, 0)
    w8 = w4 + pltpu.roll(w4, 4, 0)
    w16 = w8 + pltpu.roll(w8, 8, 0)
    grp = lax.broadcasted_iota(jnp.int32, (tm, POOL_WIDTH), 1) // POOL_GROUP_DIM
    wsel = jnp.where(grp == 0, w2[POOL_HALO:],
                     jnp.where(grp == 1, w4[POOL_HALO:],
                               jnp.where(grp == 2, w8[POOL_HALO:], w16[POOL_HALO:])))
    pos = jj * tm + lax.broadcasted_iota(jnp.int32, (tm, POOL_WIDTH), 0)
    win = jnp.left_shift(2, grp)
    cnt = jnp.minimum(pos + 1, win).astype(_F32)
    d = (wsel / cnt - a).astype(_BF16)

    q = _dot_t(h, w_ref[OFF_Q:OFF_K, :].astype(_BF16)) * (FOX_HEAD_DIM ** -0.5 * LOG2E)
    k = _dot_t(h, w_ref[OFF_K:OFF_V, :].astype(_BF16))
    v_ref[...] = _dot_t(h, w_ref[OFF_V:OFF_F, :].astype(_BF16)).astype(_BF16)
    qb = q.astype(_BF16)
    kb = k.astype(_BF16)
    for p in range(HEAD_PAIRS):
        q2_ref[:, p * PAIR_W:p * PAIR_W + LANES] = qb[:, p * LANES:(p + 1) * LANES]
        k2_ref[:, p * PAIR_W:p * PAIR_W + LANES] = kb[:, p * LANES:(p + 1) * LANES]
    qsq = (q * q).astype(_BF16)
    ksq = (k * k).astype(_BF16)

    pa_ref[...] = (_dot(d, wpool_ref[...]) * pscale_ref[...]).astype(_BF16)
    zc = cz * (0.5 * (1.0 + jnp.tanh(math.sqrt(2.0 / math.pi) * (cz + 0.044715 * (cz * cz * cz)))))
    u = zc[:, :SGU_WIDTH]
    vv = _rms(zc[:, SGU_WIDTH:], sgug_ref[...])
    wrow = lax.broadcasted_iota(jnp.int32, (SGU_GROUPS * SGU_CHUNK, SGU_CHUNK), 0) % SGU_CHUNK
    wcol = lax.broadcasted_iota(jnp.int32, (SGU_GROUPS * SGU_CHUNK, SGU_CHUNK), 1)
    wst = jnp.where(wcol <= wrow, sguw_ref[...], 0.0).astype(_BF16)
    cgrp = lax.broadcasted_iota(jnp.int32, (SGU_CHUNK, SGU_WIDTH), 1) // (SGU_WIDTH // SGU_GROUPS)
    for ci in range(tm // SGU_CHUNK):
        rs = slice(ci * SGU_CHUNK, (ci + 1) * SGU_CHUNK)
        r = _dot(wst, vv[rs].astype(_BF16))
        mixed = jnp.where(cgrp == 0, r[0:SGU_CHUNK],
                          jnp.where(cgrp == 1, r[SGU_CHUNK:2 * SGU_CHUNK],
                                    jnp.where(cgrp == 2, r[2 * SGU_CHUNK:3 * SGU_CHUNK],
                                              r[3 * SGU_CHUNK:])))
        pc_ref[rs, :] = (u[rs] * (mixed + bmat_ref[...])).astype(_BF16)

    aug = _dot(packed.astype(_BF16), e_ref[...])
    nq = jnp.minimum(_dot(qsq, en_ref[:, :LANES]), NORM_CAP)
    nk = jnp.minimum(_dot(ksq, en_ref[:, LANES:]), NORM_CAP)
    aq = (aug[:, :LANES] + cq_ref[...] + nq).astype(_BF16)
    ak = (aug[:, LANES:] + ck_ref[...] + nk).astype(_BF16)
    for p in range(HEAD_PAIRS):
        q2_ref[:, p * PAIR_W + LANES:(p + 1) * PAIR_W] = aq
        k2_ref[:, p * PAIR_W + LANES:(p + 1) * PAIR_W] = ak


def _mix_in(l, x2d, seq, g, w_in_t, bf, wpool, pscale, sgug, sguw, bmat, e, en, cq, ck):
    t = x2d.shape[0]
    tm = TM_MIX
    row_spec = lambda w: pl.BlockSpec((tm, w), lambda i: (i, 0))
    return pl.pallas_call(
        functools.partial(_mix_in_kernel, tm, seq // tm),
        grid=(t // tm,),
        in_specs=[row_spec(D_MODEL),
                  _layer_spec((1, D_MODEL), l),
                  _layer_spec((OFF_G, D_MODEL), l),
                  _layer_spec((1, LANES), l),
                  _layer_spec((POOL_WIDTH, POOL_WIDTH), l),
                  _layer_spec((1, POOL_WIDTH), l),
                  _layer_spec((1, SGU_WIDTH), l),
                  _layer_spec((SGU_GROUPS * SGU_CHUNK, SGU_CHUNK), l),
                  _layer_spec((SGU_CHUNK, SGU_WIDTH), l),
                  _const_spec((LANES, 2 * LANES)),
                  _const_spec((FOX_WIDTH, 2 * LANES)),
                  _const_spec((1, LANES)),
                  _const_spec((1, LANES))],
        out_specs=[row_spec(POOL_WIDTH), row_spec(HEAD_PAIRS * PAIR_W), row_spec(HEAD_PAIRS * PAIR_W),
                   row_spec(FOX_WIDTH), row_spec(SGU_WIDTH)],
        out_shape=[jax.ShapeDtypeStruct((t, POOL_WIDTH), _BF16),
                   jax.ShapeDtypeStruct((t, HEAD_PAIRS * PAIR_W), _BF16),
                   jax.ShapeDtypeStruct((t, HEAD_PAIRS * PAIR_W), _BF16),
                   jax.ShapeDtypeStruct((t, FOX_WIDTH), _BF16),
                   jax.ShapeDtypeStruct((t, SGU_WIDTH), _BF16)],
        scratch_shapes=[pltpu.VMEM((POOL_HALO, POOL_WIDTH), _F32),
                        pltpu.VMEM((1, LANES), _F32)],
        compiler_params=_params(("arbitrary",)),
        name="mix_in",
    )(x2d, g, w_in_t, bf, wpool, pscale, sgug, sguw, bmat, e, en, cq, ck)


FOX_LOGIT_BOUND = 64.0
NORM_SLACK = 1.1
COLMAX_SPLIT = 8


def _aug_colmax(ref):
    rows = ref.shape[0] // COLMAX_SPLIT
    parts = [ref[r * rows:(r + 1) * rows, LANES:] for r in range(COLMAX_SPLIT)]
    while len(parts) > 1:
        parts = [jnp.maximum(a, b) for a, b in zip(parts[::2], parts[1::2])]
    return jnp.max(parts[0], axis=0, keepdims=True).astype(_F32)


def _pair_logit_bound2(q2_ref, k2_ref, pair):
    qmax, kmax = _aug_colmax(q2_ref), _aug_colmax(k2_ref)
    prod = qmax * pltpu.roll(kmax, LANES - (KNORM_LANE - QNORM_LANE), 1)
    lane = lax.broadcasted_iota(jnp.int32, prod.shape, 1)
    first = 2 * pair * AUG_LANES + QNORM_LANE
    return jnp.max(jnp.where((lane == first) | (lane == first + AUG_LANES), prod, 0.0))


def _fox_kernel(seq, q2_ref, k2_ref, v_ref, o_ref, s_ref, p_ref, vext_ref):
    pair = pl.program_id(1)
    lane = lax.broadcasted_iota(jnp.int32, (TQ, PAIR_W), 1)
    keep = []
    for hh in range(2):
        aug_lo = LANES + (2 * pair + hh) * AUG_LANES
        keep.append(((lane >= hh * FOX_HEAD_DIM) & (lane < (hh + 1) * FOX_HEAD_DIM))
                    | ((lane >= aug_lo) & (lane < aug_lo + AUG_LANES)))
    drow = lax.broadcasted_iota(jnp.int32, (2 * TQ, TQ), 0) % TQ
    dcol = lax.broadcasted_iota(jnp.int32, (2 * TQ, TQ), 1)
    causal = dcol <= drow
    out_lane = lax.broadcasted_iota(jnp.int32, (TQ, LANES), 1)

    def stacked_q(i):
        qb = q2_ref[i * TQ:(i + 1) * TQ, :]
        zero = jnp.zeros_like(qb)
        return jnp.concatenate([jnp.where(keep[0], qb, zero), jnp.where(keep[1], qb, zero)], axis=0)

    def store_heads(i, o):
        o_ref[i * TQ:(i + 1) * TQ, :] = jnp.where(out_lane < FOX_HEAD_DIM, o[:TQ], o[TQ:]).astype(_BF16)

    bounded = _pair_logit_bound2(q2_ref, k2_ref, pair) <= FOX_LOGIT_BOUND * FOX_LOGIT_BOUND / NORM_SLACK

    @pl.when(bounded)
    def _():
        vext_ref[:, :LANES] = v_ref[...]
        vext_ref[:, LANES:] = jnp.ones((seq, LANES), _BF16)

        def probabilities(i):
            qm = stacked_q(i)
            for c in range(i + 1):
                sc = _dot_t(qm, k2_ref[c * TQ:(c + 1) * TQ, :])
                if c == i:
                    sc = jnp.where(causal, sc, NEG)
                p_ref[i % 2, :, c * TQ:(c + 1) * TQ] = jnp.exp2(sc).astype(_BF16)

        n_blocks = seq // TQ
        probabilities(0)
        for i in range(n_blocks):
            if i + 1 < n_blocks:
                probabilities(i + 1)
            kend = (i + 1) * TQ
            o = _dot(p_ref[i % 2, :, :kend], vext_ref[:kend, :])
            store_heads(i, o[:, :LANES] * (1.0 / o[:, LANES:]))

    @pl.when(jnp.logical_not(bounded))
    def _():
        for i in range(seq // TQ):
            kend = (i + 1) * TQ
            s_ref[:, :kend] = _dot_t(stacked_q(i), k2_ref[:kend, :])

            mrun = jnp.full((2 * TQ, LANES), NEG, _F32)
            for c in range(i + 1):
                sc = s_ref[:, c * TQ:(c + 1) * TQ]
                if c == i:
                    sc = jnp.where(causal, sc, NEG)
                    s_ref[:, c * TQ:(c + 1) * TQ] = sc
                mrun = jnp.maximum(mrun, jnp.maximum(sc[:, :LANES], sc[:, LANES:]))
            mb = jnp.broadcast_to(jnp.max(mrun, axis=1, keepdims=True), (2 * TQ, LANES))

            lrun = jnp.zeros((2 * TQ, LANES), _F32)
            for c in range(i + 1):
                p0 = jnp.exp2(s_ref[:, c * TQ:c * TQ + LANES] - mb)
                p1 = jnp.exp2(s_ref[:, c * TQ + LANES:(c + 1) * TQ] - mb)
                lrun = lrun + (p0 + p1)
                p_ref[0, :, c * TQ:c * TQ + LANES] = p0.astype(_BF16)
                p_ref[0, :, c * TQ + LANES:(c + 1) * TQ] = p1.astype(_BF16)
            linv = 1.0 / jnp.sum(lrun, axis=1, keepdims=True)
            store_heads(i, _dot(p_ref[0, :, :kend], v_ref[:kend, :]) * linv)


def _fox(q2, k2, v):
    b, seq, _ = q2.shape
    return pl.pallas_call(
        functools.partial(_fox_kernel, seq),
        grid=(b, HEAD_PAIRS),
        in_specs=[pl.BlockSpec((None, seq, PAIR_W), lambda bi, p: (bi, 0, p)),
                  pl.BlockSpec((None, seq, PAIR_W), lambda bi, p: (bi, 0, p)),
                  pl.BlockSpec((None, seq, LANES), lambda bi, p: (bi, 0, p))],
        out_specs=pl.BlockSpec((None, seq, LANES), lambda bi, p: (bi, 0, p)),
        out_shape=jax.ShapeDtypeStruct((b, seq, FOX_WIDTH), _BF16),
        scratch_shapes=[pltpu.VMEM((2 * TQ, seq), _F32),
                        pltpu.VMEM((2, 2 * TQ, seq), _BF16),
                        pltpu.VMEM((seq, 2 * LANES), _BF16)],
        compiler_params=_params(("arbitrary", "arbitrary")),
        name="fox_attention",
    )(q2, k2, v)


MERGE_CHUNK = 256


def _merge_xattn_kernel(x_ref, pa_ref, att_ref, pc_ref, gmix_ref, wg_ref, bg_ref,
                        wba_ref, wbb_ref, wbc_ref, wout_ref, gx_ref, wxq_ref,
                        xk_ref, xv_ref, wxo_ref, o_ref, merged_ref):
    x = x_ref[...]
    h, r = _rms_factored(x, gmix_ref[...])
    branches = ((pa_ref, wba_ref), (att_ref, wbb_ref), (pc_ref, wbc_ref))
    for n in range(D_MODEL // MERGE_CHUNK):
        cols = slice(n * MERGE_CHUNK, (n + 1) * MERGE_CHUNK)
        acc = None
        for j, (b_ref, wb_ref) in enumerate(branches):
            gcols = slice(j * D_MODEL + n * MERGE_CHUNK, j * D_MODEL + (n + 1) * MERGE_CHUNK)
            zg = r * _dot_t(h, wg_ref[gcols, :].astype(_BF16)) + bg_ref[:, gcols]
            gate = 0.5 * jnp.tanh(0.5 * zg) + 0.5
            term = gate * _dot(b_ref[...], wb_ref[:, cols].astype(_BF16))
            acc = term if acc is None else acc + term
        merged_ref[:, cols] = acc.astype(_BF16)
    x1 = x + _dot(merged_ref[...], wout_ref[...].astype(_BF16))

    hx, r1 = _rms_factored(x1, gx_ref[...])
    qscale = r1 * (XATTN_HEAD_DIM ** -0.5 * LOG2E)
    head_cols = [slice(hd * XATTN_HEAD_DIM, (hd + 1) * XATTN_HEAD_DIM) for hd in range(XATTN_HEADS)]
    xq, s, o = {}, {}, {}
    y = x1
    for t in range(XATTN_HEADS + 3):
        if t < XATTN_HEADS:
            xq[t] = (_dot(hx, wxq_ref[:, head_cols[t]].astype(_BF16)) * qscale).astype(_BF16)
        if 0 <= t - 1 < XATTN_HEADS:
            s[t - 1] = _dot_t(xq[t - 1], xk_ref[:, head_cols[t - 1]])
        if 0 <= t - 2 < XATTN_HEADS:
            p = jnp.exp2(s[t - 2] - jnp.max(s[t - 2], axis=1, keepdims=True))
            linv = 1.0 / jnp.sum(p, axis=1, keepdims=True)
            o[t - 2] = (_dot(p.astype(_BF16), xv_ref[:, head_cols[t - 2]]) * linv).astype(_BF16)
        if 0 <= t - 3 < XATTN_HEADS:
            y = y + _dot(o[t - 3], wxo_ref[head_cols[t - 3], :].astype(_BF16))
    o_ref[...] = y


def _merge_xattn(l, x2d, seq, pa, att, pc, gmix, w_in_t, b_gate, wba, wbb, wbc, w_out, gx, w_xq, xk, xv, w_xo):
    t = x2d.shape[0]
    mlen = xk.shape[1]
    tiles_per_seq = seq // TM
    row_spec = lambda w: pl.BlockSpec((TM, w), lambda i: (i, 0))
    mem_spec = pl.BlockSpec((None, mlen, D_MODEL), lambda i: (i // tiles_per_seq, 0, 0))
    return pl.pallas_call(
        _merge_xattn_kernel,
        grid=(t // TM,),
        in_specs=[row_spec(D_MODEL), row_spec(POOL_WIDTH), row_spec(FOX_WIDTH), row_spec(SGU_WIDTH),
                  _layer_spec((1, D_MODEL), l),
                  pl.BlockSpec((None, pl.Element(3 * D_MODEL), pl.Element(D_MODEL)),
                               lambda i: (l, OFF_G, 0), pipeline_mode=pl.Buffered(1)),
                  _layer_spec((1, 3 * D_MODEL), l),
                  _layer_spec((POOL_WIDTH, D_MODEL), l),
                  _layer_spec((FOX_WIDTH, D_MODEL), l),
                  _layer_spec((SGU_WIDTH, D_MODEL), l),
                  _layer_spec((D_MODEL, D_MODEL), l),
                  _layer_spec((1, D_MODEL), l),
                  _layer_spec((D_MODEL, D_MODEL), l),
                  mem_spec, mem_spec,
                  _layer_spec((D_MODEL, D_MODEL), l)],
        out_specs=row_spec(D_MODEL),
        out_shape=jax.ShapeDtypeStruct((t, D_MODEL), _F32),
        scratch_shapes=[pltpu.VMEM((TM, D_MODEL), _BF16)],
        compiler_params=_params(("arbitrary",)),
        name="merge_xattn",
    )(x2d, pa, att, pc, gmix, w_in_t, b_gate, wba, wbb, wbc, w_out, gx, w_xq, xk, xv, w_xo)


FF_CHUNK = 1024
TM_FFN = 1024
W1_STAGE_ROWS = 128
W2_STAGE_ROWS = 512


def _load_bf16_weights(l, w1_hbm, w2_hbm, w1b_ref, w2b_ref, st1_ref, st2_ref, sem):
    jobs = []
    for c in range(D_MODEL // W1_STAGE_ROWS):
        rows = slice(c * W1_STAGE_ROWS, (c + 1) * W1_STAGE_ROWS)
        jobs.append((w1_hbm.at[l, rows, :], st1_ref, w1b_ref.at[rows, :]))
    for c in range(D_FF // W2_STAGE_ROWS):
        rows = slice(c * W2_STAGE_ROWS, (c + 1) * W2_STAGE_ROWS)
        jobs.append((w2_hbm.at[l, rows, :], st2_ref, w2b_ref.at[rows, :]))

    def copy(n):
        src, stage, _ = jobs[n]
        return pltpu.make_async_copy(src, stage.at[n % 2], sem.at[n % 2])

    copy(0).start()
    for n, (_, stage, dst) in enumerate(jobs):
        if n + 1 < len(jobs):
            copy(n + 1).start()
        copy(n).wait()
        dst[...] = stage[n % 2].astype(_BF16)


def _ffn_kernel(l, final_norm, x_ref, g_ref, w1_hbm, w2_hbm, gfin_ref, o_ref,
                act_ref, w1b_ref, w2b_ref, st1_ref, st2_ref, sem):
    @pl.when(pl.program_id(0) == 0)
    def _():
        _load_bf16_weights(l, w1_hbm, w2_hbm, w1b_ref, w2b_ref, st1_ref, st2_ref, sem)

    x = x_ref[...]
    hf, r = _rms_factored(x, g_ref[...])
    for c in range(D_FF // FF_CHUNK):
        cols = slice(c * FF_CHUNK, (c + 1) * FF_CHUNK)
        u = jnp.maximum(_dot(hf, w1b_ref[:, cols]), 0.0)
        act_ref[:, cols] = (u * u).astype(_BF16)
    y = x + (r * r) * _dot(act_ref[...], w2b_ref[...])
    if final_norm:
        y = _rms(y, gfin_ref[...])
    o_ref[...] = y


def _ffn(l, x2d, g, w1, w2, gfin, final_norm):
    t = x2d.shape[0]
    tm = TM_FFN
    return pl.pallas_call(
        functools.partial(_ffn_kernel, l, final_norm),
        grid=(t // tm,),
        in_specs=[pl.BlockSpec((tm, D_MODEL), lambda i: (i, 0)),
                  _layer_spec((1, D_MODEL), l),
                  pl.BlockSpec(memory_space=pl.ANY),
                  pl.BlockSpec(memory_space=pl.ANY),
                  _const_spec((1, D_MODEL))],
        out_specs=pl.BlockSpec((tm, D_MODEL), lambda i: (i, 0)),
        out_shape=jax.ShapeDtypeStruct((t, D_MODEL), _F32),
        scratch_shapes=[pltpu.VMEM((tm, D_FF), _BF16),
                        pltpu.VMEM((D_MODEL, D_FF), _BF16),
                        pltpu.VMEM((D_FF, D_MODEL), _BF16),
                        pltpu.VMEM((2, W1_STAGE_ROWS, D_FF), _F32),
                        pltpu.VMEM((2, W2_STAGE_ROWS, D_MODEL), _F32),
                        pltpu.SemaphoreType.DMA((2,))],
        compiler_params=_params(("arbitrary",)),
        name="ffn",
    )(x2d, g, w1, w2, gfin)


def _aug_constants():
    e = np.zeros((LANES, 2 * LANES), np.float32)
    en = np.zeros((FOX_WIDTH, 2 * LANES), np.float32)
    cq = np.zeros((1, LANES), np.float32)
    ck = np.zeros((1, LANES), np.float32)
    for h in range(FOX_HEADS):
        for piece in range(3):
            e[piece * FOX_HEADS + h, h * AUG_LANES + piece] = 1.0
            e[piece * FOX_HEADS + h, LANES + h * AUG_LANES + 3 + piece] = -1.0
            cq[0, h * AUG_LANES + 3 + piece] = 1.0
            ck[0, h * AUG_LANES + piece] = 1.0
        cols = slice(h * FOX_HEAD_DIM, (h + 1) * FOX_HEAD_DIM)
        en[cols, h * AUG_LANES + QNORM_LANE] = 1.0
        en[cols, LANES + h * AUG_LANES + KNORM_LANE] = 1.0
    return jnp.asarray(e, _BF16), jnp.asarray(en, _BF16), jnp.asarray(cq), jnp.asarray(ck)


def kernel(x, mem, norm_mix_g, w_in, b_forget, pool_w, pool_scale, sgu_norm_g, sgu_w, sgu_b,
           w_branch_a, w_branch_b, w_branch_c, b_gate, w_out, norm_xattn_g, norm_mem_g,
           w_xq, w_xkv, w_xo, norm_ffn_g, w_ff1, w_ff2, final_norm_g):
    b, seq, d = x.shape
    mlen = mem.shape[1]
    t = b * seq
    e, en, cq, ck = _aug_constants()
    rows = lambda a: a.reshape(a.shape[0], 1, -1)
    n_groups = len(POOL_WINDOWS)

    w_in_t = jnp.swapaxes(w_in, 1, 2)
    bf = rows(jnp.pad(b_forget, ((0, 0), (0, LANES - FOX_HEADS))))
    eye = jnp.eye(n_groups, dtype=pool_w.dtype)
    wpool = jnp.einsum('lgcd,gh->lgchd', pool_w, eye).reshape(DEPTH, POOL_WIDTH, POOL_WIDTH).astype(_BF16)
    sguw = sgu_w.reshape(DEPTH, SGU_GROUPS * SGU_CHUNK, SGU_CHUNK)
    bmat = jnp.repeat(jnp.swapaxes(sgu_b, 1, 2), SGU_WIDTH // SGU_GROUPS, axis=2)
    g_mix, g_mem, g_x, g_ffn = rows(norm_mix_g), rows(norm_mem_g), rows(norm_xattn_g), rows(norm_ffn_g)
    pscale, sgug, bg = rows(pool_scale), rows(sgu_norm_g), rows(b_gate)

    xc = x.reshape(t, d)
    mem2d = mem.reshape(b * mlen, d)
    for l in range(DEPTH):
        xk, xv = _mem_kv(l, mem2d, g_mem, w_xkv)
        pa, q2, k2, v, pc = _mix_in(l, xc, seq, g_mix, w_in_t, bf, wpool, pscale, sgug, sguw, bmat, e, en, cq, ck)
        att = _fox(q2.reshape(b, seq, -1), k2.reshape(b, seq, -1), v.reshape(b, seq, -1))
        xc = _merge_xattn(l, xc, seq, pa, att.reshape(t, -1), pc, g_mix, w_in_t, bg,
                          w_branch_a, w_branch_b, w_branch_c, w_out, g_x, w_xq,
                          xk.reshape(b, mlen, d), xv.reshape(b, mlen, d), w_xo)
        xc = _ffn(l, xc, g_ffn, w_ff1, w_ff2, final_norm_g.reshape(1, d), l == DEPTH - 1)
    return xc.reshape(b, seq, d)
```

```python
import functools
import math

import numpy as np
import jax
import jax.numpy as jnp
from jax import lax
from jax.experimental import pallas as pl
from jax.experimental.pallas import tpu as pltpu

D_MODEL = 1024
DEPTH = 2
POOL_GROUP_DIM = 64
POOL_WIDTH = 256
POOL_WINDOWS = (2, 4, 8, 16)
POOL_HALO = 16
FOX_HEADS = 8
FOX_HEAD_DIM = 64
FOX_WIDTH = 512
SGU_WIDTH = 256
SGU_GROUPS = 4
SGU_CHUNK = 128
XATTN_HEADS = 4
XATTN_HEAD_DIM = 256
D_FF = 4096
EPS = 1e-6
NEG = -1e30
LOG2E = math.log2(math.e)

LANES = 128
AUG_LANES = 8
QNORM_LANE = 6
KNORM_LANE = 7
NORM_CAP = 1e30
HEAD_PAIRS = FOX_HEADS // 2
PAIR_W = 2 * LANES

OFF_A = 0
OFF_Q = OFF_A + POOL_WIDTH
OFF_K = OFF_Q + FOX_WIDTH
OFF_V = OFF_K + FOX_WIDTH
OFF_F = OFF_V + FOX_WIDTH
OFF_C = OFF_F + FOX_HEADS
OFF_G = OFF_C + 2 * SGU_WIDTH
N_IN = OFF_G + 3 * D_MODEL

TM = 512
TM_MIX = 1024
TQ = 256
VMEM_LIMIT = 56 * 1024 * 1024

_F32 = jnp.float32
_BF16 = jnp.bfloat16


def _dot(a, b):
    return jnp.dot(a, b, preferred_element_type=_F32)


def _dot_t(a, b):
    return lax.dot_general(a, b, (((1,), (1,)), ((), ())), preferred_element_type=_F32)


def _rms(x, g):
    ms = jnp.mean(x * x, axis=-1, keepdims=True)
    return x * lax.rsqrt(ms + EPS) * g


def _rms_factored(x, g):
    ms = jnp.mean(x * x, axis=-1, keepdims=True)
    return (x * g).astype(_BF16), lax.rsqrt(ms + EPS)


def _const_spec(shape):
    nd = len(shape)
    return pl.BlockSpec(shape, lambda *_: (0,) * nd, pipeline_mode=pl.Buffered(1))


def _layer_spec(tail, l):
    nd = len(tail)
    return pl.BlockSpec((None,) + tuple(tail), lambda *_: (l,) + (0,) * nd, pipeline_mode=pl.Buffered(1))


def _params(sem):
    return pltpu.CompilerParams(dimension_semantics=sem, vmem_limit_bytes=VMEM_LIMIT)


def _mem_kv_kernel(mem_ref, g_ref, w_ref, xk_ref, xv_ref):
    hm, r = _rms_factored(mem_ref[...], g_ref[...])
    xk_ref[...] = (r * _dot(hm, w_ref[:, :D_MODEL].astype(_BF16))).astype(_BF16)
    xv_ref[...] = (r * _dot(hm, w_ref[:, D_MODEL:].astype(_BF16))).astype(_BF16)


def _mem_kv(l, mem2d, g, w_xkv):
    rows = mem2d.shape[0]
    return pl.pallas_call(
        _mem_kv_kernel,
        grid=(rows // TM,),
        in_specs=[pl.BlockSpec((TM, D_MODEL), lambda i: (i, 0)),
                  _layer_spec((1, D_MODEL), l),
                  _layer_spec((D_MODEL, 2 * D_MODEL), l)],
        out_specs=[pl.BlockSpec((TM, D_MODEL), lambda i: (i, 0))] * 2,
        out_shape=[jax.ShapeDtypeStruct((rows, D_MODEL), _BF16)] * 2,
        compiler_params=_params(("arbitrary",)),
        name="mem_kv",
    )(mem2d, g, w_xkv)


def _mix_in_kernel(tm, tiles_per_seq,
                   x_ref, g_ref, w_ref, bf_ref, wpool_ref, pscale_ref, sgug_ref, sguw_ref,
                   bmat_ref, e_ref, en_ref, cq_ref, ck_ref,
                   pa_ref, q2_ref, k2_ref, v_ref, pc_ref,
                   halo_ref, fcarry_ref):
    jj = pl.program_id(0) % tiles_per_seq

    @pl.when(jj == 0)
    def _():
        halo_ref[...] = jnp.zeros_like(halo_ref)
        fcarry_ref[...] = jnp.zeros_like(fcarry_ref)

    h = _rms(x_ref[...], g_ref[...]).astype(_BF16)

    wf = jnp.concatenate([w_ref[OFF_F:OFF_C, :], jnp.zeros((LANES - FOX_HEADS, D_MODEL), _F32)], axis=0)
    z = _dot_t(h, wf.astype(_BF16)) + bf_ref[...]
    logf = jnp.minimum(z, 0.0) - jnp.log1p(jnp.exp(-jnp.abs(z)))
    row = lax.broadcasted_iota(jnp.int32, (tm, LANES), 0)
    c = logf
    shift = 1
    while shift < tm:
        c = c + jnp.where(row >= shift, pltpu.roll(c, shift, 0), 0.0)
        shift *= 2
    fcum = (c + fcarry_ref[...]) * LOG2E
    fcarry_ref[...] = fcarry_ref[...] + jnp.sum(logf, axis=0, keepdims=True)
    hi = fcum.astype(_BF16).astype(_F32)
    r1 = fcum - hi
    mid = r1.astype(_BF16).astype(_F32)
    lo = r1 - mid
    lane = lax.broadcasted_iota(jnp.int32, (tm, LANES), 1)
    packed = jnp.where(lane < FOX_HEADS, hi,
                       jnp.where(lane < 2 * FOX_HEADS, pltpu.roll(mid, FOX_HEADS, 1),
                                 jnp.where(lane < 3 * FOX_HEADS, pltpu.roll(lo, 2 * FOX_HEADS, 1), 0.0)))

    a = _dot_t(h, w_ref[OFF_A:OFF_Q, :].astype(_BF16))
    cz = _dot_t(h, w_ref[OFF_C:OFF_G, :].astype(_BF16))
    ext = jnp.concatenate([halo_ref[...], a], axis=0)
    halo_ref[...] = a[tm - POOL_HALO:, :]
    w2 = ext + pltpu.roll(ext, 1, 0)
    w4 = w2 + pltpu.roll(w2, 2, 0)
    w8 = w4 + pltpu.roll(w4, 4, 0)
    w16 = w8 + pltpu.roll(w8, 8, 0)
    grp = lax.broadcasted_iota(jnp.int32, (tm, POOL_WIDTH), 1) // POOL_GROUP_DIM
    wsel = jnp.where(grp == 0, w2[POOL_HALO:],
                     jnp.where(grp == 1, w4[POOL_HALO:],
                               jnp.where(grp == 2, w8[POOL_HALO:], w16[POOL_HALO:])))
    pos = jj * tm + lax.broadcasted_iota(jnp.int32, (tm, POOL_WIDTH), 0)
    win = jnp.left_shift(2, grp)
    cnt = jnp.minimum(pos + 1, win).astype(_F32)
    d = (wsel / cnt - a).astype(_BF16)

    q = _dot_t(h, w_ref[OFF_Q:OFF_K, :].astype(_BF16)) * (FOX_HEAD_DIM ** -0.5 * LOG2E)
    k = _dot_t(h, w_ref[OFF_K:OFF_V, :].astype(_BF16))
    v_ref[...] = _dot_t(h, w_ref[OFF_V:OFF_F, :].astype(_BF16)).astype(_BF16)
    qb = q.astype(_BF16)
    kb = k.astype(_BF16)
    for p in range(HEAD_PAIRS):
        q2_ref[:, p * PAIR_W:p * PAIR_W + LANES] = qb[:, p * LANES:(p + 1) * LANES]
        k2_ref[:, p * PAIR_W:p * PAIR_W + LANES] = kb[:, p * LANES:(p + 1) * LANES]
    qsq = (q * q).astype(_BF16)
    ksq = (k * k).astype(_BF16)

    pa_ref[...] = (_dot(d, wpool_ref[...]) * pscale_ref[...]).astype(_BF16)
    zc = cz * (0.5 * (1.0 + jnp.tanh(math.sqrt(2.0 / math.pi) * (cz + 0.044715 * (cz * cz * cz)))))
    u = zc[:, :SGU_WIDTH]
    vv = _rms(zc[:, SGU_WIDTH:], sgug_ref[...])
    wrow = lax.broadcasted_iota(jnp.int32, (SGU_GROUPS * SGU_CHUNK, SGU_CHUNK), 0) % SGU_CHUNK
    wcol = lax.broadcasted_iota(jnp.int32, (SGU_GROUPS * SGU_CHUNK, SGU_CHUNK), 1)
    wst = jnp.where(wcol <= wrow, sguw_ref[...], 0.0).astype(_BF16)
    cgrp = lax.broadcasted_iota(jnp.int32, (SGU_CHUNK, SGU_WIDTH), 1) // (SGU_WIDTH // SGU_GROUPS)
    for ci in range(tm // SGU_CHUNK):
        rs = slice(ci * SGU_CHUNK, (ci + 1) * SGU_CHUNK)
        r = _dot(wst, vv[rs].astype(_BF16))
        mixed = jnp.where(cgrp == 0, r[0:SGU_CHUNK],
                          jnp.where(cgrp == 1, r[SGU_CHUNK:2 * SGU_CHUNK],
                                    jnp.where(cgrp == 2, r[2 * SGU_CHUNK:3 * SGU_CHUNK],
                                              r[3 * SGU_CHUNK:])))
        pc_ref[rs, :] = (u[rs] * (mixed + bmat_ref[...])).astype(_BF16)

    aug = _dot(packed.astype(_BF16), e_ref[...])
    nq = jnp.minimum(_dot(qsq, en_ref[:, :LANES]), NORM_CAP)
    nk = jnp.minimum(_dot(ksq, en_ref[:, LANES:]), NORM_CAP)
    aq = (aug[:, :LANES] + cq_ref[...] + nq).astype(_BF16)
    ak = (aug[:, LANES:] + ck_ref[...] + nk).astype(_BF16)
    for p in range(HEAD_PAIRS):
        q2_ref[:, p * PAIR_W + LANES:(p + 1) * PAIR_W] = aq
        k2_ref[:, p * PAIR_W + LANES:(p + 1) * PAIR_W] = ak


def _mix_in(l, x2d, seq, g, w_in_t, bf, wpool, pscale, sgug, sguw, bmat, e, en, cq, ck):
    t = x2d.shape[0]
    tm = TM_MIX
    row_spec = lambda w: pl.BlockSpec((tm, w), lambda i: (i, 0))
    return pl.pallas_call(
        functools.partial(_mix_in_kernel, tm, seq // tm),
        grid=(t // tm,),
        in_specs=[row_spec(D_MODEL),
                  _layer_spec((1, D_MODEL), l),
                  _layer_spec((OFF_G, D_MODEL), l),
                  _layer_spec((1, LANES), l),
                  _layer_spec((POOL_WIDTH, POOL_WIDTH), l),
                  _layer_spec((1, POOL_WIDTH), l),
                  _layer_spec((1, SGU_WIDTH), l),
                  _layer_spec((SGU_GROUPS * SGU_CHUNK, SGU_CHUNK), l),
                  _layer_spec((SGU_CHUNK, SGU_WIDTH), l),
                  _const_spec((LANES, 2 * LANES)),
                  _const_spec((FOX_WIDTH, 2 * LANES)),
                  _const_spec((1, LANES)),
                  _const_spec((1, LANES))],
        out_specs=[row_spec(POOL_WIDTH), row_spec(HEAD_PAIRS * PAIR_W), row_spec(HEAD_PAIRS * PAIR_W),
                   row_spec(FOX_WIDTH), row_spec(SGU_WIDTH)],
        out_shape=[jax.ShapeDtypeStruct((t, POOL_WIDTH), _BF16),
                   jax.ShapeDtypeStruct((t, HEAD_PAIRS * PAIR_W), _BF16),
                   jax.ShapeDtypeStruct((t, HEAD_PAIRS * PAIR_W), _BF16),
                   jax.ShapeDtypeStruct((t, FOX_WIDTH), _BF16),
                   jax.ShapeDtypeStruct((t, SGU_WIDTH), _BF16)],
        scratch_shapes=[pltpu.VMEM((POOL_HALO, POOL_WIDTH), _F32),
                        pltpu.VMEM((1, LANES), _F32)],
        compiler_params=_params(("arbitrary",)),
        name="mix_in",
    )(x2d, g, w_in_t, bf, wpool, pscale, sgug, sguw, bmat, e, en, cq, ck)


FOX_LOGIT_BOUND = 64.0
NORM_SLACK = 1.1
COLMAX_SPLIT = 8


def _aug_colmax(ref):
    rows = ref.shape[0] // COLMAX_SPLIT
    parts = [ref[r * rows:(r + 1) * rows, LANES:] for r in range(COLMAX_SPLIT)]
    while len(parts) > 1:
        parts = [jnp.maximum(a, b) for a, b in zip(parts[::2], parts[1::2])]
    return jnp.max(parts[0], axis=0, keepdims=True).astype(_F32)


def _pair_logit_bound2(q2_ref, k2_ref, pair):
    qmax, kmax = _aug_colmax(q2_ref), _aug_colmax(k2_ref)
    prod = qmax * pltpu.roll(kmax, LANES - (KNORM_LANE - QNORM_LANE), 1)
    lane = lax.broadcasted_iota(jnp.int32, prod.shape, 1)
    first = 2 * pair * AUG_LANES + QNORM_LANE
    return jnp.max(jnp.where((lane == first) | (lane == first + AUG_LANES), prod, 0.0))


def _fox_kernel(seq, q2_ref, k2_ref, v_ref, o_ref, s_ref, p_ref, vext_ref):
    pair = pl.program_id(1)
    lane = lax.broadcasted_iota(jnp.int32, (TQ, PAIR_W), 1)
    keep = []
    for hh in range(2):
        aug_lo = LANES + (2 * pair + hh) * AUG_LANES
        keep.append(((lane >= hh * FOX_HEAD_DIM) & (lane < (hh + 1) * FOX_HEAD_DIM))
                    | ((lane >= aug_lo) & (lane < aug_lo + AUG_LANES)))
    drow = lax.broadcasted_iota(jnp.int32, (2 * TQ, TQ), 0) % TQ
    dcol = lax.broadcasted_iota(jnp.int32, (2 * TQ, TQ), 1)
    causal = dcol <= drow
    out_lane = lax.broadcasted_iota(jnp.int32, (TQ, LANES), 1)

    def stacked_q(i):
        qb = q2_ref[i * TQ:(i + 1) * TQ, :]
        zero = jnp.zeros_like(qb)
        return jnp.concatenate([jnp.where(keep[0], qb, zero), jnp.where(keep[1], qb, zero)], axis=0)

    def store_heads(i, o):
        o_ref[i * TQ:(i + 1) * TQ, :] = jnp.where(out_lane < FOX_HEAD_DIM, o[:TQ], o[TQ:]).astype(_BF16)

    bounded = _pair_logit_bound2(q2_ref, k2_ref, pair) <= FOX_LOGIT_BOUND * FOX_LOGIT_BOUND / NORM_SLACK

    @pl.when(bounded)
    def _():
        vext_ref[:, :LANES] = v_ref[...]
        vext_ref[:, LANES:] = jnp.ones((seq, LANES), _BF16)

        def probabilities(i):
            qm = stacked_q(i)
            for c in range(i + 1):
                sc = _dot_t(qm, k2_ref[c * TQ:(c + 1) * TQ, :])
                if c == i:
                    sc = jnp.where(causal, sc, NEG)
                p_ref[i % 2, :, c * TQ:(c + 1) * TQ] = jnp.exp2(sc).astype(_BF16)

        n_blocks = seq // TQ
        probabilities(0)
        for i in range(n_blocks):
            if i + 1 < n_blocks:
                probabilities(i + 1)
            kend = (i + 1) * TQ
            o = _dot(p_ref[i % 2, :, :kend], vext_ref[:kend, :])
            store_heads(i, o[:, :LANES] * (1.0 / o[:, LANES:]))

    @pl.when(jnp.logical_not(bounded))
    def _():
        for i in range(seq // TQ):
            kend = (i + 1) * TQ
            s_ref[:, :kend] = _dot_t(stacked_q(i), k2_ref[:kend, :])

            mrun = jnp.full((2 * TQ, LANES), NEG, _F32)
            for c in range(i + 1):
                sc = s_ref[:, c * TQ:(c + 1) * TQ]
                if c == i:
                    sc = jnp.where(causal, sc, NEG)
                    s_ref[:, c * TQ:(c + 1) * TQ] = sc
                mrun = jnp.maximum(mrun, jnp.maximum(sc[:, :LANES], sc[:, LANES:]))
            mb = jnp.broadcast_to(jnp.max(mrun, axis=1, keepdims=True), (2 * TQ, LANES))

            lrun = jnp.zeros((2 * TQ, LANES), _F32)
            for c in range(i + 1):
                p0 = jnp.exp2(s_ref[:, c * TQ:c * TQ + LANES] - mb)
                p1 = jnp.exp2(s_ref[:, c * TQ + LANES:(c + 1) * TQ] - mb)
                lrun = lrun + (p0 + p1)
                p_ref[0, :, c * TQ:c * TQ + LANES] = p0.astype(_BF16)
                p_ref[0, :, c * TQ + LANES:(c + 1) * TQ] = p1.astype(_BF16)
            linv = 1.0 / jnp.sum(lrun, axis=1, keepdims=True)
            store_heads(i, _dot(p_ref[0, :, :kend], v_ref[:kend, :]) * linv)


def _fox(q2, k2, v):
    b, seq, _ = q2.shape
    return pl.pallas_call(
        functools.partial(_fox_kernel, seq),
        grid=(b, HEAD_PAIRS),
        in_specs=[pl.BlockSpec((None, seq, PAIR_W), lambda bi, p: (bi, 0, p)),
                  pl.BlockSpec((None, seq, PAIR_W), lambda bi, p: (bi, 0, p)),
                  pl.BlockSpec((None, seq, LANES), lambda bi, p: (bi, 0, p))],
        out_specs=pl.BlockSpec((None, seq, LANES), lambda bi, p: (bi, 0, p)),
        out_shape=jax.ShapeDtypeStruct((b, seq, FOX_WIDTH), _BF16),
        scratch_shapes=[pltpu.VMEM((2 * TQ, seq), _F32),
                        pltpu.VMEM((2, 2 * TQ, seq), _BF16),
                        pltpu.VMEM((seq, 2 * LANES), _BF16)],
        compiler_params=_params(("arbitrary", "arbitrary")),
        name="fox_attention",
    )(q2, k2, v)


MERGE_CHUNK = 512


def _merge_xattn_kernel(x_ref, pa_ref, att_ref, pc_ref, gmix_ref, wg_ref, bg_ref,
                        wba_ref, wbb_ref, wbc_ref, wout_ref, gx_ref, wxq_ref,
                        xk_ref, xv_ref, wxo_ref, o_ref, merged_ref):
    x = x_ref[...]
    h, r = _rms_factored(x, gmix_ref[...])
    branches = ((pa_ref, wba_ref), (att_ref, wbb_ref), (pc_ref, wbc_ref))
    for n in range(D_MODEL // MERGE_CHUNK):
        cols = slice(n * MERGE_CHUNK, (n + 1) * MERGE_CHUNK)
        acc = None
        for j, (b_ref, wb_ref) in enumerate(branches):
            gcols = slice(j * D_MODEL + n * MERGE_CHUNK, j * D_MODEL + (n + 1) * MERGE_CHUNK)
            zg = r * _dot_t(h, wg_ref[gcols, :].astype(_BF16)) + bg_ref[:, gcols]
            gate = 0.5 * jnp.tanh(0.5 * zg) + 0.5
            term = gate * _dot(b_ref[...], wb_ref[:, cols].astype(_BF16))
            acc = term if acc is None else acc + term
        merged_ref[:, cols] = acc.astype(_BF16)
    x1 = x + _dot(merged_ref[...], wout_ref[...].astype(_BF16))

    hx, r1 = _rms_factored(x1, gx_ref[...])
    qscale = r1 * (XATTN_HEAD_DIM ** -0.5 * LOG2E)
    head_cols = [slice(hd * XATTN_HEAD_DIM, (hd + 1) * XATTN_HEAD_DIM) for hd in range(XATTN_HEADS)]
    xq, s, o = {}, {}, {}
    y = x1
    for t in range(XATTN_HEADS + 3):
        if t < XATTN_HEADS:
            xq[t] = (_dot(hx, wxq_ref[:, head_cols[t]].astype(_BF16)) * qscale).astype(_BF16)
        if 0 <= t - 1 < XATTN_HEADS:
            s[t - 1] = _dot_t(xq[t - 1], xk_ref[:, head_cols[t - 1]])
        if 0 <= t - 2 < XATTN_HEADS:
            p = jnp.exp2(s[t - 2] - jnp.max(s[t - 2], axis=1, keepdims=True))
            linv = 1.0 / jnp.sum(p, axis=1, keepdims=True)
            o[t - 2] = (_dot(p.astype(_BF16), xv_ref[:, head_cols[t - 2]]) * linv).astype(_BF16)
        if 0 <= t - 3 < XATTN_HEADS:
            y = y + _dot(o[t - 3], wxo_ref[head_cols[t - 3], :].astype(_BF16))
    o_ref[...] = y


def _merge_xattn(l, x2d, seq, pa, att, pc, gmix, w_in_t, b_gate, wba, wbb, wbc, w_out, gx, w_xq, xk, xv, w_xo):
    t = x2d.shape[0]
    mlen = xk.shape[1]
    tiles_per_seq = seq // TM
    row_spec = lambda w: pl.BlockSpec((TM, w), lambda i: (i, 0))
    mem_spec = pl.BlockSpec((None, mlen, D_MODEL), lambda i: (i // tiles_per_seq, 0, 0))
    return pl.pallas_call(
        _merge_xattn_kernel,
        grid=(t // TM,),
        in_specs=[row_spec(D_MODEL), row_spec(POOL_WIDTH), row_spec(FOX_WIDTH), row_spec(SGU_WIDTH),
                  _layer_spec((1, D_MODEL), l),
                  pl.BlockSpec((None, pl.Element(3 * D_MODEL), pl.Element(D_MODEL)),
                               lambda i: (l, OFF_G, 0), pipeline_mode=pl.Buffered(1)),
                  _layer_spec((1, 3 * D_MODEL), l),
                  _layer_spec((POOL_WIDTH, D_MODEL), l),
                  _layer_spec((FOX_WIDTH, D_MODEL), l),
                  _layer_spec((SGU_WIDTH, D_MODEL), l),
                  _layer_spec((D_MODEL, D_MODEL), l),
                  _layer_spec((1, D_MODEL), l),
                  _layer_spec((D_MODEL, D_MODEL), l),
                  mem_spec, mem_spec,
                  _layer_spec((D_MODEL, D_MODEL), l)],
        out_specs=row_spec(D_MODEL),
        out_shape=jax.ShapeDtypeStruct((t, D_MODEL), _F32),
        scratch_shapes=[pltpu.VMEM((TM, D_MODEL), _BF16)],
        compiler_params=_params(("arbitrary",)),
        name="merge_xattn",
    )(x2d, pa, att, pc, gmix, w_in_t, b_gate, wba, wbb, wbc, w_out, gx, w_xq, xk, xv, w_xo)


FF_CHUNK = 1024


def _ffn_kernel(final_norm, x_ref, g_ref, w1_ref, w2_ref, gfin_ref, o_ref, act_ref):
    x = x_ref[...]
    hf, r = _rms_factored(x, g_ref[...])
    for c in range(D_FF // FF_CHUNK):
        cols = slice(c * FF_CHUNK, (c + 1) * FF_CHUNK)
        u = jnp.maximum(_dot(hf, w1_ref[:, cols].astype(_BF16)), 0.0)
        act_ref[:, cols] = (u * u).astype(_BF16)
    y = x + (r * r) * _dot(act_ref[...], w2_ref[...].astype(_BF16))
    if final_norm:
        y = _rms(y, gfin_ref[...])
    o_ref[...] = y


def _ffn(l, x2d, g, w1, w2, gfin, final_norm):
    t = x2d.shape[0]
    return pl.pallas_call(
        functools.partial(_ffn_kernel, final_norm),
        grid=(t // TM,),
        in_specs=[pl.BlockSpec((TM, D_MODEL), lambda i: (i, 0)),
                  _layer_spec((1, D_MODEL), l),
                  _layer_spec((D_MODEL, D_FF), l),
                  _layer_spec((D_FF, D_MODEL), l),
                  _const_spec((1, D_MODEL))],
        out_specs=pl.BlockSpec((TM, D_MODEL), lambda i: (i, 0)),
        out_shape=jax.ShapeDtypeStruct((t, D_MODEL), _F32),
        scratch_shapes=[pltpu.VMEM((TM, D_FF), _BF16)],
        compiler_params=_params(("arbitrary",)),
        name="ffn",
    )(x2d, g, w1, w2, gfin)


def _aug_constants():
    e = np.zeros((LANES, 2 * LANES), np.float32)
    en = np.zeros((FOX_WIDTH, 2 * LANES), np.float32)
    cq = np.zeros((1, LANES), np.float32)
    ck = np.zeros((1, LANES), np.float32)
    for h in range(FOX_HEADS):
        for piece in range(3):
            e[piece * FOX_HEADS + h, h * AUG_LANES + piece] = 1.0
            e[piece * FOX_HEADS + h, LANES + h * AUG_LANES + 3 + piece] = -1.0
            cq[0, h * AUG_LANES + 3 + piece] = 1.0
            ck[0, h * AUG_LANES + piece] = 1.0
        cols = slice(h * FOX_HEAD_DIM, (h + 1) * FOX_HEAD_DIM)
        en[cols, h * AUG_LANES + QNORM_LANE] = 1.0
        en[cols, LANES + h * AUG_LANES + KNORM_LANE] = 1.0
    return jnp.asarray(e, _BF16), jnp.asarray(en, _BF16), jnp.asarray(cq), jnp.asarray(ck)


def kernel(x, mem, norm_mix_g, w_in, b_forget, pool_w, pool_scale, sgu_norm_g, sgu_w, sgu_b,
           w_branch_a, w_branch_b, w_branch_c, b_gate, w_out, norm_xattn_g, norm_mem_g,
           w_xq, w_xkv, w_xo, norm_ffn_g, w_ff1, w_ff2, final_norm_g):
    b, seq, d = x.shape
    mlen = mem.shape[1]
    t = b * seq
    e, en, cq, ck = _aug_constants()
    rows = lambda a: a.reshape(a.shape[0], 1, -1)
    n_groups = len(POOL_WINDOWS)

    w_in_t = jnp.swapaxes(w_in, 1, 2)
    bf = rows(jnp.pad(b_forget, ((0, 0), (0, LANES - FOX_HEADS))))
    eye = jnp.eye(n_groups, dtype=pool_w.dtype)
    wpool = jnp.einsum('lgcd,gh->lgchd', pool_w, eye).reshape(DEPTH, POOL_WIDTH, POOL_WIDTH).astype(_BF16)
    sguw = sgu_w.reshape(DEPTH, SGU_GROUPS * SGU_CHUNK, SGU_CHUNK)
    bmat = jnp.repeat(jnp.swapaxes(sgu_b, 1, 2), SGU_WIDTH // SGU_GROUPS, axis=2)
    g_mix, g_mem, g_x, g_ffn = rows(norm_mix_g), rows(norm_mem_g), rows(norm_xattn_g), rows(norm_ffn_g)
    pscale, sgug, bg = rows(pool_scale), rows(sgu_norm_g), rows(b_gate)

    xc = x.reshape(t, d)
    mem2d = mem.reshape(b * mlen, d)
    for l in range(DEPTH):
        xk, xv = _mem_kv(l, mem2d, g_mem, w_xkv)
        pa, q2, k2, v, pc = _mix_in(l, xc, seq, g_mix, w_in_t, bf, wpool, pscale, sgug, sguw, bmat, e, en, cq, ck)
        att = _fox(q2.reshape(b, seq, -1), k2.reshape(b, seq, -1), v.reshape(b, seq, -1))
        xc = _merge_xattn(l, xc, seq, pa, att.reshape(t, -1), pc, g_mix, w_in_t, bg,
                          w_branch_a, w_branch_b, w_branch_c, w_out, g_x, w_xq,
                          xk.reshape(b, mlen, d), xv.reshape(b, mlen, d), w_xo)
        xc = _ffn(l, xc, g_ffn, w_ff1, w_ff2, final_norm_g.reshape(1, d), l == DEPTH - 1)
    return xc.reshape(b, seq, d)
```

```python
import functools
import math

import numpy as np
import jax
import jax.numpy as jnp
from jax import lax
from jax.experimental import pallas as pl
from jax.experimental.pallas import tpu as pltpu

D_MODEL = 1024
DEPTH = 2
POOL_GROUP_DIM = 64
POOL_WIDTH = 256
POOL_WINDOWS = (2, 4, 8, 16)
POOL_HALO = 16
FOX_HEADS = 8
FOX_HEAD_DIM = 64
FOX_WIDTH = 512
SGU_WIDTH = 256
SGU_GROUPS = 4
SGU_CHUNK = 128
XATTN_HEADS = 4
XATTN_HEAD_DIM = 256
D_FF = 4096
EPS = 1e-6
NEG = -1e30
LOG2E = math.log2(math.e)

LANES = 128
AUG_LANES = 8
QNORM_LANE = 6
KNORM_LANE = 7
NORM_CAP = 1e30
HEAD_PAIRS = FOX_HEADS // 2
PAIR_W = 2 * LANES

OFF_A = 0
OFF_Q = OFF_A + POOL_WIDTH
OFF_K = OFF_Q + FOX_WIDTH
OFF_V = OFF_K + FOX_WIDTH
OFF_F = OFF_V + FOX_WIDTH
OFF_C = OFF_F + FOX_HEADS
OFF_G = OFF_C + 2 * SGU_WIDTH
N_IN = OFF_G + 3 * D_MODEL

TM = 512
TM_MIX = 1024
TQ = 256
VMEM_LIMIT = 56 * 1024 * 1024

_F32 = jnp.float32
_BF16 = jnp.bfloat16


def _dot(a, b):
    return jnp.dot(a, b, preferred_element_type=_F32)


def _dot_t(a, b):
    return lax.dot_general(a, b, (((1,), (1,)), ((), ())), preferred_element_type=_F32)


def _rms(x, g):
    ms = jnp.mean(x * x, axis=-1, keepdims=True)
    return x * lax.rsqrt(ms + EPS) * g


def _rms_factored(x, g):
    ms = jnp.mean(x * x, axis=-1, keepdims=True)
    return (x * g).astype(_BF16), lax.rsqrt(ms + EPS)


def _const_spec(shape):
    nd = len(shape)
    return pl.BlockSpec(shape, lambda *_: (0,) * nd, pipeline_mode=pl.Buffered(1))


def _layer_spec(tail, l):
    nd = len(tail)
    return pl.BlockSpec((None,) + tuple(tail), lambda *_: (l,) + (0,) * nd, pipeline_mode=pl.Buffered(1))


def _params(sem):
    return pltpu.CompilerParams(dimension_semantics=sem, vmem_limit_bytes=VMEM_LIMIT)


def _mem_kv_kernel(mem_ref, g_ref, w_ref, xk_ref, xv_ref):
    hm, r = _rms_factored(mem_ref[...], g_ref[...])
    xk_ref[...] = (r * _dot(hm, w_ref[:, :D_MODEL].astype(_BF16))).astype(_BF16)
    xv_ref[...] = (r * _dot(hm, w_ref[:, D_MODEL:].astype(_BF16))).astype(_BF16)


def _mem_kv(mem2d, g, w_xkv):
    rows = mem2d.shape[0]
    depth = w_xkv.shape[0]
    out_spec = pl.BlockSpec((None, TM, D_MODEL), lambda l, i: (l, i, 0))
    return pl.pallas_call(
        _mem_kv_kernel,
        grid=(depth, rows // TM),
        in_specs=[pl.BlockSpec((TM, D_MODEL), lambda l, i: (i, 0)),
                  pl.BlockSpec((None, 1, D_MODEL), lambda l, i: (l, 0, 0)),
                  pl.BlockSpec((None, D_MODEL, 2 * D_MODEL), lambda l, i: (l, 0, 0))],
        out_specs=[out_spec, out_spec],
        out_shape=[jax.ShapeDtypeStruct((depth, rows, D_MODEL), _BF16)] * 2,
        compiler_params=_params(("arbitrary", "arbitrary")),
        name="mem_kv",
    )(mem2d, g, w_xkv)


def _mix_in_kernel(tm, tiles_per_seq,
                   x_ref, g_ref, w_ref, bf_ref, wpool_ref, pscale_ref, sgug_ref, sguw_ref,
                   bmat_ref, e_ref, en_ref, cq_ref, ck_ref,
                   pa_ref, q2_ref, k2_ref, v_ref, pc_ref,
                   halo_ref, fcarry_ref):
    jj = pl.program_id(0) % tiles_per_seq

    @pl.when(jj == 0)
    def _():
        halo_ref[...] = jnp.zeros_like(halo_ref)
        fcarry_ref[...] = jnp.zeros_like(fcarry_ref)

    h = _rms(x_ref[...], g_ref[...]).astype(_BF16)

    wf = jnp.concatenate([w_ref[OFF_F:OFF_C, :], jnp.zeros((LANES - FOX_HEADS, D_MODEL), _F32)], axis=0)
    z = _dot_t(h, wf.astype(_BF16)) + bf_ref[...]
    logf = jnp.minimum(z, 0.0) - jnp.log1p(jnp.exp(-jnp.abs(z)))
    row = lax.broadcasted_iota(jnp.int32, (tm, LANES), 0)
    c = logf
    shift = 1
    while shift < tm:
        c = c + jnp.where(row >= shift, pltpu.roll(c, shift, 0), 0.0)
        shift *= 2
    fcum = (c + fcarry_ref[...]) * LOG2E
    fcarry_ref[...] = fcarry_ref[...] + jnp.sum(logf, axis=0, keepdims=True)
    hi = fcum.astype(_BF16).astype(_F32)
    r1 = fcum - hi
    mid = r1.astype(_BF16).astype(_F32)
    lo = r1 - mid
    lane = lax.broadcasted_iota(jnp.int32, (tm, LANES), 1)
    packed = jnp.where(lane < FOX_HEADS, hi,
                       jnp.where(lane < 2 * FOX_HEADS, pltpu.roll(mid, FOX_HEADS, 1),
                                 jnp.where(lane < 3 * FOX_HEADS, pltpu.roll(lo, 2 * FOX_HEADS, 1), 0.0)))

    a = _dot_t(h, w_ref[OFF_A:OFF_Q, :].astype(_BF16))
    cz = _dot_t(h, w_ref[OFF_C:OFF_G, :].astype(_BF16))
    ext = jnp.concatenate([halo_ref[...], a], axis=0)
    halo_ref[...] = a[tm - POOL_HALO:, :]
    w2 = ext + pltpu.roll(ext, 1, 0)
    w4 = w2 + pltpu.roll(w2, 2, 0)
    w8 = w4 + pltpu.roll(w4, 4, 0)
    w16 = w8 + pltpu.roll(w8, 8, 0)
    grp = lax.broadcasted_iota(jnp.int32, (tm, POOL_WIDTH), 1) // POOL_GROUP_DIM
    wsel = jnp.where(grp == 0, w2[POOL_HALO:],
                     jnp.where(grp == 1, w4[POOL_HALO:],
                               jnp.where(grp == 2, w8[POOL_HALO:], w16[POOL_HALO:])))
    pos = jj * tm + lax.broadcasted_iota(jnp.int32, (tm, POOL_WIDTH), 0)
    win = jnp.left_shift(2, grp)
    cnt = jnp.minimum(pos + 1, win).astype(_F32)
    d = (wsel / cnt - a).astype(_BF16)

    q = _dot_t(h, w_ref[OFF_Q:OFF_K, :].astype(_BF16)) * (FOX_HEAD_DIM ** -0.5 * LOG2E)
    k = _dot_t(h, w_ref[OFF_K:OFF_V, :].astype(_BF16))
    v_ref[...] = _dot_t(h, w_ref[OFF_V:OFF_F, :].astype(_BF16)).astype(_BF16)
    qb = q.astype(_BF16)
    kb = k.astype(_BF16)
    for p in range(HEAD_PAIRS):
        q2_ref[:, p * PAIR_W:p * PAIR_W + LANES] = qb[:, p * LANES:(p + 1) * LANES]
        k2_ref[:, p * PAIR_W:p * PAIR_W + LANES] = kb[:, p * LANES:(p + 1) * LANES]
    qsq = (q * q).astype(_BF16)
    ksq = (k * k).astype(_BF16)

    pa_ref[...] = (_dot(d, wpool_ref[...]) * pscale_ref[...]).astype(_BF16)
    zc = cz * (0.5 * (1.0 + jnp.tanh(math.sqrt(2.0 / math.pi) * (cz + 0.044715 * (cz * cz * cz)))))
    u = zc[:, :SGU_WIDTH]
    vv = _rms(zc[:, SGU_WIDTH:], sgug_ref[...])
    wrow = lax.broadcasted_iota(jnp.int32, (SGU_GROUPS * SGU_CHUNK, SGU_CHUNK), 0) % SGU_CHUNK
    wcol = lax.broadcasted_iota(jnp.int32, (SGU_GROUPS * SGU_CHUNK, SGU_CHUNK), 1)
    wst = jnp.where(wcol <= wrow, sguw_ref[...], 0.0).astype(_BF16)
    cgrp = lax.broadcasted_iota(jnp.int32, (SGU_CHUNK, SGU_WIDTH), 1) // (SGU_WIDTH // SGU_GROUPS)
    for ci in range(tm // SGU_CHUNK):
        rs = slice(ci * SGU_CHUNK, (ci + 1) * SGU_CHUNK)
        r = _dot(wst, vv[rs].astype(_BF16))
        mixed = jnp.where(cgrp == 0, r[0:SGU_CHUNK],
                          jnp.where(cgrp == 1, r[SGU_CHUNK:2 * SGU_CHUNK],
                                    jnp.where(cgrp == 2, r[2 * SGU_CHUNK:3 * SGU_CHUNK],
                                              r[3 * SGU_CHUNK:])))
        pc_ref[rs, :] = (u[rs] * (mixed + bmat_ref[...])).astype(_BF16)

    aug = _dot(packed.astype(_BF16), e_ref[...])
    nq = jnp.minimum(_dot(qsq, en_ref[:, :LANES]), NORM_CAP)
    nk = jnp.minimum(_dot(ksq, en_ref[:, LANES:]), NORM_CAP)
    aq = (aug[:, :LANES] + cq_ref[...] + nq).astype(_BF16)
    ak = (aug[:, LANES:] + ck_ref[...] + nk).astype(_BF16)
    for p in range(HEAD_PAIRS):
        q2_ref[:, p * PAIR_W + LANES:(p + 1) * PAIR_W] = aq
        k2_ref[:, p * PAIR_W + LANES:(p + 1) * PAIR_W] = ak


def _mix_in(l, x2d, seq, g, w_in_t, bf, wpool, pscale, sgug, sguw, bmat, e, en, cq, ck):
    t = x2d.shape[0]
    tm = TM_MIX
    row_spec = lambda w: pl.BlockSpec((tm, w), lambda i: (i, 0))
    return pl.pallas_call(
        functools.partial(_mix_in_kernel, tm, seq // tm),
        grid=(t // tm,),
        in_specs=[row_spec(D_MODEL),
                  _layer_spec((1, D_MODEL), l),
                  _layer_spec((OFF_G, D_MODEL), l),
                  _layer_spec((1, LANES), l),
                  _layer_spec((POOL_WIDTH, POOL_WIDTH), l),
                  _layer_spec((1, POOL_WIDTH), l),
                  _layer_spec((1, SGU_WIDTH), l),
                  _layer_spec((SGU_GROUPS * SGU_CHUNK, SGU_CHUNK), l),
                  _layer_spec((SGU_CHUNK, SGU_WIDTH), l),
                  _const_spec((LANES, 2 * LANES)),
                  _const_spec((FOX_WIDTH, 2 * LANES)),
                  _const_spec((1, LANES)),
                  _const_spec((1, LANES))],
        out_specs=[row_spec(POOL_WIDTH), row_spec(HEAD_PAIRS * PAIR_W), row_spec(HEAD_PAIRS * PAIR_W),
                   row_spec(FOX_WIDTH), row_spec(SGU_WIDTH)],
        out_shape=[jax.ShapeDtypeStruct((t, POOL_WIDTH), _BF16),
                   jax.ShapeDtypeStruct((t, HEAD_PAIRS * PAIR_W), _BF16),
                   jax.ShapeDtypeStruct((t, HEAD_PAIRS * PAIR_W), _BF16),
                   jax.ShapeDtypeStruct((t, FOX_WIDTH), _BF16),
                   jax.ShapeDtypeStruct((t, SGU_WIDTH), _BF16)],
        scratch_shapes=[pltpu.VMEM((POOL_HALO, POOL_WIDTH), _F32),
                        pltpu.VMEM((1, LANES), _F32)],
        compiler_params=_params(("arbitrary",)),
        name="mix_in",
    )(x2d, g, w_in_t, bf, wpool, pscale, sgug, sguw, bmat, e, en, cq, ck)


FOX_LOGIT_BOUND = 64.0
NORM_SLACK = 1.1
COLMAX_SPLIT = 8
VT_ROWS = FOX_HEAD_DIM + 16


def _aug_colmax(ref):
    rows = ref.shape[0] // COLMAX_SPLIT
    parts = [ref[r * rows:(r + 1) * rows, LANES:] for r in range(COLMAX_SPLIT)]
    while len(parts) > 1:
        parts = [jnp.maximum(a, b) for a, b in zip(parts[::2], parts[1::2])]
    return jnp.max(parts[0], axis=0, keepdims=True).astype(_F32)


def _pair_logit_bound2(q2_ref, k2_ref, pair):
    qmax, kmax = _aug_colmax(q2_ref), _aug_colmax(k2_ref)
    prod = qmax * pltpu.roll(kmax, LANES - (KNORM_LANE - QNORM_LANE), 1)
    lane = lax.broadcasted_iota(jnp.int32, prod.shape, 1)
    first = 2 * pair * AUG_LANES + QNORM_LANE
    return jnp.max(jnp.where((lane == first) | (lane == first + AUG_LANES), prod, 0.0))


def _fox_kernel(seq, q2_ref, k2_ref, v_ref, o_ref, s_ref, p_ref, kmask_ref, pt_ref, vt_ref):
    pair = pl.program_id(1)
    lane = lax.broadcasted_iota(jnp.int32, (TQ, PAIR_W), 1)
    keep = []
    for hh in range(2):
        aug_lo = LANES + (2 * pair + hh) * AUG_LANES
        keep.append(((lane >= hh * FOX_HEAD_DIM) & (lane < (hh + 1) * FOX_HEAD_DIM))
                    | ((lane >= aug_lo) & (lane < aug_lo + AUG_LANES)))
    drow = lax.broadcasted_iota(jnp.int32, (2 * TQ, TQ), 0) % TQ
    dcol = lax.broadcasted_iota(jnp.int32, (2 * TQ, TQ), 1)
    causal = dcol <= drow
    causal_t = drow <= dcol
    out_lane = lax.broadcasted_iota(jnp.int32, (TQ, LANES), 1)

    def stacked_q(i):
        qb = q2_ref[i * TQ:(i + 1) * TQ, :]
        zero = jnp.zeros_like(qb)
        return jnp.concatenate([jnp.where(keep[0], qb, zero), jnp.where(keep[1], qb, zero)], axis=0)

    def store_heads(i, o):
        o_ref[i * TQ:(i + 1) * TQ, :] = jnp.where(out_lane < FOX_HEAD_DIM, o[:TQ], o[TQ:]).astype(_BF16)

    bounded = _pair_logit_bound2(q2_ref, k2_ref, pair) <= FOX_LOGIT_BOUND * FOX_LOGIT_BOUND / NORM_SLACK

    @pl.when(bounded)
    def _():
        n_blocks = seq // TQ
        for c in range(n_blocks):
            kb = k2_ref[c * TQ:(c + 1) * TQ, :]
            for hh in range(2):
                kmask_ref[hh, c * TQ:(c + 1) * TQ, :] = jnp.where(keep[hh], kb, jnp.zeros_like(kb))
        vt = v_ref[...].astype(_F32).T
        ones_row = lax.broadcasted_iota(jnp.int32, (VT_ROWS - FOX_HEAD_DIM, seq), 0) == 0
        for hh in range(2):
            vt_ref[hh, :FOX_HEAD_DIM, :] = vt[hh * FOX_HEAD_DIM:(hh + 1) * FOX_HEAD_DIM, :].astype(_BF16)
            vt_ref[hh, FOX_HEAD_DIM:, :] = jnp.where(ones_row, 1.0, 0.0).astype(_BF16)

        def probabilities(i):
            qb = q2_ref[i * TQ:(i + 1) * TQ, :]
            for c in range(i + 1):
                ks = kmask_ref[:, c * TQ:(c + 1) * TQ, :].reshape(2 * TQ, PAIR_W)
                st = _dot_t(ks, qb)
                if c == i:
                    st = jnp.where(causal_t, st, NEG)
                pt = jnp.exp2(st).astype(_BF16)
                for hh in range(2):
                    pt_ref[i % 2, hh, c * TQ:(c + 1) * TQ, :] = pt[hh * TQ:(hh + 1) * TQ, :]

        probabilities(0)
        for i in range(n_blocks):
            if i + 1 < n_blocks:
                probabilities(i + 1)
            kend = (i + 1) * TQ
            heads = []
            for hh in range(2):
                ot = _dot(vt_ref[hh, :, :kend], pt_ref[i % 2, hh, :kend, :])
                heads.append(ot[:FOX_HEAD_DIM] * (1.0 / ot[FOX_HEAD_DIM:FOX_HEAD_DIM + 1]))
            o_ref[i * TQ:(i + 1) * TQ, :] = jnp.concatenate(heads, axis=0).T.astype(_BF16)

    @pl.when(jnp.logical_not(bounded))
    def _():
        for i in range(seq // TQ):
            kend = (i + 1) * TQ
            s_ref[:, :kend] = _dot_t(stacked_q(i), k2_ref[:kend, :])

            mrun = jnp.full((2 * TQ, LANES), NEG, _F32)
            for c in range(i + 1):
                sc = s_ref[:, c * TQ:(c + 1) * TQ]
                if c == i:
                    sc = jnp.where(causal, sc, NEG)
                    s_ref[:, c * TQ:(c + 1) * TQ] = sc
                mrun = jnp.maximum(mrun, jnp.maximum(sc[:, :LANES], sc[:, LANES:]))
            mb = jnp.broadcast_to(jnp.max(mrun, axis=1, keepdims=True), (2 * TQ, LANES))

            lrun = jnp.zeros((2 * TQ, LANES), _F32)
            for c in range(i + 1):
                p0 = jnp.exp2(s_ref[:, c * TQ:c * TQ + LANES] - mb)
                p1 = jnp.exp2(s_ref[:, c * TQ + LANES:(c + 1) * TQ] - mb)
                lrun = lrun + (p0 + p1)
                p_ref[0, :, c * TQ:c * TQ + LANES] = p0.astype(_BF16)
                p_ref[0, :, c * TQ + LANES:(c + 1) * TQ] = p1.astype(_BF16)
            linv = 1.0 / jnp.sum(lrun, axis=1, keepdims=True)
            store_heads(i, _dot(p_ref[0, :, :kend], v_ref[:kend, :]) * linv)


def _fox(q2, k2, v):
    b, seq, _ = q2.shape
    return pl.pallas_call(
        functools.partial(_fox_kernel, seq),
        grid=(b, HEAD_PAIRS),
        in_specs=[pl.BlockSpec((None, seq, PAIR_W), lambda bi, p: (bi, 0, p)),
                  pl.BlockSpec((None, seq, PAIR_W), lambda bi, p: (bi, 0, p)),
                  pl.BlockSpec((None, seq, LANES), lambda bi, p: (bi, 0, p))],
        out_specs=pl.BlockSpec((None, seq, LANES), lambda bi, p: (bi, 0, p)),
        out_shape=jax.ShapeDtypeStruct((b, seq, FOX_WIDTH), _BF16),
        scratch_shapes=[pltpu.VMEM((2 * TQ, seq), _F32),
                        pltpu.VMEM((2, 2 * TQ, seq), _BF16),
                        pltpu.VMEM((2, seq, PAIR_W), _BF16),
                        pltpu.VMEM((2, 2, seq, TQ), _BF16),
                        pltpu.VMEM((2, VT_ROWS, seq), _BF16)],
        compiler_params=_params(("arbitrary", "arbitrary")),
        name="fox_attention",
    )(q2, k2, v)


MERGE_CHUNK = 512


def _merge_xattn_kernel(x_ref, pa_ref, att_ref, pc_ref, gmix_ref, wg_ref, bg_ref,
                        wba_ref, wbb_ref, wbc_ref, wout_ref, gx_ref, wxq_ref,
                        xk_ref, xv_ref, wxo_ref, o_ref, merged_ref):
    x = x_ref[...]
    h, r = _rms_factored(x, gmix_ref[...])
    branches = ((pa_ref, wba_ref), (att_ref, wbb_ref), (pc_ref, wbc_ref))
    for n in range(D_MODEL // MERGE_CHUNK):
        cols = slice(n * MERGE_CHUNK, (n + 1) * MERGE_CHUNK)
        acc = None
        for j, (b_ref, wb_ref) in enumerate(branches):
            gcols = slice(j * D_MODEL + n * MERGE_CHUNK, j * D_MODEL + (n + 1) * MERGE_CHUNK)
            zg = r * _dot_t(h, wg_ref[gcols, :].astype(_BF16)) + bg_ref[:, gcols]
            gate = 0.5 * jnp.tanh(0.5 * zg) + 0.5
            term = gate * _dot(b_ref[...], wb_ref[:, cols].astype(_BF16))
            acc = term if acc is None else acc + term
        merged_ref[:, cols] = acc.astype(_BF16)
    x1 = x + _dot(merged_ref[...], wout_ref[...].astype(_BF16))

    hx, r1 = _rms_factored(x1, gx_ref[...])
    qscale = r1 * (XATTN_HEAD_DIM ** -0.5 * LOG2E)
    head_cols = [slice(hd * XATTN_HEAD_DIM, (hd + 1) * XATTN_HEAD_DIM) for hd in range(XATTN_HEADS)]
    xq, s, o = {}, {}, {}
    y = x1
    for t in range(XATTN_HEADS + 3):
        if t < XATTN_HEADS:
            xq[t] = (_dot(hx, wxq_ref[:, head_cols[t]].astype(_BF16)) * qscale).astype(_BF16)
        if 0 <= t - 1 < XATTN_HEADS:
            s[t - 1] = _dot_t(xq[t - 1], xk_ref[:, head_cols[t - 1]])
        if 0 <= t - 2 < XATTN_HEADS:
            p = jnp.exp2(s[t - 2] - jnp.max(s[t - 2], axis=1, keepdims=True))
            linv = 1.0 / jnp.sum(p, axis=1, keepdims=True)
            o[t - 2] = (_dot(p.astype(_BF16), xv_ref[:, head_cols[t - 2]]) * linv).astype(_BF16)
        if 0 <= t - 3 < XATTN_HEADS:
            y = y + _dot(o[t - 3], wxo_ref[head_cols[t - 3], :].astype(_BF16))
    o_ref[...] = y


def _merge_xattn(l, x2d, seq, pa, att, pc, gmix, w_in_t, b_gate, wba, wbb, wbc, w_out, gx, w_xq, xk, xv, w_xo):
    t = x2d.shape[0]
    mlen = xk.shape[2]
    tiles_per_seq = seq // TM
    row_spec = lambda w: pl.BlockSpec((TM, w), lambda i: (i, 0))
    mem_spec = pl.BlockSpec((None, None, mlen, D_MODEL), lambda i: (l, i // tiles_per_seq, 0, 0))
    return pl.pallas_call(
        _merge_xattn_kernel,
        grid=(t // TM,),
        in_specs=[row_spec(D_MODEL), row_spec(POOL_WIDTH), row_spec(FOX_WIDTH), row_spec(SGU_WIDTH),
                  _layer_spec((1, D_MODEL), l),
                  pl.BlockSpec((None, pl.Element(3 * D_MODEL), pl.Element(D_MODEL)),
                               lambda i: (l, OFF_G, 0), pipeline_mode=pl.Buffered(1)),
                  _layer_spec((1, 3 * D_MODEL), l),
                  _layer_spec((POOL_WIDTH, D_MODEL), l),
                  _layer_spec((FOX_WIDTH, D_MODEL), l),
                  _layer_spec((SGU_WIDTH, D_MODEL), l),
                  _layer_spec((D_MODEL, D_MODEL), l),
                  _layer_spec((1, D_MODEL), l),
                  _layer_spec((D_MODEL, D_MODEL), l),
                  mem_spec, mem_spec,
                  _layer_spec((D_MODEL, D_MODEL), l)],
        out_specs=row_spec(D_MODEL),
        out_shape=jax.ShapeDtypeStruct((t, D_MODEL), _F32),
        scratch_shapes=[pltpu.VMEM((TM, D_MODEL), _BF16)],
        compiler_params=_params(("arbitrary",)),
        name="merge_xattn",
    )(x2d, pa, att, pc, gmix, w_in_t, b_gate, wba, wbb, wbc, w_out, gx, w_xq, xk, xv, w_xo)


FF_CHUNK = 1024


def _ffn_kernel(final_norm, x_ref, g_ref, w1_ref, w2_ref, gfin_ref, o_ref, act_ref):
    x = x_ref[...]
    hf, r = _rms_factored(x, g_ref[...])
    for c in range(D_FF // FF_CHUNK):
        cols = slice(c * FF_CHUNK, (c + 1) * FF_CHUNK)
        u = jnp.maximum(_dot(hf, w1_ref[:, cols].astype(_BF16)), 0.0)
        act_ref[:, cols] = (u * u).astype(_BF16)
    y = x + (r * r) * _dot(act_ref[...], w2_ref[...].astype(_BF16))
    if final_norm:
        y = _rms(y, gfin_ref[...])
    o_ref[...] = y


def _ffn(l, x2d, g, w1, w2, gfin, final_norm):
    t = x2d.shape[0]
    return pl.pallas_call(
        functools.partial(_ffn_kernel, final_norm),
        grid=(t // TM,),
        in_specs=[pl.BlockSpec((TM, D_MODEL), lambda i: (i, 0)),
                  _layer_spec((1, D_MODEL), l),
                  _layer_spec((D_MODEL, D_FF), l),
                  _layer_spec((D_FF, D_MODEL), l),
                  _const_spec((1, D_MODEL))],
        out_specs=pl.BlockSpec((TM, D_MODEL), lambda i: (i, 0)),
        out_shape=jax.ShapeDtypeStruct((t, D_MODEL), _F32),
        scratch_shapes=[pltpu.VMEM((TM, D_FF), _BF16)],
        compiler_params=_params(("arbitrary",)),
        name="ffn",
    )(x2d, g, w1, w2, gfin)


def _aug_constants():
    e = np.zeros((LANES, 2 * LANES), np.float32)
    en = np.zeros((FOX_WIDTH, 2 * LANES), np.float32)
    cq = np.zeros((1, LANES), np.float32)
    ck = np.zeros((1, LANES), np.float32)
    for h in range(FOX_HEADS):
        for piece in range(3):
            e[piece * FOX_HEADS + h, h * AUG_LANES + piece] = 1.0
            e[piece * FOX_HEADS + h, LANES + h * AUG_LANES + 3 + piece] = -1.0
            cq[0, h * AUG_LANES + 3 + piece] = 1.0
            ck[0, h * AUG_LANES + piece] = 1.0
        cols = slice(h * FOX_HEAD_DIM, (h + 1) * FOX_HEAD_DIM)
        en[cols, h * AUG_LANES + QNORM_LANE] = 1.0
        en[cols, LANES + h * AUG_LANES + KNORM_LANE] = 1.0
    return jnp.asarray(e, _BF16), jnp.asarray(en, _BF16), jnp.asarray(cq), jnp.asarray(ck)


def kernel(x, mem, norm_mix_g, w_in, b_forget, pool_w, pool_scale, sgu_norm_g, sgu_w, sgu_b,
           w_branch_a, w_branch_b, w_branch_c, b_gate, w_out, norm_xattn_g, norm_mem_g,
           w_xq, w_xkv, w_xo, norm_ffn_g, w_ff1, w_ff2, final_norm_g):
    b, seq, d = x.shape
    mlen = mem.shape[1]
    t = b * seq
    e, en, cq, ck = _aug_constants()
    rows = lambda a: a.reshape(a.shape[0], 1, -1)
    n_groups = len(POOL_WINDOWS)

    w_in_t = jnp.swapaxes(w_in, 1, 2)
    bf = rows(jnp.pad(b_forget, ((0, 0), (0, LANES - FOX_HEADS))))
    eye = jnp.eye(n_groups, dtype=pool_w.dtype)
    wpool = jnp.einsum('lgcd,gh->lgchd', pool_w, eye).reshape(DEPTH, POOL_WIDTH, POOL_WIDTH).astype(_BF16)
    sguw = sgu_w.reshape(DEPTH, SGU_GROUPS * SGU_CHUNK, SGU_CHUNK)
    bmat = jnp.repeat(jnp.swapaxes(sgu_b, 1, 2), SGU_WIDTH // SGU_GROUPS, axis=2)
    g_mix, g_mem, g_x, g_ffn = rows(norm_mix_g), rows(norm_mem_g), rows(norm_xattn_g), rows(norm_ffn_g)
    pscale, sgug, bg = rows(pool_scale), rows(sgu_norm_g), rows(b_gate)

    xc = x.reshape(t, d)
    mem2d = mem.reshape(b * mlen, d)
    xk, xv = _mem_kv(mem2d, g_mem, w_xkv)
    xk, xv = xk.reshape(DEPTH, b, mlen, d), xv.reshape(DEPTH, b, mlen, d)
    for l in range(DEPTH):
        pa, q2, k2, v, pc = _mix_in(l, xc, seq, g_mix, w_in_t, bf, wpool, pscale, sgug, sguw, bmat, e, en, cq, ck)
        att = _fox(q2.reshape(b, seq, -1), k2.reshape(b, seq, -1), v.reshape(b, seq, -1))
        xc = _merge_xattn(l, xc, seq, pa, att.reshape(t, -1), pc, g_mix, w_in_t, bg,
                          w_branch_a, w_branch_b, w_branch_c, w_out, g_x, w_xq,
                          xk, xv, w_xo)
        xc = _ffn(l, xc, g_ffn, w_ff1, w_ff2, final_norm_g.reshape(1, d), l == DEPTH - 1)
    return xc.reshape(b, seq, d)
```
